```python
import math
import jax, jax.numpy as jnp
from jax import lax
import numpy as np

D_MODEL = 4096
BATCH = 2
SEQ = 8192
DEPTH = 2
DEC_BATCH = 8
DEC_SEQ = 64
PAST_LEN = 2048

CHUNK = 64
D_FF = 4 * D_MODEL
ADA_DIM = 6 * D_MODEL
NORM_EPS = 1e-6
DK_A = 128
DV_A = 128
H_A = (D_MODEL // 2) // DV_A
A_COLS = 2 * H_A * DK_A + 2 * H_A * DV_A
N_B = 64
W_B = D_MODEL - H_A * DV_A
H_B = W_B // N_B
LORA_W = max(32, int(round(1.8 * D_MODEL ** 0.5 / 32)) * 32)
LORA_A = max(32, int(round(1.8 * D_MODEL ** 0.5 / 32)) * 32)
LORA_G = max(32, int(round(0.6 * D_MODEL ** 0.8 / 32)) * 32)
B_COLS = 3 * W_B + LORA_W + LORA_A + LORA_G
RWKV_LN_EPS = 64e-5
AB_COLS = A_COLS + B_COLS
D_MIX_AB = H_A * DV_A + W_B
DK_C = 128
DV_C = 128
H_C = D_MODEL // DK_C
CONV_W = 4
C_QKV = 2 * H_C * DK_C + H_C * DV_C
C_COLS = C_QKV + H_C * DV_C + 2 * H_C
N_AB = (DEPTH + 1) // 2
N_C = DEPTH // 2

kernel_name = 'hybrid_stream_hgrn2_rwkv7_gdn_step'


def _rms_norm(x, eps=NORM_EPS):
    xf = x.astype(jnp.float32)
    return (xf * lax.rsqrt(jnp.mean(xf * xf, axis=-1, keepdims=True) + eps)).astype(x.dtype)


def _l2norm(x, eps=1e-12):
    xf = x.astype(jnp.float32)
    return xf * lax.rsqrt(jnp.sum(xf * xf, axis=-1, keepdims=True) + eps)


def _to_blocks(t, bs):
    b, l, h, d = t.shape
    return t.astype(jnp.float32).reshape(b, l // bs, bs, h, d).transpose(1, 0, 3, 2, 4)


def _from_blocks(t):
    nb, b, h, bs, d = t.shape
    return t.transpose(1, 0, 3, 2, 4).reshape(b, nb * bs, h, d)


def _hgrn2_chunked(q, k, v, log_f, s0):
    bs = math.gcd(q.shape[1], CHUNK)
    incl = jnp.tril(jnp.ones((bs, bs), dtype=bool))

    def step(S, blk):
        qb, kb, vb, gb = blk
        G = jnp.cumsum(gb, axis=2)
        rel = jnp.where(incl[:, :, None], G[:, :, :, None, :] - G[:, :, None, :, :], -jnp.inf)
        scores = jnp.einsum('bhtk,bhtsk,bhsk->bhts', qb, jnp.exp(rel), kb)
        o = jnp.einsum('bhts,bhsv->bhtv', scores, vb) + jnp.einsum('bhtk,bhkv->bhtv', qb * jnp.exp(G), S)
        g_last = G[:, :, -1:, :]
        S = jnp.exp(g_last[:, :, 0, :, None]) * S + jnp.einsum('bhsk,bhsv->bhkv', kb * jnp.exp(g_last - G), vb)
        return S, o

    S, o = lax.scan(step, s0.astype(jnp.float32),
                    (_to_blocks(q, bs), _to_blocks(k, bs), _to_blocks(v, bs), _to_blocks(log_f, bs)))
    return _from_blocks(o), S


def _rwkv7_scan(r, decay, k, v, a, b, s0):
    def step(S, inp):
        rt, wt, kt, vt, at, bt = inp
        sa = jnp.einsum('bhvk,bhk->bhv', S, at)
        S = S * wt[:, :, None, :] + sa[..., None] * bt[:, :, None, :] + vt[..., None] * kt[:, :, None, :]
        return S, jnp.einsum('bhvk,bhk->bhv', S, rt)

    seq = lambda t: jnp.moveaxis(t.astype(jnp.float32), 1, 0)
    S, o = lax.scan(step, s0.astype(jnp.float32), (seq(r), seq(decay), seq(k), seq(v), seq(a), seq(b)))
    return jnp.moveaxis(o, 0, 1), S


def _gated_delta_chunked(q, k, v, log_a, beta, s0):
    bs = math.gcd(q.shape[1], CHUNK)
    dv = v.shape[-1]
    incl = jnp.tril(jnp.ones((bs, bs), dtype=bool))
    strict = jnp.tril(jnp.ones((bs, bs), dtype=jnp.float32), -1)
    eye = jnp.eye(bs, dtype=jnp.float32)

    def step(S, blk):
        qb, kb, vb, gb, bb = blk
        G = jnp.cumsum(gb, axis=2)
        decay = jnp.exp(jnp.where(incl, G - jnp.swapaxes(G, -1, -2), -jnp.inf))
        a_mat = bb * jnp.einsum('bhtk,bhsk->bhts', kb, kb) * decay * strict
        rhs = jnp.concatenate([bb * vb, bb * jnp.exp(G) * kb], axis=-1)
        uw = lax.linalg.triangular_solve(a_mat + eye, rhs, left_side=True, lower=True, unit_diagonal=True)
        u, w = uw[..., :dv], uw[..., dv:]
        v_new = u - jnp.einsum('bhtk,bhkv->bhtv', w, S)
        att = jnp.einsum('bhtk,bhsk->bhts', qb, kb) * decay
        o = jnp.einsum('bhtk,bhkv->bhtv', qb * jnp.exp(G), S) + jnp.einsum('bhts,bhsv->bhtv', att, v_new)
        g_last = G[:, :, -1:, :]
        S = jnp.exp(g_last) * S + jnp.einsum('bhsk,bhsv->bhkv', kb * jnp.exp(g_last - G), v_new)
        return S, o

    S, o = lax.scan(step, s0.astype(jnp.float32),
                    (_to_blocks(q, bs), _to_blocks(k, bs), _to_blocks(v, bs),
                     _to_blocks(log_a, bs), _to_blocks(beta, bs)))
    return _from_blocks(o), S


def _causal_conv(x, buf, w):
    xp = jnp.concatenate([buf.astype(x.dtype), x], axis=1)
    y = lax.conv_general_dilated(xp, w[:, None, :].astype(x.dtype), (1,), 'VALID',
                                 dimension_numbers=('NWC', 'WIO', 'NWC'), feature_group_count=x.shape[-1])
    return y, xp[:, -(CONV_W - 1):]


def _ab_mixer(h, s_hgrn, s_rwkv, s_shift, lb, w_in, w_out, hgrn_norm_w, mu, w0, w2, a0, a2, g2,
              k_k, k_a, r_k, ln_w, ln_b):
    bsz, L, _ = h.shape
    dt = h.dtype
    f32 = jnp.float32
    hd = lambda t, d: t.reshape(bsz, L, -1, d)
    p = h @ w_in
    q, f, i, g = jnp.split(p[..., :A_COLS], [H_A * DK_A, 2 * H_A * DK_A, 2 * H_A * DK_A + H_A * DV_A], axis=-1)
    f = lb + (1.0 - lb) * jax.nn.sigmoid(f.astype(f32))
    o_a, s_hgrn_new = _hgrn2_chunked(hd(jax.nn.silu(q) * DK_A ** -0.5, DK_A), hd(1.0 - f, DK_A),
                                     hd(i, DV_A), hd(jnp.log(f), DK_A), s_hgrn)
    o_a = _rms_norm(o_a) * hgrn_norm_w * jax.nn.silu(hd(g, DV_A).astype(f32))
    pb = p[..., A_COLS:]
    prev = jnp.concatenate([s_shift.astype(dt), pb], axis=1)
    xs = pb + (prev[:, :-1] - pb) * mu
    new_shift = prev[:, -1:]
    r, k, v, xw, xa, xg = jnp.split(xs, [W_B, 2 * W_B, 3 * W_B, 3 * W_B + LORA_W, 3 * W_B + LORA_W + LORA_A], axis=-1)
    w = -jax.nn.softplus(-(w0 + jnp.tanh(xw) @ w2).astype(f32)) - 0.5
    decay = jnp.exp(-jnp.exp(w))
    a = jax.nn.sigmoid((a0 + xa @ a2).astype(f32))
    gate = jax.nn.sigmoid(xg) @ g2
    kk = _l2norm(hd(k * k_k, N_B))
    k = k.astype(f32) * (1.0 + (a - 1.0) * k_a)
    r4, k4, v4, a4 = hd(r.astype(f32), N_B), hd(k, N_B), hd(v.astype(f32), N_B), hd(a, N_B)
    o_b, s_rwkv_new = _rwkv7_scan(r4, hd(decay, N_B), k4, v4, -kk, kk * a4, s_rwkv)
    mean = jnp.mean(o_b, axis=-1, keepdims=True)
    var = jnp.mean(jnp.square(o_b - mean), axis=-1, keepdims=True)
    o_b = (o_b - mean) * lax.rsqrt(var + RWKV_LN_EPS) * ln_w.reshape(H_B, N_B) + ln_b.reshape(H_B, N_B)
    o_b = o_b + jnp.sum(r4 * k4 * r_k, axis=-1, keepdims=True) * v4
    o_b = o_b.reshape(bsz, L, W_B) * gate
    mix = jnp.concatenate([o_a.reshape(bsz, L, -1).astype(dt), o_b.astype(dt)], axis=-1) @ w_out
    return mix, s_hgrn_new, s_rwkv_new, new_shift


def _gdn_mixer(h, s_gdn, s_conv, w_in, w_out, conv_w, a_log, dt_bias, norm_w):
    bsz, L, _ = h.shape
    f32 = jnp.float32
    p = h @ w_in
    qkv, z, b, a = jnp.split(p, [C_QKV, C_QKV + H_C * DV_C, C_QKV + H_C * DV_C + H_C], axis=-1)
    qkv, new_conv = _causal_conv(qkv, s_conv, conv_w)
    q, k, v = jnp.split(jax.nn.silu(qkv), [H_C * DK_C, 2 * H_C * DK_C], axis=-1)
    q = _l2norm(q.reshape(bsz, L, H_C, DK_C)) * DK_C ** -0.5
    k = _l2norm(k.reshape(bsz, L, H_C, DK_C))
    beta = jax.nn.sigmoid(b.astype(f32))[..., None]
    log_a = (-jnp.exp(a_log.astype(f32)) * jax.nn.softplus(a.astype(f32) + dt_bias))[..., None]
    o, s_new = _gated_delta_chunked(q, k, v.reshape(bsz, L, H_C, DV_C), log_a, beta, s_gdn)
    o = _rms_norm(o) * norm_w * jax.nn.silu(z.reshape(bsz, L, H_C, DV_C).astype(f32))
    return o.reshape(bsz, L, -1).astype(h.dtype) @ w_out, s_new, new_conv


def _trunk(x, c, s_hgrn, s_rwkv, s_shift, s_gdn, s_conv,
           ada_w, ada_b, mlp_w1, mlp_w2, final_norm_w, ab_w_in, ab_w_out, hgrn_lb_logits, hgrn_norm_w,
           rwkv_mu, rwkv_w0, rwkv_w2, rwkv_a0, rwkv_a2, rwkv_g2, rwkv_k_k, rwkv_k_a, rwkv_r_k,
           rwkv_ln_w, rwkv_ln_b, gdn_w_in, gdn_w_out, gdn_conv_w, gdn_a_log, gdn_dt_bias, gdn_norm_w):
    lbs = jnp.cumsum(jax.nn.softmax(hgrn_lb_logits.astype(jnp.float32), axis=0), axis=0)
    cond = jax.nn.silu(c)
    new_hgrn, new_rwkv, new_shift, new_gdn, new_conv = [], [], [], [], []
    for layer in range(DEPTH):
        j = layer // 2
        mod = (cond @ ada_w[layer] + ada_b[layer])[:, None, :]
        sh1, sc1, g1, sh2, sc2, g2 = jnp.split(mod, 6, axis=-1)
        h = _rms_norm(x) * (1.0 + sc1) + sh1
        if layer % 2 == 0:
            mix, sa, sb, ss = _ab_mixer(h, s_hgrn[j], s_rwkv[j], s_shift[j], lbs[j], ab_w_in[j], ab_w_out[j],
                                        hgrn_norm_w[j], rwkv_mu[j], rwkv_w0[j], rwkv_w2[j], rwkv_a0[j],
                                        rwkv_a2[j], rwkv_g2[j], rwkv_k_k[j], rwkv_k_a[j], rwkv_r_k[j],
                                        rwkv_ln_w[j], rwkv_ln_b[j])
            new_hgrn.append(sa)
            new_rwkv.append(sb)
            new_shift.append(ss)
        else:
            mix, sg, sc = _gdn_mixer(h, s_gdn[j], s_conv[j], gdn_w_in[j], gdn_w_out[j], gdn_conv_w[j],
                                     gdn_a_log[j], gdn_dt_bias[j], gdn_norm_w[j])
            new_gdn.append(sg)
            new_conv.append(sc)
        x = x + g1 * mix
        h = _rms_norm(x) * (1.0 + sc2) + sh2
        x = x + g2 * (jnp.square(jax.nn.relu(h @ mlp_w1[layer])) @ mlp_w2[layer])
    y = _rms_norm(x) * final_norm_w
    dt = x.dtype
    return (y, jnp.stack(new_hgrn).astype(dt), jnp.stack(new_rwkv).astype(dt), jnp.stack(new_shift).astype(dt),
            jnp.stack(new_gdn).astype(dt), jnp.stack(new_conv).astype(dt))


def setup_inputs(seed: int = 0) -> dict:
    key = jax.random.key(seed)
    ks = iter(jax.random.split(key, 48))
    nrm = lambda shape, scale: scale * jax.random.normal(next(ks), shape, jnp.float32)
    uni = lambda shape, lo, hi: jax.random.uniform(next(ks), shape, jnp.float32, minval=lo, maxval=hi)
    dt_init = jnp.exp(uni((N_C, H_C), math.log(1e-3), math.log(0.1)))
    return {
        'x_prompt': nrm((BATCH, SEQ, D_MODEL), 1.0),
        'x_sample': nrm((DEC_BATCH, DEC_SEQ, D_MODEL), 1.0),
        'c_prompt': nrm((BATCH, D_MODEL), 1.0),
        'c_sample': nrm((DEC_BATCH, D_MODEL), 1.0),
        'state_hgrn': nrm((N_AB, DEC_BATCH, H_A, DK_A, DV_A), 0.5),
        'state_rwkv': nrm((N_AB, DEC_BATCH, H_B, N_B, N_B), 0.5),
        'state_rwkv_shift': nrm((N_AB, DEC_BATCH, 1, B_COLS), 1.0),
        'state_gdn': nrm((N_C, DEC_BATCH, H_C, DK_C, DV_C), 0.5),
        'state_gdn_conv': nrm((N_C, DEC_BATCH, CONV_W - 1, C_QKV), 1.0),
        'ada_w': nrm((DEPTH, D_MODEL, ADA_DIM), 0.5 * D_MODEL ** -0.5),
        'ada_b': nrm((DEPTH, ADA_DIM), 0.02),
        'mlp_w1': nrm((DEPTH, D_MODEL, D_FF), D_MODEL ** -0.5),
        'mlp_w2': nrm((DEPTH, D_FF, D_MODEL), D_FF ** -0.5),
        'final_norm_w': 1.0 + nrm((D_MODEL,), 0.1),
        'ab_w_in': nrm((N_AB, D_MODEL, AB_COLS), D_MODEL ** -0.5),
        'ab_w_out': nrm((N_AB, D_MIX_AB, D_MODEL), D_MIX_AB ** -0.5),
        'hgrn_lb_logits': nrm((N_AB + 1, H_A * DK_A), 1.0),
        'hgrn_norm_w': 1.0 + nrm((N_AB, DV_A), 0.1),
        'rwkv_mu': uni((N_AB, B_COLS), 0.0, 1.0),
        'rwkv_w0': uni((N_AB, W_B), -6.0, -1.0),
        'rwkv_w2': nrm((N_AB, LORA_W, W_B), LORA_W ** -0.5),
        'rwkv_a0': nrm((N_AB, W_B), 0.5),
        'rwkv_a2': nrm((N_AB, LORA_A, W_B), LORA_A ** -0.5),
        'rwkv_g2': nrm((N_AB, LORA_G, W_B), LORA_G ** -0.5),
        'rwkv_k_k': 1.0 + nrm((N_AB, W_B), 0.1),
        'rwkv_k_a': 1.0 + nrm((N_AB, W_B), 0.1),
        'rwkv_r_k': nrm((N_AB, H_B, N_B), 0.1),
        'rwkv_ln_w': 1.0 + nrm((N_AB, W_B), 0.1),
        'rwkv_ln_b': nrm((N_AB, W_B), 0.02),
        'gdn_w_in': nrm((N_C, D_MODEL, C_COLS), D_MODEL ** -0.5),
        'gdn_w_out': nrm((N_C, H_C * DV_C, D_MODEL), (H_C * DV_C) ** -0.5),
        'gdn_conv_w': nrm((N_C, CONV_W, C_QKV), CONV_W ** -0.5),
        'gdn_a_log': jnp.log(uni((N_C, H_C), 1.0, 16.0)),
        'gdn_dt_bias': dt_init + jnp.log(-jnp.expm1(-dt_init)),
        'gdn_norm_w': 1.0 + nrm((N_C, DV_C), 0.1),
    }


def reference(x_prompt, x_sample, c_prompt, c_sample, state_hgrn, state_rwkv, state_rwkv_shift, state_gdn,
              state_gdn_conv, ada_w, ada_b, mlp_w1, mlp_w2, final_norm_w, ab_w_in, ab_w_out, hgrn_lb_logits,
              hgrn_norm_w, rwkv_mu, rwkv_w0, rwkv_w2, rwkv_a0, rwkv_a2, rwkv_g2, rwkv_k_k, rwkv_k_a, rwkv_r_k,
              rwkv_ln_w, rwkv_ln_b, gdn_w_in, gdn_w_out, gdn_conv_w, gdn_a_log, gdn_dt_bias, gdn_norm_w):
    weights = (ada_w, ada_b, mlp_w1, mlp_w2, final_norm_w, ab_w_in, ab_w_out, hgrn_lb_logits, hgrn_norm_w,
               rwkv_mu, rwkv_w0, rwkv_w2, rwkv_a0, rwkv_a2, rwkv_g2, rwkv_k_k, rwkv_k_a, rwkv_r_k,
               rwkv_ln_w, rwkv_ln_b, gdn_w_in, gdn_w_out, gdn_conv_w, gdn_a_log, gdn_dt_bias, gdn_norm_w)
    nb = x_prompt.shape[0]
    zeros = lambda s: jnp.zeros((s.shape[0], nb) + s.shape[2:], x_prompt.dtype)
    y_p, hg_p, rw_p, sh_p, gd_p, cv_p = _trunk(x_prompt, c_prompt, zeros(state_hgrn), zeros(state_rwkv),
                                               zeros(state_rwkv_shift), zeros(state_gdn), zeros(state_gdn_conv),
                                               *weights)
    y_s, hg_s, rw_s, sh_s, gd_s, cv_s = _trunk(x_sample, c_sample, state_hgrn, state_rwkv, state_rwkv_shift,
                                               state_gdn, state_gdn_conv, *weights)
    return (y_p, y_s, hg_p, rw_p, sh_p, gd_p, cv_p, hg_s, rw_s, sh_s, gd_s, cv_s)
```

```python
import functools
import math

import jax
import jax.numpy as jnp
from jax import lax
from jax.experimental import pallas as pl
from jax.experimental.pallas import tpu as pltpu

F32 = jnp.float32
BF16 = jnp.bfloat16
HI = lax.Precision.HIGHEST

NORM_EPS = 1e-6
RWKV_LN_EPS = 64e-5
CHUNK = 64
DIAG = 8
LANES = 128
VMEM_LIMIT = 56 * 1024 * 1024
PROMPT_TILES = (1024, 512, 256, 256)
SAMPLE_TILES = (512, 64, 64, 64)


def _mm(a, b, ca, cb, exact=False):
    dims = (((ca,), (cb,)), ((), ()))
    if exact:
        return lax.dot_general(a, b, dims, precision=HI, preferred_element_type=F32)
    return lax.dot_general(a.astype(BF16), b.astype(BF16), dims, preferred_element_type=F32)


def _nn(a, b, exact=False):
    return _mm(a, b, 1, 0, exact)


def _nt(a, b, exact=False):
    return _mm(a, b, 1, 1, exact)


def _tn(a, b, exact=False):
    return _mm(a, b, 0, 0, exact)


def _sigmoid(x):
    return 1.0 / (1.0 + jnp.exp(-x))


def _silu(x):
    return x * _sigmoid(x)


def _softplus(x):
    return jnp.maximum(x, 0.0) + jnp.log(1.0 + jnp.exp(-jnp.abs(x)))


def _iota(shape, dim):
    return lax.broadcasted_iota(jnp.int32, shape, dim)


def _neumann_inverse(m, steps):
    n = m.shape[0]
    eye = (_iota((n, n), 0) == _iota((n, n), 1)).astype(F32)
    p = eye + m
    for _ in range(steps):
        m = _nn(m, m, exact=True)
        p = p + _nn(p, m, exact=True)
    return p


def _params(sem, vmem=VMEM_LIMIT):
    return pltpu.CompilerParams(dimension_semantics=sem, vmem_limit_bytes=vmem)


def _rows_per_tile(rows_per_batch, tile):
    if tile <= rows_per_batch:
        assert rows_per_batch % tile == 0
        return 1
    assert tile % rows_per_batch == 0
    return tile // rows_per_batch


def _adaln_kernel(c_ref, w_ref, b_ref, o_ref):
    c = _silu(c_ref[...]).astype(BF16)
    o_ref[...] = jnp.dot(c, w_ref[...].astype(BF16), preferred_element_type=F32) + b_ref[...]


def _adaln(c, ada_w, ada_b):
    depth, d, n = ada_w.shape
    rows = c.shape[0]
    tn = 512 if n % 512 == 0 else n
    return pl.pallas_call(
        _adaln_kernel,
        grid=(depth, n // tn),
        in_specs=[pl.BlockSpec((rows, d), lambda l, j: (0, 0)),
                  pl.BlockSpec((None, d, tn), lambda l, j: (l, 0, j)),
                  pl.BlockSpec((None, 1, tn), lambda l, j: (l, 0, j))],
        out_specs=pl.BlockSpec((None, rows, tn), lambda l, j: (l, 0, j)),
        out_shape=jax.ShapeDtypeStruct((depth, rows, n), F32),
        compiler_params=_params(("parallel", "parallel")),
        name="adaln",
    )(c, ada_w, ada_b.reshape(depth, 1, n))


def _modulated_norm(x_ref, sc_ref, sh_ref, h_ref, groups):
    rows = x_ref.shape[0] // groups
    step = min(rows, 128)

    for gi in range(groups):
        sc = 1.0 + sc_ref[gi]
        sh = sh_ref[gi]

        def body(i, carry, gi=gi, sc=sc, sh=sh):
            sl = pl.ds(pl.multiple_of(gi * rows + i * step, step), step)
            x = x_ref[sl, :]
            ms = jnp.mean(x * x, axis=-1, keepdims=True)
            h_ref[sl, :] = (x * lax.rsqrt(ms + NORM_EPS) * sc + sh).astype(BF16)
            return carry

        lax.fori_loop(0, rows // step, body, 0)


def _norm_mm_kernel(x_ref, sc_ref, sh_ref, w_ref, o_ref, h_ref, *, groups):
    @pl.when(pl.program_id(1) == 0)
    def _():
        _modulated_norm(x_ref, sc_ref, sh_ref, h_ref, groups)

    o_ref[...] = jnp.dot(h_ref[...], w_ref[...], preferred_element_type=F32).astype(o_ref.dtype)


def _norm_matmul(x, scale, shift, w, rows_per_batch, tm, tn):
    m, d = x.shape
    n = w.shape[1]
    tm = min(tm, m)
    groups = _rows_per_tile(rows_per_batch, tm)
    bidx = (lambda i, j: (i * tm // rows_per_batch, 0, 0)) if groups == 1 else (lambda i, j: (i, 0, 0))
    return pl.pallas_call(
        functools.partial(_norm_mm_kernel, groups=groups),
        grid=(m // tm, n // tn),
        in_specs=[pl.BlockSpec((tm, d), lambda i, j: (i, 0), pipeline_mode=pl.Buffered(1)),
                  pl.BlockSpec((groups, 1, d), bidx),
                  pl.BlockSpec((groups, 1, d), bidx),
                  pl.BlockSpec((d, tn), lambda i, j: (0, j))],
        out_specs=pl.BlockSpec((tm, tn), lambda i, j: (i, j)),
        out_shape=jax.ShapeDtypeStruct((m, n), F32),
        scratch_shapes=[pltpu.VMEM((tm, d), BF16)],
        compiler_params=_params(("parallel", "arbitrary")),
        name="norm_matmul",
    )(x, scale, shift, w)


def _out_res_kernel(a_ref, w_ref, x_ref, g_ref, o_ref, *, groups):
    acc = jnp.dot(a_ref[...], w_ref[...], preferred_element_type=F32)
    rows = acc.shape[0] // groups
    for gi in range(groups):
        sl = slice(gi * rows, (gi + 1) * rows)
        o_ref[sl, :] = x_ref[sl, :] + g_ref[gi] * acc[sl, :]


def _out_residual(a, w, x, gate, rows_per_batch, tm, tn):
    m, k = a.shape
    n = w.shape[1]
    tm = min(tm, m)
    groups = _rows_per_tile(rows_per_batch, tm)
    bidx = (lambda i, j: (i * tm // rows_per_batch, 0, j)) if groups == 1 else (lambda i, j: (i, 0, j))
    return pl.pallas_call(
        functools.partial(_out_res_kernel, groups=groups),
        grid=(m // tm, n // tn),
        in_specs=[pl.BlockSpec((tm, k), lambda i, j: (i, 0)),
                  pl.BlockSpec((k, tn), lambda i, j: (0, j)),
                  pl.BlockSpec((tm, tn), lambda i, j: (i, j)),
                  pl.BlockSpec((groups, 1, tn), bidx)],
        out_specs=pl.BlockSpec((tm, tn), lambda i, j: (i, j)),
        out_shape=jax.ShapeDtypeStruct((m, n), F32),
        compiler_params=_params(("parallel", "parallel")),
        name="out_residual",
    )(a, w, x, gate)


def _mlp_kernel(x_ref, sc_ref, sh_ref, g_ref, w1_ref, w2_ref, fw_ref, o_ref, h_ref, *, groups, final_norm, tn2):
    f = pl.program_id(1)

    @pl.when(f == 0)
    def _():
        _modulated_norm(x_ref, sc_ref, sh_ref, h_ref, groups)

    u = jnp.dot(h_ref[...], w1_ref[...], preferred_element_type=F32)
    u = jnp.square(jnp.maximum(u, 0.0)).astype(BF16)
    d = o_ref.shape[1]
    for n0 in range(0, d, tn2):
        part = jnp.dot(u, w2_ref[:, n0:n0 + tn2], preferred_element_type=F32)

        @pl.when(f == 0)
        def _(part=part, n0=n0):
            o_ref[:, n0:n0 + tn2] = part

        @pl.when(f != 0)
        def _(part=part, n0=n0):
            o_ref[:, n0:n0 + tn2] += part

    @pl.when(f == pl.num_programs(1) - 1)
    def _():
        rows = x_ref.shape[0] // groups
        step = min(rows, 128)
        for gi in range(groups):
            g = g_ref[gi]

            def body(i, carry, gi=gi, g=g):
                sl = pl.ds(pl.multiple_of(gi * rows + i * step, step), step)
                y = x_ref[sl, :] + g * o_ref[sl, :]
                if final_norm:
                    ms = jnp.mean(y * y, axis=-1, keepdims=True)
                    y = y * lax.rsqrt(ms + NORM_EPS) * fw_ref[...]
                o_ref[sl, :] = y
                return carry

            lax.fori_loop(0, rows // step, body, 0)


def _mlp(x, scale, shift, gate, w1, w2, final_w, rows_per_batch, tm, tf, final_norm):
    m, d = x.shape
    ff = w1.shape[1]
    tm = min(tm, m)
    groups = _rows_per_tile(rows_per_batch, tm)
    bidx = (lambda i, f: (i * tm // rows_per_batch, 0, 0)) if groups == 1 else (lambda i, f: (i, 0, 0))
    vec = pl.BlockSpec((groups, 1, d), bidx)
    return pl.pallas_call(
        functools.partial(_mlp_kernel, groups=groups, final_norm=final_norm, tn2=min(d, 1024)),
        grid=(m // tm, ff // tf),
        in_specs=[pl.BlockSpec((tm, d), lambda i, f: (i, 0), pipeline_mode=pl.Buffered(1)),
                  vec, vec, vec,
                  pl.BlockSpec((d, tf), lambda i, f: (0, f)),
                  pl.BlockSpec((tf, d), lambda i, f: (f, 0)),
                  pl.BlockSpec((1, d), lambda i, f: (0, 0))],
        out_specs=pl.BlockSpec((tm, d), lambda i, f: (i, 0)),
        out_shape=jax.ShapeDtypeStruct((m, d), F32),
        scratch_shapes=[pltpu.VMEM((tm, d), BF16)],
        compiler_params=_params(("parallel", "arbitrary")),
        name="mlp",
    )(x, scale, shift, gate, w1, w2, final_w)


def _hgrn_consts(c):
    row = _iota((c, c), 0)
    col = _iota((c, c), 1)
    ltri = (row >= col).astype(F32)
    levels = []
    gs = 2 * DIAG
    while gs <= c:
        half = gs // 2
        pair = (row // gs == col // gs) & (row % gs >= half) & (col % gs < half)
        levels.append((gs, pair))
        gs *= 2
    rix = _iota((c, 1), 0)
    return ltri, levels, rix


def _hgrn_chunk(q, k, v, g, st, consts):
    ltri, levels, rix = consts
    c = q.shape[0]
    gc = _nn(ltri, g, exact=True)
    o = _nt(q * jnp.exp(gc), st)
    att = jnp.zeros((c, c), F32)
    for gs, pair in levels:
        half = gs // 2
        ref = jnp.concatenate(
            [jnp.broadcast_to(gc[m0 * gs + half - 1:m0 * gs + half, :], (gs, gc.shape[1])) for m0 in range(c // gs)],
            axis=0)
        is_q = (rix % gs) >= half
        qt = jnp.where(is_q, q * jnp.exp(jnp.minimum(gc - ref, 0.0)), 0.0)
        kt = jnp.where(is_q, 0.0, k * jnp.exp(jnp.minimum(ref - gc, 0.0)))
        att = att + jnp.where(pair, _nt(qt, kt), 0.0)
    o = o + _nn(att, v)
    for dist in range(DIAG):
        if dist == 0:
            kd, gd, vd = k, gc, v
        else:
            kd = pltpu.roll(k, dist, 0)
            gd = pltpu.roll(gc, dist, 0)
            vd = pltpu.roll(v, dist, 0)
        w = jnp.sum(q * kd * jnp.exp(jnp.minimum(gc - gd, 0.0)), axis=-1, keepdims=True)
        o = o + jnp.where((rix % DIAG) >= dist, w, 0.0) * vd
    gl = gc[c - 1:c, :]
    st_new = st * jnp.exp(gl) + _tn(v, k * jnp.exp(gl - gc))
    return o, st_new


def _hgrn_kernel(q_ref, f_ref, i_ref, g_ref, lb_ref, nw_ref, s0_ref, o_ref, so_ref, s_ref, *, dk):
    l = pl.program_id(2)

    @pl.when(l == 0)
    def _():
        s_ref[...] = s0_ref[...]

    t = q_ref.shape[0]
    c = min(CHUNK, t)
    consts = _hgrn_consts(c)
    lb = lb_ref[...]
    nw = nw_ref[...]

    def body(ci, carry):
        sl = pl.ds(pl.multiple_of(ci * c, c), c)
        f = lb + (1.0 - lb) * _sigmoid(f_ref[sl, :])
        q = _silu(q_ref[sl, :]) * dk ** -0.5
        o, st = _hgrn_chunk(q, 1.0 - f, i_ref[sl, :], jnp.log(f), s_ref[...], consts)
        s_ref[...] = st
        ms = jnp.mean(o * o, axis=-1, keepdims=True)
        o_ref[sl, :] = (o * lax.rsqrt(ms + NORM_EPS) * nw * _silu(g_ref[sl, :])).astype(o_ref.dtype)
        return carry

    lax.fori_loop(0, t // c, body, 0)

    @pl.when(l == pl.num_programs(2) - 1)
    def _():
        so_ref[...] = s_ref[...]


def _hgrn(p, lb, norm_w, s0t, batch, seq, heads, tile):
    dk = LANES
    tile = min(tile, seq)
    nl = seq // tile
    blk = lambda off: pl.BlockSpec((tile, dk), lambda b, h, l, off=off: (b * nl + l, off * heads + h))
    st_spec = pl.BlockSpec((None, None, dk, dk), lambda b, h, l: (b, h, 0, 0))
    return pl.pallas_call(
        functools.partial(_hgrn_kernel, dk=dk),
        grid=(batch, heads, nl),
        in_specs=[blk(0), blk(1), blk(2), blk(3),
                  pl.BlockSpec((1, dk), lambda b, h, l: (0, h)),
                  pl.BlockSpec((1, dk), lambda b, h, l: (0, 0)),
                  st_spec],
        out_specs=[pl.BlockSpec((tile, dk), lambda b, h, l: (b * nl + l, h)), st_spec],
        out_shape=[jax.ShapeDtypeStruct((batch * seq, heads * dk), BF16),
                   jax.ShapeDtypeStruct(s0t.shape, F32)],
        scratch_shapes=[pltpu.VMEM((dk, dk), F32)],
        compiler_params=_params(("parallel", "parallel", "arbitrary")),
        name="hgrn2",
    )(p, p, p, p, lb, norm_w, s0t)


def _rwkv_consts(c):
    n = 4 * c
    row = _iota((n, n), 0)
    col = _iota((n, n), 1)
    same_head = ((row // c) % 2) == ((col // c) % 2)
    strict = (row % c) > (col % c)
    incl = (row % c) >= (col % c)
    keep = same_head & (strict | ((row >= 2 * c) & incl))
    r1 = _iota((c, c), 0)
    c1 = _iota((c, c), 1)
    ltri = (r1 >= c1).astype(F32)
    head0 = _iota((1, LANES), 1) < (LANES // 2)
    rs = _iota((LANES, LANES), 0)
    cs = _iota((LANES, LANES), 1)
    bdiag = (rs // (LANES // 2)) == (cs // (LANES // 2))
    return keep, ltri, head0, bdiag


def _rwkv_chunk(r, lg, k, v, a, b, s, consts):
    keep, ltri, head0, bdiag = consts
    c = r.shape[0]
    gc = _nn(ltri, lg, exact=True)
    e_incl = jnp.exp(gc)
    e_inv = jnp.exp(-gc)
    at = a * jnp.exp(gc - lg)
    rt = r * e_incl
    bt = b * e_inv
    kt = k * e_inv
    split = lambda x: [jnp.where(head0, x, 0.0), jnp.where(head0, 0.0, x)]
    lhs = jnp.concatenate(split(at) + split(rt), axis=0)
    rhs = jnp.concatenate([bt, bt, kt, kt], axis=0)
    p = jnp.where(keep, _nt(lhs, rhs), 0.0)
    ls = _nt(lhs, s)
    vst = jnp.concatenate(split(v), axis=0)
    pv = _nn(p[:, 2 * c:], vst)
    inv = _neumann_inverse(p[:2 * c, :2 * c], int(math.log2(c)) - 1)
    ust = _nn(inv, ls[:2 * c] + pv[:2 * c], exact=True)
    ost = ls[2 * c:] + pv[2 * c:] + _nn(p[2 * c:, :2 * c], ust)
    o = ost[:c] + ost[c:]
    u = ust[:c] + ust[c:]
    gl = gc[c - 1:c, :]
    e_tail = jnp.exp(gl - gc)
    s_new = s * jnp.exp(gl) + jnp.where(bdiag, _tn(u, b * e_tail) + _tn(v, k * e_tail), 0.0)
    return o, s_new


def _rwkv_kernel(xr_ref, xk_ref, xv_ref, xw_ref, xa_ref, xg_ref,
                 hr_ref, hk_ref, hv_ref, hw_ref, ha_ref, hg_ref,
                 vec_ref, mus_ref, w2_ref, a2_ref, g2_ref, s0_ref,
                 o_ref, so_ref,
                 s_ref, buf_ref, bufg_ref, r_ref, lg_ref, k_ref, v_ref, a_ref, b_ref, y_ref):
    l = pl.program_id(2)
    t = xr_ref.shape[0]
    c = min(CHUNK, t)
    pad = 8
    streams = ((xr_ref, hr_ref), (xk_ref, hk_ref), (xv_ref, hv_ref), (xw_ref, hw_ref), (xa_ref, ha_ref))

    @pl.when(l == 0)
    def _():
        s_ref[...] = s0_ref[...]
        for i, (_, h_ref) in enumerate(streams):
            buf_ref[i, pad - 1:pad, :] = h_ref[...]
        bufg_ref[pad - 1:pad, :] = hg_ref[...]

    def shifted(i, x_ref, mu):
        x = x_ref[...]
        buf_ref[i, pad:pad + t, :] = x
        prev = buf_ref[i, pad - 1:pad - 1 + t, :]
        buf_ref[i, pad - 1:pad, :] = x[t - 1:t, :]
        return x + (prev - x) * mu

    vec = vec_ref[...]
    mus = mus_ref[...]
    r = shifted(0, xr_ref, vec[0:1])
    k = shifted(1, xk_ref, vec[1:2])
    v = shifted(2, xv_ref, vec[2:3])
    xw = shifted(3, xw_ref, mus[0:1, :LANES])
    xa = shifted(4, xa_ref, mus[1:2, :LANES])
    xg = xg_ref[...]
    bufg_ref[pad:pad + t, :] = xg
    prev_g = bufg_ref[pad - 1:pad - 1 + t, :]
    bufg_ref[pad - 1:pad, :] = xg[t - 1:t, :]
    xg = xg + (prev_g - xg) * mus[2:3]

    w0, a0, k_k, k_a, r_k, ln_w, ln_b = (vec[i:i + 1] for i in range(3, 10))
    half = LANES // 2
    ones_bd = ((_iota((LANES, LANES), 0) // half) == (_iota((LANES, LANES), 1) // half)).astype(F32)
    w_log = -_softplus(-(w0 + _nn(jnp.tanh(xw), w2_ref[...]))) - 0.5
    a_lr = _sigmoid(a0 + _nn(xa, a2_ref[...]))
    gate = _nn(_sigmoid(xg), g2_ref[...])
    kk = k * k_k
    kk = kk * lax.rsqrt(_nn(kk * kk, ones_bd, exact=True) + 1e-12)
    k = k * (1.0 + (a_lr - 1.0) * k_a)
    r_ref[...] = r
    lg_ref[...] = -jnp.exp(w_log)
    k_ref[...] = k
    v_ref[...] = v
    a_ref[...] = -kk
    b_ref[...] = kk * a_lr

    consts = _rwkv_consts(c)

    def body(ci, carry):
        sl = pl.ds(pl.multiple_of(ci * c, c), c)
        o, s_new = _rwkv_chunk(r_ref[sl, :], lg_ref[sl, :], k_ref[sl, :], v_ref[sl, :], a_ref[sl, :], b_ref[sl, :],
                               s_ref[...], consts)
        s_ref[...] = s_new
        y_ref[sl, :] = o
        return carry

    lax.fori_loop(0, t // c, body, 0)

    o = y_ref[...]
    inv_n = 1.0 / half
    mean = _nn(o, ones_bd, exact=True) * inv_n
    dev = o - mean
    var = _nn(dev * dev, ones_bd, exact=True) * inv_n
    o = dev * lax.rsqrt(var + RWKV_LN_EPS) * ln_w + ln_b
    o = o + _nn(r * k * r_k, ones_bd, exact=True) * v
    o_ref[...] = (o * gate).astype(o_ref.dtype)

    @pl.when(l == pl.num_programs(2) - 1)
    def _():
        so_ref[...] = s_ref[...]


def _rwkv(p, shift, vecs, mus, w2, a2, g2, s0, batch, seq, pairs, col0, tile):
    tile = min(tile, seq)
    nl = seq // tile
    wb = pairs * LANES
    cb0 = col0 // LANES
    xg_off = -(-(3 * wb + 2 * LANES) // 512) * 512
    assert (col0 + xg_off) % 512 == 0
    pblk = lambda off: pl.BlockSpec((tile, LANES), lambda b, h, l, off=off: (b * nl + l, cb0 + off * pairs + h))
    pfix = lambda cb: pl.BlockSpec((tile, LANES), lambda b, h, l, cb=cb: (b * nl + l, cb))
    hblk = lambda off: pl.BlockSpec((None, 1, LANES), lambda b, h, l, off=off: (b, 0, off * pairs + h))
    hfix = lambda cb: pl.BlockSpec((None, 1, LANES), lambda b, h, l, cb=cb: (b, 0, cb))
    st_spec = pl.BlockSpec((None, None, LANES, LANES), lambda b, h, l: (b, h, 0, 0))
    col = lambda rows: pl.BlockSpec((rows, LANES), lambda b, h, l: (0, h))
    full = lambda a: pl.BlockSpec(a.shape, lambda b, h, l: (0,) * a.ndim)
    tbuf = pltpu.VMEM((tile, LANES), F32)
    return pl.pallas_call(
        _rwkv_kernel,
        grid=(batch, pairs, nl),
        in_specs=[pblk(0), pblk(1), pblk(2), pfix(cb0 + 3 * pairs), pfix(cb0 + 3 * pairs + 1),
                  pl.BlockSpec((tile, 512), lambda b, h, l: (b * nl + l, (col0 + xg_off) // 512)),
                  hblk(0), hblk(1), hblk(2), hfix(3 * pairs), hfix(3 * pairs + 1),
                  pl.BlockSpec((None, 1, 512), lambda b, h, l: (b, 0, xg_off // 512)),
                  col(vecs.shape[0]), full(mus), col(w2.shape[0]), col(a2.shape[0]), col(g2.shape[0]), st_spec],
        out_specs=[pl.BlockSpec((tile, LANES), lambda b, h, l: (b * nl + l, h)), st_spec],
        out_shape=[jax.ShapeDtypeStruct((batch * seq, wb), BF16), jax.ShapeDtypeStruct(s0.shape, F32)],
        scratch_shapes=[pltpu.VMEM((LANES, LANES), F32),
                        pltpu.VMEM((5, tile + 8, LANES), F32),
                        pltpu.VMEM((tile + 8, 512), F32),
                        tbuf, tbuf, tbuf, tbuf, tbuf, tbuf, tbuf],
        compiler_params=_params(("parallel", "parallel", "arbitrary")),
        name="rwkv7",
    )(p, p, p, p, p, p, shift, shift, shift, shift, shift, shift, vecs, mus, w2, a2, g2, s0)


def _gdn_consts(c):
    row = _iota((c, c), 0)
    col = _iota((c, c), 1)
    return (row >= col).astype(F32), (row <= col).astype(F32), row >= col, row > col


def _gdn_chunk(q, k, v, gcol, grow, beta, s, consts):
    ltri, utri, incl, strict = consts
    c = q.shape[0]
    gc = _nn(ltri, gcol, exact=True)
    gr = _nn(grow, utri, exact=True)
    dec = jnp.exp(jnp.where(incl, gc[:, :c] - gr, -1e30))
    kq = _nt(jnp.concatenate([k, q], axis=0), k)
    a_mat = jnp.where(strict, beta[:, :c] * kq[:c] * dec, 0.0)
    inv = _neumann_inverse(-a_mat, int(math.log2(c)) - 1)
    eg = jnp.exp(gc)
    uw = _nn(inv, jnp.concatenate([beta * v, beta * eg * k], axis=1), exact=True)
    dv = v.shape[1]
    v_new = uw[:, :dv] - _nn(uw[:, dv:], s)
    o = _nn(q * eg, s) + _nn(kq[c:] * dec, v_new)
    gl = gc[c - 1:c, :]
    s_new = s * jnp.exp(gl) + _tn(k * jnp.exp(gl - gc), v_new)
    return o, s_new


def _gdn_kernel(q_ref, k_ref, v_ref, z_ref, ba_ref, cwq_ref, cwk_ref, cwv_ref, cq_ref, ck_ref, cv_ref,
                par_ref, nw_ref, s0_ref, o_ref, so_ref, s_ref, buf_ref, qs_ref, ks_ref, vs_ref,
                *, heads, conv_w):
    h = pl.program_id(1)
    l = pl.program_id(2)
    t = q_ref.shape[0]
    c = min(CHUNK, t)
    pad = 8
    hist = conv_w - 1
    streams = ((q_ref, cq_ref, cwq_ref, qs_ref), (k_ref, ck_ref, cwk_ref, ks_ref), (v_ref, cv_ref, cwv_ref, vs_ref))

    @pl.when(l == 0)
    def _():
        s_ref[...] = s0_ref[...]
        for i, (_, c_ref, _, _) in enumerate(streams):
            buf_ref[i, pad - hist:pad, :] = c_ref[...]

    for i, (x_ref, _, w_ref, dst_ref) in enumerate(streams):
        buf_ref[i, pad:pad + t, :] = x_ref[...]
        w = w_ref[...]
        y = jnp.zeros((t, LANES), F32)
        for j in range(conv_w):
            y = y + buf_ref[i, pad - hist + j:pad - hist + j + t, :] * w[j:j + 1]
        tail = buf_ref[i, pad + t - hist:pad + t, :]
        buf_ref[i, pad - hist:pad, :] = tail
        dst_ref[...] = _silu(y)

    par = par_ref[...]
    lane = _iota((1, LANES), 1)
    sel_b = (_iota((LANES, LANES), 0) == h).astype(F32)
    sel_a = (_iota((LANES, LANES), 0) == heads + h).astype(F32)
    sel_row = (_iota((c, LANES), 1) == heads + h).astype(F32)
    consts = _gdn_consts(c)
    nw = nw_ref[...]
    dk = q_ref.shape[1]

    def body(ci, carry):
        sl = pl.ds(pl.multiple_of(ci * c, c), c)
        q = qs_ref[sl, :]
        k = ks_ref[sl, :]
        q = q * lax.rsqrt(jnp.sum(q * q, axis=-1, keepdims=True) + 1e-12) * dk ** -0.5
        k = k * lax.rsqrt(jnp.sum(k * k, axis=-1, keepdims=True) + 1e-12)
        bac = ba_ref[sl, :]
        bac = jnp.where(lane < heads, _sigmoid(bac), -jnp.exp(par[0:1]) * _softplus(bac + par[1:2]))
        beta = _nn(bac, sel_b, exact=True)
        gcol = _nn(bac, sel_a, exact=True)
        grow = _nt(sel_row, bac, exact=True)
        o, s_new = _gdn_chunk(q, k, vs_ref[sl, :], gcol, grow, beta, s_ref[...], consts)
        s_ref[...] = s_new
        ms = jnp.mean(o * o, axis=-1, keepdims=True)
        o_ref[sl, :] = (o * lax.rsqrt(ms + NORM_EPS) * nw * _silu(z_ref[sl, :])).astype(o_ref.dtype)
        return carry

    lax.fori_loop(0, t // c, body, 0)

    @pl.when(l == pl.num_programs(2) - 1)
    def _():
        so_ref[...] = s_ref[...]


def _gdn(p, conv_w, conv_state, par, norm_w, s0, batch, seq, heads, tile):
    tile = min(tile, seq)
    nl = seq // tile
    cw = conv_w.shape[0]
    pblk = lambda off: pl.BlockSpec((tile, LANES), lambda b, h, l, off=off: (b * nl + l, off * heads + h))
    wblk = lambda off: pl.BlockSpec((cw, LANES), lambda b, h, l, off=off: (0, off * heads + h))
    cblk = lambda off: pl.BlockSpec((None, cw - 1, LANES), lambda b, h, l, off=off: (b, 0, off * heads + h))
    st_spec = pl.BlockSpec((None, None, LANES, LANES), lambda b, h, l: (b, h, 0, 0))
    tbuf = pltpu.VMEM((tile, LANES), F32)
    return pl.pallas_call(
        functools.partial(_gdn_kernel, heads=heads, conv_w=cw),
        grid=(batch, heads, nl),
        in_specs=[pblk(0), pblk(1), pblk(2), pblk(3),
                  pl.BlockSpec((tile, LANES), lambda b, h, l: (b * nl + l, 4 * heads)),
                  wblk(0), wblk(1), wblk(2), cblk(0), cblk(1), cblk(2),
                  pl.BlockSpec(par.shape, lambda b, h, l: (0, 0)),
                  pl.BlockSpec((1, LANES), lambda b, h, l: (0, 0)),
                  st_spec],
        out_specs=[pl.BlockSpec((tile, LANES), lambda b, h, l: (b * nl + l, h)), st_spec],
        out_shape=[jax.ShapeDtypeStruct((batch * seq, heads * LANES), BF16), jax.ShapeDtypeStruct(s0.shape, F32)],
        scratch_shapes=[pltpu.VMEM((LANES, LANES), F32),
                        pltpu.VMEM((3, tile + 8, LANES), F32),
                        tbuf, tbuf, tbuf],
        compiler_params=_params(("parallel", "parallel", "arbitrary")),
        name="gdn",
    )(p, p, p, p, p, conv_w, conv_w, conv_w, conv_state, conv_state, conv_state, par, norm_w, s0)


def _pad_cols(w, total):
    return jnp.pad(w, ((0, 0),) * (w.ndim - 1) + ((0, total - w.shape[-1]),))


def _ab_layout(a_cols, wb, lora_w, lora_a, lora_g):
    assert lora_w == LANES and lora_a == LANES and lora_g <= 512
    xg_off = -(-(3 * wb + 2 * LANES) // 512) * 512
    b_width = xg_off + 512
    assert a_cols % 512 == 0
    return xg_off, b_width


def _rwkv_cols(t, wb, xg_off, b_width, lora_g):
    head = t[..., :3 * wb + 2 * LANES]
    tail = t[..., 3 * wb + 2 * LANES:]
    z = lambda n: jnp.zeros(t.shape[:-1] + (n,), t.dtype)
    return jnp.concatenate([head, z(xg_off - head.shape[-1]), tail, z(b_width - xg_off - lora_g)], axis=-1)


def _trunk(x, mod, s_hgrn, s_rwkv, s_shift, s_gdn, s_conv, wts, tiles):
    batch, seq, d = x.shape
    m = batch * seq
    x2 = x.reshape(m, d)
    tm, t_hgrn, t_rwkv, t_gdn = tiles
    outs = {}
    depth = mod.shape[0]
    for layer in range(depth):
        j = layer // 2
        vecs6 = [mod[layer, :, i * d:(i + 1) * d].reshape(batch, 1, d) for i in range(6)]
        sh1, sc1, g1, sh2, sc2, g2 = vecs6
        if layer % 2 == 0:
            w = wts["ab"][j]
            p = _norm_matmul(x2, sc1, sh1, w["w_in"], seq, tm, 512)
            ha, wb = w["ha"], w["wb"]
            oa, st_a = _hgrn(p, w["lb"], w["hgrn_norm_w"], jnp.swapaxes(s_hgrn[j], -1, -2), batch, seq, ha, t_hgrn)
            pairs = wb // LANES
            hb = s_rwkv.shape[2]
            nb = s_rwkv.shape[-1]
            sp = s_rwkv[j].reshape(batch, pairs, 2, nb, nb)
            zero = jnp.zeros_like(sp[:, :, 0])
            s0 = jnp.concatenate([jnp.concatenate([sp[:, :, 0], zero], -1),
                                  jnp.concatenate([zero, sp[:, :, 1]], -1)], -2)
            shift = _rwkv_cols(s_shift[j], wb, w["xg_off"], w["b_width"], w["lora_g"])
            ob, st_b = _rwkv(p, shift, w["vecs"], w["mus"], w["w2"], w["a2"], w["g2"], s0,
                             batch, seq, pairs, w["a_cols"], t_rwkv)
            mix = jnp.concatenate([oa, ob], axis=-1)
            outs.setdefault("hgrn", []).append(jnp.swapaxes(st_a, -1, -2))
            outs.setdefault("rwkv", []).append(
                jnp.stack([st_b[:, :, :nb, :nb], st_b[:, :, nb:, nb:]], axis=2).reshape(batch, hb, nb, nb))
            last = p.reshape(batch, seq, -1)[:, seq - 1:, w["a_cols"]:]
            n_head = 3 * wb + 2 * LANES
            outs.setdefault("shift", []).append(
                jnp.concatenate([last[..., :n_head], last[..., w["xg_off"]:w["xg_off"] + w["lora_g"]]], axis=-1))
            w_out = w["w_out"]
        else:
            w = wts["gdn"][j]
            p = _norm_matmul(x2, sc1, sh1, w["w_in"], seq, tm, 512)
            hc = w["hc"]
            mix, st_c = _gdn(p, w["conv_w"], s_conv[j], w["par"], w["norm_w"], s_gdn[j], batch, seq, hc, t_gdn)
            outs.setdefault("gdn", []).append(st_c)
            cw = w["conv_w"].shape[0]
            raw = p.reshape(batch, seq, -1)[:, :, :3 * hc * LANES]
            prev = jnp.concatenate([s_conv[j], raw[:, max(seq - (cw - 1), 0):]], axis=1)
            outs.setdefault("conv", []).append(prev[:, -(cw - 1):])
            w_out = w["w_out"]
        x2 = _out_residual(mix, w_out, x2, g1, seq, tm, 512)
        x2 = _mlp(x2, sc2, sh2, g2, wts["mlp_w1"][layer], wts["mlp_w2"][layer], wts["final_w"],
                  seq, min(tm, 512), 512, final_norm=(layer == depth - 1))
    st = lambda name: jnp.stack(outs[name])
    return x2.reshape(batch, seq, d), st("hgrn"), st("rwkv"), st("shift"), st("gdn"), st("conv")


def kernel(x_prompt, x_sample, c_prompt, c_sample, state_hgrn, state_rwkv, state_rwkv_shift, state_gdn,
           state_gdn_conv, ada_w, ada_b, mlp_w1, mlp_w2, final_norm_w, ab_w_in, ab_w_out, hgrn_lb_logits,
           hgrn_norm_w, rwkv_mu, rwkv_w0, rwkv_w2, rwkv_a0, rwkv_a2, rwkv_g2, rwkv_k_k, rwkv_k_a, rwkv_r_k,
           rwkv_ln_w, rwkv_ln_b, gdn_w_in, gdn_w_out, gdn_conv_w, gdn_a_log, gdn_dt_bias, gdn_norm_w):
    d = x_prompt.shape[-1]
    nb_p, nb_s = x_prompt.shape[0], x_sample.shape[0]
    n_ab, n_c = ab_w_in.shape[0], gdn_w_in.shape[0]
    ha = state_hgrn.shape[2]
    a_cols = 4 * ha * LANES
    wb = rwkv_w0.shape[1]
    lora_w, lora_a, lora_g = rwkv_w2.shape[1], rwkv_a2.shape[1], rwkv_g2.shape[1]
    xg_off, b_width = _ab_layout(a_cols, wb, lora_w, lora_a, lora_g)
    hc = state_gdn.shape[2]

    rows = nb_p + nb_s
    rows_pad = -(-rows // 8) * 8
    c_all = jnp.pad(jnp.concatenate([c_prompt, c_sample], axis=0), ((0, rows_pad - rows), (0, 0)))
    mod = _adaln(c_all, ada_w, ada_b)

    lbs = jnp.cumsum(jax.nn.softmax(hgrn_lb_logits.astype(F32), axis=0), axis=0)
    wts = {"ab": [], "gdn": [], "final_w": final_norm_w.reshape(1, d),
           "mlp_w1": mlp_w1.astype(BF16), "mlp_w2": mlp_w2.astype(BF16)}
    for j in range(n_ab):
        w_in = ab_w_in[j]
        w_in = jnp.concatenate([w_in[:, :a_cols], _rwkv_cols(w_in[:, a_cols:], wb, xg_off, b_width, lora_g)], axis=-1)
        mu = _rwkv_cols(rwkv_mu[j][None], wb, xg_off, b_width, lora_g)[0]
        vec_rows = [mu[:wb], mu[wb:2 * wb], mu[2 * wb:3 * wb], rwkv_w0[j], rwkv_a0[j], rwkv_k_k[j], rwkv_k_a[j],
                    rwkv_r_k[j].reshape(-1), rwkv_ln_w[j], rwkv_ln_b[j]]
        vecs = jnp.pad(jnp.stack(vec_rows), ((0, 16 - len(vec_rows)), (0, 0)))
        mus = jnp.stack([_pad_cols(mu[3 * wb:3 * wb + LANES], 512), _pad_cols(mu[3 * wb + LANES:3 * wb + 2 * LANES], 512),
                         mu[xg_off:xg_off + 512]])
        wts["ab"].append({
            "w_in": w_in.astype(BF16), "w_out": ab_w_out[j].astype(BF16), "lb": lbs[j][None],
            "hgrn_norm_w": hgrn_norm_w[j][None], "vecs": vecs, "mus": jnp.pad(mus, ((0, 5), (0, 0))),
            "w2": rwkv_w2[j].astype(BF16), "a2": rwkv_a2[j].astype(BF16),
            "g2": jnp.pad(rwkv_g2[j], ((0, 512 - lora_g), (0, 0))).astype(BF16),
            "ha": ha, "wb": wb, "a_cols": a_cols, "xg_off": xg_off, "b_width": b_width, "lora_g": lora_g})
    for j in range(n_c):
        cols = gdn_w_in.shape[-1]
        cols_pad = -(-cols // 512) * 512
        par = jnp.zeros((8, LANES), F32)
        par = par.at[0, hc:2 * hc].set(gdn_a_log[j]).at[1, hc:2 * hc].set(gdn_dt_bias[j])
        wts["gdn"].append({
            "w_in": _pad_cols(gdn_w_in[j], cols_pad).astype(BF16), "w_out": gdn_w_out[j].astype(BF16),
            "conv_w": gdn_conv_w[j], "par": par, "norm_w": gdn_norm_w[j][None], "hc": hc})

    zeros = lambda s: jnp.zeros((s.shape[0], nb_p) + s.shape[2:], x_prompt.dtype)
    y_p, hg_p, rw_p, sh_p, gd_p, cv_p = _trunk(
        x_prompt, mod[:, :nb_p], zeros(state_hgrn), zeros(state_rwkv), zeros(state_rwkv_shift),
        zeros(state_gdn), zeros(state_gdn_conv), wts, PROMPT_TILES)
    y_s, hg_s, rw_s, sh_s, gd_s, cv_s = _trunk(
        x_sample, mod[:, nb_p:rows], state_hgrn, state_rwkv, state_rwkv_shift, state_gdn, state_gdn_conv,
        wts, SAMPLE_TILES)
    return (y_p, y_s, hg_p, rw_p, sh_p, gd_p, cv_p, hg_s, rw_s, sh_s, gd_s, cv_s)
```

```python
import functools
import math

import jax
import jax.numpy as jnp
from jax import lax
from jax.experimental import pallas as pl
from jax.experimental.pallas import tpu as pltpu

F32 = jnp.float32
BF16 = jnp.bfloat16
HI = lax.Precision.HIGHEST

NORM_EPS = 1e-6
RWKV_LN_EPS = 64e-5
CHUNK = 64
DIAG = 8
LANES = 128
VMEM_LIMIT = 56 * 1024 * 1024
PROMPT_TILES = (1024, 512, 256, 256)
SAMPLE_TILES = (512, 64, 64, 64)
RWKV_PAIRS_PER_STEP = 4
GDN_HEADS_PER_STEP = 8


def _mm(a, b, ca, cb, exact=False):
    dims = (((ca,), (cb,)), ((), ()))
    if exact:
        return lax.dot_general(a, b, dims, precision=HI, preferred_element_type=F32)
    return lax.dot_general(a.astype(BF16), b.astype(BF16), dims, preferred_element_type=F32)


def _nn(a, b, exact=False):
    return _mm(a, b, 1, 0, exact)


def _nt(a, b, exact=False):
    return _mm(a, b, 1, 1, exact)


def _tn(a, b, exact=False):
    return _mm(a, b, 0, 0, exact)


def _sigmoid(x):
    return 1.0 / (1.0 + jnp.exp(-x))


def _silu(x):
    return x * _sigmoid(x)


def _softplus(x):
    return jnp.maximum(x, 0.0) + jnp.log(1.0 + jnp.exp(-jnp.abs(x)))


def _iota(shape, dim):
    return lax.broadcasted_iota(jnp.int32, shape, dim)


def _split2(x):
    hi = x.astype(BF16)
    return hi, (x - hi.astype(F32)).astype(BF16)


def _split3(x):
    hi = x.astype(BF16)
    r = x - hi.astype(F32)
    lo = r.astype(BF16)
    return hi, lo, (r - lo.astype(F32)).astype(BF16)


def _mm_hi(a, b, a_parts=None, b_parts=None):
    a_hi, a_lo = a_parts if a_parts is not None else _split2(a)
    b_hi, b_lo = b_parts if b_parts is not None else _split2(b)
    return jnp.dot(jnp.concatenate([a_hi, a_lo, a_hi], axis=1), jnp.concatenate([b_hi, b_hi, b_lo], axis=0),
                   preferred_element_type=F32)


def _mask_mm_left(mask3, x):
    return jnp.dot(mask3, jnp.concatenate(_split3(x), axis=0), preferred_element_type=F32)


def _mask_mm_right(x, mask3):
    return jnp.dot(jnp.concatenate(_split3(x), axis=1), mask3, preferred_element_type=F32)


def _nilpotent_solve(m, x, stages):
    n = m.shape[0]
    for k in range(stages):
        last = k == stages - 1
        y = _mm_hi(m, x if last else jnp.concatenate([m, x], axis=1))
        yield
        if last:
            x = x + y
        else:
            m = y[:, :n]
            x = x + y[:, n:]
    return x


def _lockstep(generators):
    generators = list(generators)
    results = [None] * len(generators)
    live = list(range(len(generators)))
    while live:
        for i in list(live):
            try:
                next(generators[i])
            except StopIteration as stop:
                results[i] = stop.value
                live.remove(i)
    return results


def _params(sem, vmem=VMEM_LIMIT):
    return pltpu.CompilerParams(dimension_semantics=sem, vmem_limit_bytes=vmem)


def _rows_per_tile(rows_per_batch, tile):
    if tile <= rows_per_batch:
        assert rows_per_batch % tile == 0
        return 1
    assert tile % rows_per_batch == 0
    return tile // rows_per_batch


def _adaln_kernel(c_ref, w_ref, b_ref, o_ref):
    c = _silu(c_ref[...]).astype(BF16)
    o_ref[...] = jnp.dot(c, w_ref[...].astype(BF16), preferred_element_type=F32) + b_ref[...]


def _adaln(c, ada_w, ada_b):
    depth, d, n = ada_w.shape
    rows = c.shape[0]
    tn = 512 if n % 512 == 0 else n
    return pl.pallas_call(
        _adaln_kernel,
        grid=(depth, n // tn),
        in_specs=[pl.BlockSpec((rows, d), lambda l, j: (0, 0)),
                  pl.BlockSpec((None, d, tn), lambda l, j: (l, 0, j)),
                  pl.BlockSpec((None, 1, tn), lambda l, j: (l, 0, j))],
        out_specs=pl.BlockSpec((None, rows, tn), lambda l, j: (l, 0, j)),
        out_shape=jax.ShapeDtypeStruct((depth, rows, n), F32),
        compiler_params=_params(("parallel", "parallel")),
        name="adaln",
    )(c, ada_w, ada_b.reshape(depth, 1, n))


def _modulated_norm(x_ref, sc_ref, sh_ref, h_ref, groups):
    rows = x_ref.shape[0] // groups
    step = min(rows, 128)

    for gi in range(groups):
        sc = 1.0 + sc_ref[gi]
        sh = sh_ref[gi]

        def body(i, carry, gi=gi, sc=sc, sh=sh):
            sl = pl.ds(pl.multiple_of(gi * rows + i * step, step), step)
            x = x_ref[sl, :]
            ms = jnp.mean(x * x, axis=-1, keepdims=True)
            h_ref[sl, :] = (x * lax.rsqrt(ms + NORM_EPS) * sc + sh).astype(BF16)
            return carry

        lax.fori_loop(0, rows // step, body, 0)


def _norm_mm_kernel(x_ref, sc_ref, sh_ref, w_ref, o_ref, h_ref, *, groups):
    @pl.when(pl.program_id(1) == 0)
    def _():
        _modulated_norm(x_ref, sc_ref, sh_ref, h_ref, groups)

    o_ref[...] = jnp.dot(h_ref[...], w_ref[...], preferred_element_type=F32).astype(o_ref.dtype)


def _norm_matmul(x, scale, shift, w, rows_per_batch, tm, tn):
    m, d = x.shape
    n = w.shape[1]
    tm = min(tm, m)
    groups = _rows_per_tile(rows_per_batch, tm)
    bidx = (lambda i, j: (i * tm // rows_per_batch, 0, 0)) if groups == 1 else (lambda i, j: (i, 0, 0))
    return pl.pallas_call(
        functools.partial(_norm_mm_kernel, groups=groups),
        grid=(m // tm, n // tn),
        in_specs=[pl.BlockSpec((tm, d), lambda i, j: (i, 0), pipeline_mode=pl.Buffered(1)),
                  pl.BlockSpec((groups, 1, d), bidx),
                  pl.BlockSpec((groups, 1, d), bidx),
                  pl.BlockSpec((d, tn), lambda i, j: (0, j))],
        out_specs=pl.BlockSpec((tm, tn), lambda i, j: (i, j)),
        out_shape=jax.ShapeDtypeStruct((m, n), F32),
        scratch_shapes=[pltpu.VMEM((tm, d), BF16)],
        compiler_params=_params(("parallel", "arbitrary")),
        name="norm_matmul",
    )(x, scale, shift, w)


def _out_res_kernel(a_ref, w_ref, x_ref, g_ref, o_ref, *, groups):
    acc = jnp.dot(a_ref[...], w_ref[...], preferred_element_type=F32)
    rows = acc.shape[0] // groups
    for gi in range(groups):
        sl = slice(gi * rows, (gi + 1) * rows)
        o_ref[sl, :] = x_ref[sl, :] + g_ref[gi] * acc[sl, :]


def _out_residual(a, w, x, gate, rows_per_batch, tm, tn):
    m, k = a.shape
    n = w.shape[1]
    tm = min(tm, m)
    groups = _rows_per_tile(rows_per_batch, tm)
    bidx = (lambda i, j: (i * tm // rows_per_batch, 0, j)) if groups == 1 else (lambda i, j: (i, 0, j))
    return pl.pallas_call(
        functools.partial(_out_res_kernel, groups=groups),
        grid=(m // tm, n // tn),
        in_specs=[pl.BlockSpec((tm, k), lambda i, j: (i, 0)),
                  pl.BlockSpec((k, tn), lambda i, j: (0, j)),
                  pl.BlockSpec((tm, tn), lambda i, j: (i, j)),
                  pl.BlockSpec((groups, 1, tn), bidx)],
        out_specs=pl.BlockSpec((tm, tn), lambda i, j: (i, j)),
        out_shape=jax.ShapeDtypeStruct((m, n), F32),
        compiler_params=_params(("parallel", "parallel")),
        name="out_residual",
    )(a, w, x, gate)


def _mlp_kernel(x_ref, sc_ref, sh_ref, g_ref, w1_ref, w2_ref, fw_ref, o_ref, h_ref, *, groups, final_norm, tn2):
    f = pl.program_id(1)

    @pl.when(f == 0)
    def _():
        _modulated_norm(x_ref, sc_ref, sh_ref, h_ref, groups)

    u = jnp.dot(h_ref[...], w1_ref[...], preferred_element_type=F32)
    u = jnp.square(jnp.maximum(u, 0.0)).astype(BF16)
    d = o_ref.shape[1]
    for n0 in range(0, d, tn2):
        part = jnp.dot(u, w2_ref[:, n0:n0 + tn2], preferred_element_type=F32)

        @pl.when(f == 0)
        def _(part=part, n0=n0):
            o_ref[:, n0:n0 + tn2] = part

        @pl.when(f != 0)
        def _(part=part, n0=n0):
            o_ref[:, n0:n0 + tn2] += part

    @pl.when(f == pl.num_programs(1) - 1)
    def _():
        rows = x_ref.shape[0] // groups
        step = min(rows, 128)
        for gi in range(groups):
            g = g_ref[gi]

            def body(i, carry, gi=gi, g=g):
                sl = pl.ds(pl.multiple_of(gi * rows + i * step, step), step)
                y = x_ref[sl, :] + g * o_ref[sl, :]
                if final_norm:
                    ms = jnp.mean(y * y, axis=-1, keepdims=True)
                    y = y * lax.rsqrt(ms + NORM_EPS) * fw_ref[...]
                o_ref[sl, :] = y
                return carry

            lax.fori_loop(0, rows // step, body, 0)


def _mlp(x, scale, shift, gate, w1, w2, final_w, rows_per_batch, tm, tf, final_norm):
    m, d = x.shape
    ff = w1.shape[1]
    tm = min(tm, m)
    groups = _rows_per_tile(rows_per_batch, tm)
    bidx = (lambda i, f: (i * tm // rows_per_batch, 0, 0)) if groups == 1 else (lambda i, f: (i, 0, 0))
    vec = pl.BlockSpec((groups, 1, d), bidx)
    return pl.pallas_call(
        functools.partial(_mlp_kernel, groups=groups, final_norm=final_norm, tn2=min(d, 1024)),
        grid=(m // tm, ff // tf),
        in_specs=[pl.BlockSpec((tm, d), lambda i, f: (i, 0), pipeline_mode=pl.Buffered(1)),
                  vec, vec, vec,
                  pl.BlockSpec((d, tf), lambda i, f: (0, f)),
                  pl.BlockSpec((tf, d), lambda i, f: (f, 0)),
                  pl.BlockSpec((1, d), lambda i, f: (0, 0))],
        out_specs=pl.BlockSpec((tm, d), lambda i, f: (i, 0)),
        out_shape=jax.ShapeDtypeStruct((m, d), F32),
        scratch_shapes=[pltpu.VMEM((tm, d), BF16)],
        compiler_params=_params(("parallel", "arbitrary")),
        name="mlp",
    )(x, scale, shift, gate, w1, w2, final_w)


def _hgrn_consts(c):
    row = _iota((c, c), 0)
    col = _iota((c, c), 1)
    ltri = (_iota((c, 3 * c), 0) >= _iota((c, 3 * c), 1) % c).astype(BF16)
    levels = []
    gs = 2 * DIAG
    while gs <= c:
        half = gs // 2
        pair = (row // gs == col // gs) & (row % gs >= half) & (col % gs < half)
        levels.append((gs, pair))
        gs *= 2
    rix = _iota((c, 1), 0)
    return ltri, levels, rix


def _hgrn_chunk(q, k, v, g, st, consts):
    ltri, levels, rix = consts
    c = q.shape[0]
    gc = _mask_mm_left(ltri, g)
    o = _nt(q * jnp.exp(gc), st)
    att = jnp.zeros((c, c), F32)
    for gs, pair in levels:
        half = gs // 2
        ref = jnp.concatenate(
            [jnp.broadcast_to(gc[m0 * gs + half - 1:m0 * gs + half, :], (gs, gc.shape[1])) for m0 in range(c // gs)],
            axis=0)
        is_q = (rix % gs) >= half
        qt = jnp.where(is_q, q * jnp.exp(jnp.minimum(gc - ref, 0.0)), 0.0)
        kt = jnp.where(is_q, 0.0, k * jnp.exp(jnp.minimum(ref - gc, 0.0)))
        att = att + jnp.where(pair, _nt(qt, kt), 0.0)
    o = o + _nn(att, v)
    for dist in range(DIAG):
        if dist == 0:
            kd, gd, vd = k, gc, v
        else:
            kd = pltpu.roll(k, dist, 0)
            gd = pltpu.roll(gc, dist, 0)
            vd = pltpu.roll(v, dist, 0)
        w = jnp.sum(q * kd * jnp.exp(jnp.minimum(gc - gd, 0.0)), axis=-1, keepdims=True)
        o = o + jnp.where((rix % DIAG) >= dist, w, 0.0) * vd
    gl = gc[c - 1:c, :]
    st_new = st * jnp.exp(gl) + _tn(v, k * jnp.exp(gl - gc))
    return o, st_new


def _hgrn_kernel(q_ref, f_ref, i_ref, g_ref, lb_ref, nw_ref, s0_ref, o_ref, so_ref, s_ref, *, dk):
    l = pl.program_id(2)

    @pl.when(l == 0)
    def _():
        s_ref[...] = s0_ref[...]

    t = q_ref.shape[0]
    c = min(CHUNK, t)
    consts = _hgrn_consts(c)
    lb = lb_ref[...]
    nw = nw_ref[...]

    def body(ci, carry):
        sl = pl.ds(pl.multiple_of(ci * c, c), c)
        f = lb + (1.0 - lb) * _sigmoid(f_ref[sl, :])
        q = _silu(q_ref[sl, :]) * dk ** -0.5
        o, st = _hgrn_chunk(q, 1.0 - f, i_ref[sl, :], jnp.log(f), s_ref[...], consts)
        s_ref[...] = st
        ms = jnp.mean(o * o, axis=-1, keepdims=True)
        o_ref[sl, :] = (o * lax.rsqrt(ms + NORM_EPS) * nw * _silu(g_ref[sl, :])).astype(o_ref.dtype)
        return carry

    lax.fori_loop(0, t // c, body, 0)

    @pl.when(l == pl.num_programs(2) - 1)
    def _():
        so_ref[...] = s_ref[...]


def _hgrn(p, lb, norm_w, s0t, batch, seq, heads, tile):
    dk = LANES
    tile = min(tile, seq)
    nl = seq // tile
    blk = lambda off: pl.BlockSpec((tile, dk), lambda b, h, l, off=off: (b * nl + l, off * heads + h))
    st_spec = pl.BlockSpec((None, None, dk, dk), lambda b, h, l: (b, h, 0, 0))
    return pl.pallas_call(
        functools.partial(_hgrn_kernel, dk=dk),
        grid=(batch, heads, nl),
        in_specs=[blk(0), blk(1), blk(2), blk(3),
                  pl.BlockSpec((1, dk), lambda b, h, l: (0, h)),
                  pl.BlockSpec((1, dk), lambda b, h, l: (0, 0)),
                  st_spec],
        out_specs=[pl.BlockSpec((tile, dk), lambda b, h, l: (b * nl + l, h)), st_spec],
        out_shape=[jax.ShapeDtypeStruct((batch * seq, heads * dk), BF16),
                   jax.ShapeDtypeStruct(s0t.shape, F32)],
        scratch_shapes=[pltpu.VMEM((dk, dk), F32)],
        compiler_params=_params(("parallel", "parallel", "arbitrary")),
        name="hgrn2",
    )(p, p, p, p, lb, norm_w, s0t)


def _rwkv_consts(c):
    n = 4 * c
    row = _iota((n, n), 0)
    col = _iota((n, n), 1)
    same_head = ((row // c) % 2) == ((col // c) % 2)
    strict = (row % c) > (col % c)
    incl = (row % c) >= (col % c)
    keep = same_head & (strict | ((row >= 2 * c) & incl))
    ltri = (_iota((c, 3 * c), 0) >= _iota((c, 3 * c), 1) % c).astype(BF16)
    head0 = _iota((1, LANES), 1) < (LANES // 2)
    rs = _iota((LANES, LANES), 0)
    cs = _iota((LANES, LANES), 1)
    bdiag = (rs // (LANES // 2)) == (cs // (LANES // 2))
    return keep, ltri, head0, bdiag


def _rwkv_chunk(r, lg, k, v, a, b, s, consts):
    keep, ltri, head0, bdiag = consts
    c = r.shape[0]
    gc = _mask_mm_left(ltri, lg)
    yield
    e_incl = jnp.exp(gc)
    e_inv = jnp.exp(-gc)
    at = a * jnp.exp(gc - lg)
    rt = r * e_incl
    bt = b * e_inv
    kt = k * e_inv
    split = lambda x: [jnp.where(head0, x, 0.0), jnp.where(head0, 0.0, x)]
    lhs = jnp.concatenate(split(at) + split(rt), axis=0)
    rhs = jnp.concatenate([bt, bt, kt, kt, s], axis=0)
    pls = _nt(lhs, rhs)
    yield
    p = jnp.where(keep, pls[:, :4 * c], 0.0)
    ls = pls[:, 4 * c:]
    vst = jnp.concatenate(split(v), axis=0)
    pv = _nn(p[:, 2 * c:], vst)
    yield
    ust = yield from _nilpotent_solve(p[:2 * c, :2 * c], ls[:2 * c] + pv[:2 * c], int(math.log2(c)))
    ost = ls[2 * c:] + pv[2 * c:] + _nn(p[2 * c:, :2 * c], ust)
    yield
    o = ost[:c] + ost[c:]
    u = ust[:c] + ust[c:]
    gl = gc[c - 1:c, :]
    e_tail = jnp.exp(gl - gc)
    upd = _tn(jnp.concatenate([u, v], axis=0), jnp.concatenate([b * e_tail, k * e_tail], axis=0))
    yield
    return o, s * jnp.exp(gl) + jnp.where(bdiag, upd, 0.0)


def _rwkv_kernel(xr_ref, xk_ref, xv_ref, xw_ref, xa_ref, xg_ref,
                 hr_ref, hk_ref, hv_ref, hw_ref, ha_ref, hg_ref,
                 vec_ref, mus_ref, w2_ref, a2_ref, g2_ref, s0_ref,
                 o_ref, so_ref,
                 s_ref, buf_ref, bufs_ref, bufg_ref, r_ref, lg_ref, k_ref, v_ref, a_ref, b_ref, y_ref):
    l = pl.program_id(2)
    t, width = xr_ref.shape
    c = min(CHUNK, t)
    pad = 8

    @pl.when(l == 0)
    def _():
        s_ref[...] = s0_ref[...]
        for i, h_ref in enumerate((hr_ref, hk_ref, hv_ref)):
            buf_ref[i, pad - 1:pad, :] = h_ref[...]
        for i, h_ref in enumerate((hw_ref, ha_ref)):
            bufs_ref[i, pad - 1:pad, :] = h_ref[...]
        bufg_ref[pad - 1:pad, :] = hg_ref[...]

    def shifted(buf, x_ref, mu):
        x = x_ref[...]
        buf[pad:pad + t, :] = x
        prev = buf[pad - 1:pad - 1 + t, :]
        buf[pad - 1:pad, :] = x[t - 1:t, :]
        return x + (prev - x) * mu

    vec = vec_ref[...]
    mus = mus_ref[...]
    r = shifted(buf_ref.at[0], xr_ref, vec[0:1])
    k = shifted(buf_ref.at[1], xk_ref, vec[1:2])
    v = shifted(buf_ref.at[2], xv_ref, vec[2:3])
    xw = shifted(bufs_ref.at[0], xw_ref, mus[0:1, :LANES])
    xa = shifted(bufs_ref.at[1], xa_ref, mus[1:2, :LANES])
    xg = shifted(bufg_ref, xg_ref, mus[2:3])

    w0, a0, k_k, k_a, r_k, ln_w, ln_b = (vec[i:i + 1] for i in range(3, 10))
    half = LANES // 2
    ones3 = ((_iota((3 * width, width), 0) % width) // half == _iota((3 * width, width), 1) // half).astype(BF16)
    w_log = -_softplus(-(w0 + _nn(jnp.tanh(xw), w2_ref[...]))) - 0.5
    a_lr = _sigmoid(a0 + _nn(xa, a2_ref[...]))
    gate = _nn(_sigmoid(xg), g2_ref[...])
    kk = k * k_k
    kk = kk * lax.rsqrt(_mask_mm_right(kk * kk, ones3) + 1e-12)
    k = k * (1.0 + (a_lr - 1.0) * k_a)
    r_ref[...] = r
    lg_ref[...] = -jnp.exp(w_log)
    k_ref[...] = k
    v_ref[...] = v
    a_ref[...] = -kk
    b_ref[...] = kk * a_lr

    consts = _rwkv_consts(c)

    def body(ci, carry):
        sl = pl.ds(pl.multiple_of(ci * c, c), c)
        operands = []
        for pi in range(width // LANES):
            cols = slice(pi * LANES, (pi + 1) * LANES)
            operands.append((r_ref[sl, cols], lg_ref[sl, cols], k_ref[sl, cols], v_ref[sl, cols],
                             a_ref[sl, cols], b_ref[sl, cols], s_ref[pi]))
        results = _lockstep(_rwkv_chunk(*ops, consts) for ops in operands)
        for pi, (o, s_new) in enumerate(results):
            s_ref[pi] = s_new
            y_ref[sl, pi * LANES:(pi + 1) * LANES] = o
        return carry

    lax.fori_loop(0, t // c, body, 0)

    o = y_ref[...]
    inv_n = 1.0 / half
    mean = _mask_mm_right(o, ones3) * inv_n
    dev = o - mean
    var = _mask_mm_right(dev * dev, ones3) * inv_n
    o = dev * lax.rsqrt(var + RWKV_LN_EPS) * ln_w + ln_b
    o = o + _mask_mm_right(r * k * r_k, ones3) * v
    o_ref[...] = (o * gate).astype(o_ref.dtype)

    @pl.when(l == pl.num_programs(2) - 1)
    def _():
        so_ref[...] = s_ref[...]


def _rwkv(p, shift, vecs, mus, w2, a2, g2, s0, batch, seq, pairs, col0, tile):
    tile = min(tile, seq)
    nl = seq // tile
    wb = pairs * LANES
    pb = RWKV_PAIRS_PER_STEP
    width = pb * LANES
    groups = pairs // pb
    assert pairs % pb == 0 and col0 % width == 0
    cb0 = col0 // LANES
    gb0 = col0 // width
    xg_off = -(-(3 * wb + 2 * LANES) // 512) * 512
    assert (col0 + xg_off) % 512 == 0
    pblk = lambda off: pl.BlockSpec((tile, width), lambda b, h, l, off=off: (b * nl + l, gb0 + off * groups + h))
    pfix = lambda cb: pl.BlockSpec((tile, LANES), lambda b, h, l, cb=cb: (b * nl + l, cb))
    hblk = lambda off: pl.BlockSpec((None, 1, width), lambda b, h, l, off=off: (b, 0, off * groups + h))
    hfix = lambda cb: pl.BlockSpec((None, 1, LANES), lambda b, h, l, cb=cb: (b, 0, cb))
    st_spec = pl.BlockSpec((None, pb, LANES, LANES), lambda b, h, l: (b, h, 0, 0))
    col = lambda rows: pl.BlockSpec((rows, width), lambda b, h, l: (0, h))
    full = lambda a: pl.BlockSpec(a.shape, lambda b, h, l: (0,) * a.ndim)
    tbuf = pltpu.VMEM((tile, width), F32)
    return pl.pallas_call(
        _rwkv_kernel,
        grid=(batch, groups, nl),
        in_specs=[pblk(0), pblk(1), pblk(2), pfix(cb0 + 3 * pairs), pfix(cb0 + 3 * pairs + 1),
                  pl.BlockSpec((tile, 512), lambda b, h, l: (b * nl + l, (col0 + xg_off) // 512)),
                  hblk(0), hblk(1), hblk(2), hfix(3 * pairs), hfix(3 * pairs + 1),
                  pl.BlockSpec((None, 1, 512), lambda b, h, l: (b, 0, xg_off // 512)),
                  col(vecs.shape[0]), full(mus), col(w2.shape[0]), col(a2.shape[0]), col(g2.shape[0]), st_spec],
        out_specs=[pl.BlockSpec((tile, width), lambda b, h, l: (b * nl + l, h)), st_spec],
        out_shape=[jax.ShapeDtypeStruct((batch * seq, wb), BF16), jax.ShapeDtypeStruct(s0.shape, F32)],
        scratch_shapes=[pltpu.VMEM((pb, LANES, LANES), F32),
                        pltpu.VMEM((3, tile + 8, width), F32),
                        pltpu.VMEM((2, tile + 8, LANES), F32),
                        pltpu.VMEM((tile + 8, 512), F32),
                        tbuf, tbuf, tbuf, tbuf, tbuf, tbuf, tbuf],
        compiler_params=_params(("parallel", "parallel", "arbitrary")),
        name="rwkv7",
    )(p, p, p, p, p, p, shift, shift, shift, shift, shift, shift, vecs, mus, w2, a2, g2, s0)


def _gdn_consts(c):
    n = 2 * c
    row = _iota((n, n), 0)
    col = _iota((n, n), 1)
    same = (row // c) == (col // c)
    incl = same & ((row % c) >= (col % c))
    strict = same & ((row % c) > (col % c))
    r3 = _iota((n, 3 * n), 0)
    c3 = _iota((n, 3 * n), 1) % n
    ltri3 = (((r3 // c) == (c3 // c)) & ((r3 % c) >= (c3 % c))).astype(BF16)
    return ltri3, incl, strict, strict.astype(F32)


def _gdn_pair_chunk(q, k, v, g, beta, states, consts):
    ltri3, incl, strict, strict_f = consts
    n = q.shape[0]
    c = n // 2
    cum = _mask_mm_left(ltri3, jnp.concatenate([g * strict_f, jnp.broadcast_to(g, (n, LANES))], axis=1))
    yield
    gc = cum[:, n:]
    dec = jnp.exp(jnp.where(incl, cum[:, :n], -1e30))
    kq = _nt(jnp.concatenate([k, q], axis=0), k)
    yield
    a_mat = jnp.where(strict, beta * kq[:n] * dec, 0.0)
    eg = jnp.exp(gc)
    uw = yield from _nilpotent_solve(-a_mat, jnp.concatenate([beta * v, beta * eg * k], axis=1), int(math.log2(c)))
    u, w = uw[:, :LANES], uw[:, LANES:]
    head0 = _iota((n, 1), 0) < c
    head0_2 = (_iota((2 * n, 1), 0) % n) < c
    by_head = lambda x, m: jnp.concatenate([jnp.where(m, x, 0.0), jnp.where(m, 0.0, x)], axis=1)
    wq = jnp.concatenate([w, q * eg], axis=0)
    ws = _nn(by_head(wq, head0_2), jnp.concatenate(states, axis=0))
    yield
    v_new = u - ws[:n]
    o = ws[n:] + _nn(kq[n:] * dec, v_new)
    yield
    gl = jnp.where(head0, gc[c - 1:c, :], gc[n - 1:n, :])
    upd = _tn(k * jnp.exp(gl - gc), by_head(v_new, head0))
    yield
    new_states = [s * jnp.exp(gc[(hi + 1) * c - 1:(hi + 1) * c, :]) + upd[:, hi * LANES:(hi + 1) * LANES]
                  for hi, s in enumerate(states)]
    return o, new_states


def _gdn_kernel(q_ref, k_ref, v_ref, z_ref, ba_ref, cwq_ref, cwk_ref, cwv_ref, cq_ref, ck_ref, cv_ref,
                par_ref, nw_ref, s0_ref, o_ref, so_ref, s_ref, buf_ref, qs_ref, ks_ref, vs_ref,
                *, heads, conv_w):
    hg = pl.program_id(1)
    l = pl.program_id(2)
    t, width = q_ref.shape
    hb = width // LANES
    c = min(CHUNK, t)
    pad = 8
    hist = conv_w - 1
    streams = ((q_ref, cq_ref, cwq_ref, qs_ref), (k_ref, ck_ref, cwk_ref, ks_ref), (v_ref, cv_ref, cwv_ref, vs_ref))

    @pl.when(l == 0)
    def _():
        s_ref[...] = s0_ref[...]
        for i, (_, c_ref, _, _) in enumerate(streams):
            buf_ref[i, pad - hist:pad, :] = c_ref[...]

    for i, (x_ref, _, w_ref, dst_ref) in enumerate(streams):
        buf_ref[i, pad:pad + t, :] = x_ref[...]
        w = w_ref[...]
        y = jnp.zeros((t, width), F32)
        for j in range(conv_w):
            y = y + buf_ref[i, pad - hist + j:pad - hist + j + t, :] * w[j:j + 1]
        tail = buf_ref[i, pad + t - hist:pad + t, :]
        buf_ref[i, pad - hist:pad, :] = tail
        dst_ref[...] = _silu(y)

    par = par_ref[...]
    lane = _iota((1, LANES), 1)
    consts = _gdn_consts(c)
    nw = nw_ref[...]

    def body(ci, carry):
        sl = pl.ds(pl.multiple_of(ci * c, c), c)
        bac = ba_ref[sl, :]
        bac = jnp.where(lane < heads, _sigmoid(bac), -jnp.exp(par[0:1]) * _softplus(bac + par[1:2]))
        column = lambda idx: jnp.sum(jnp.where(lane == idx, bac, 0.0), axis=-1, keepdims=True)
        operands = []
        for pi in range(hb // 2):
            local = (2 * pi, 2 * pi + 1)
            cols = [slice(h * LANES, (h + 1) * LANES) for h in local]
            stack = lambda ref: jnp.concatenate([ref[sl, cs] for cs in cols], axis=0)
            beta = jnp.concatenate([column(hg * hb + h) for h in local], axis=0)
            g = jnp.concatenate([column(heads + hg * hb + h) for h in local], axis=0)
            operands.append((stack(qs_ref), stack(ks_ref), stack(vs_ref), g, beta, [s_ref[h] for h in local],
                             [z_ref[sl, cs] for cs in cols]))
        chains = []
        for q, k, v, g, beta, states, _ in operands:
            q = q * lax.rsqrt(jnp.sum(q * q, axis=-1, keepdims=True) + 1e-12) * LANES ** -0.5
            k = k * lax.rsqrt(jnp.sum(k * k, axis=-1, keepdims=True) + 1e-12)
            chains.append(_gdn_pair_chunk(q, k, v, g, beta, states, consts))
        results = _lockstep(chains)
        for pi, (o, new_states) in enumerate(results):
            z = operands[pi][6]
            for i in range(2):
                h = 2 * pi + i
                s_ref[h] = new_states[i]
                oh = o[i * c:(i + 1) * c]
                ms = jnp.mean(oh * oh, axis=-1, keepdims=True)
                o_ref[sl, h * LANES:(h + 1) * LANES] = (
                    oh * lax.rsqrt(ms + NORM_EPS) * nw * _silu(z[i])).astype(o_ref.dtype)
        return carry

    lax.fori_loop(0, t // c, body, 0)

    @pl.when(l == pl.num_programs(2) - 1)
    def _():
        so_ref[...] = s_ref[...]


def _gdn(p, conv_w, conv_state, par, norm_w, s0, batch, seq, heads, tile):
    tile = min(tile, seq)
    nl = seq // tile
    cw = conv_w.shape[0]
    hb = GDN_HEADS_PER_STEP
    assert heads % hb == 0 and hb % 2 == 0
    groups = heads // hb
    width = hb * LANES
    pblk = lambda off: pl.BlockSpec((tile, width), lambda b, h, l, off=off: (b * nl + l, off * groups + h))
    wblk = lambda off: pl.BlockSpec((cw, width), lambda b, h, l, off=off: (0, off * groups + h))
    cblk = lambda off: pl.BlockSpec((None, cw - 1, width), lambda b, h, l, off=off: (b, 0, off * groups + h))
    st_spec = pl.BlockSpec((None, hb, LANES, LANES), lambda b, h, l: (b, h, 0, 0))
    tbuf = pltpu.VMEM((tile, width), F32)
    return pl.pallas_call(
        functools.partial(_gdn_kernel, heads=heads, conv_w=cw),
        grid=(batch, groups, nl),
        in_specs=[pblk(0), pblk(1), pblk(2), pblk(3),
                  pl.BlockSpec((tile, LANES), lambda b, h, l: (b * nl + l, 4 * heads)),
                  wblk(0), wblk(1), wblk(2), cblk(0), cblk(1), cblk(2),
                  pl.BlockSpec(par.shape, lambda b, h, l: (0, 0)),
                  pl.BlockSpec((1, LANES), lambda b, h, l: (0, 0)),
                  st_spec],
        out_specs=[pl.BlockSpec((tile, width), lambda b, h, l: (b * nl + l, h)), st_spec],
        out_shape=[jax.ShapeDtypeStruct((batch * seq, heads * LANES), BF16), jax.ShapeDtypeStruct(s0.shape, F32)],
        scratch_shapes=[pltpu.VMEM((hb, LANES, LANES), F32),
                        pltpu.VMEM((3, tile + 8, width), F32),
                        tbuf, tbuf, tbuf],
        compiler_params=_params(("parallel", "parallel", "arbitrary")),
        name="gdn",
    )(p, p, p, p, p, conv_w, conv_w, conv_w, conv_state, conv_state, conv_state, par, norm_w, s0)


def _pad_cols(w, total):
    return jnp.pad(w, ((0, 0),) * (w.ndim - 1) + ((0, total - w.shape[-1]),))


def _ab_layout(a_cols, wb, lora_w, lora_a, lora_g):
    assert lora_w == LANES and lora_a == LANES and lora_g <= 512
    xg_off = -(-(3 * wb + 2 * LANES) // 512) * 512
    b_width = xg_off + 512
    assert a_cols % 512 == 0
    return xg_off, b_width


def _rwkv_cols(t, wb, xg_off, b_width, lora_g):
    head = t[..., :3 * wb + 2 * LANES]
    tail = t[..., 3 * wb + 2 * LANES:]
    z = lambda n: jnp.zeros(t.shape[:-1] + (n,), t.dtype)
    return jnp.concatenate([head, z(xg_off - head.shape[-1]), tail, z(b_width - xg_off - lora_g)], axis=-1)


def _trunk(x, mod, s_hgrn, s_rwkv, s_shift, s_gdn, s_conv, wts, tiles):
    batch, seq, d = x.shape
    m = batch * seq
    x2 = x.reshape(m, d)
    tm, t_hgrn, t_rwkv, t_gdn = tiles
    outs = {}
    depth = mod.shape[0]
    for layer in range(depth):
        j = layer // 2
        vecs6 = [mod[layer, :, i * d:(i + 1) * d].reshape(batch, 1, d) for i in range(6)]
        sh1, sc1, g1, sh2, sc2, g2 = vecs6
        if layer % 2 == 0:
            w = wts["ab"][j]
            p = _norm_matmul(x2, sc1, sh1, w["w_in"], seq, tm, 512)
            ha, wb = w["ha"], w["wb"]
            oa, st_a = _hgrn(p, w["lb"], w["hgrn_norm_w"], jnp.swapaxes(s_hgrn[j], -1, -2), batch, seq, ha, t_hgrn)
            pairs = wb // LANES
            hb = s_rwkv.shape[2]
            nb = s_rwkv.shape[-1]
            sp = s_rwkv[j].reshape(batch, pairs, 2, nb, nb)
            zero = jnp.zeros_like(sp[:, :, 0])
            s0 = jnp.concatenate([jnp.concatenate([sp[:, :, 0], zero], -1),
                                  jnp.concatenate([zero, sp[:, :, 1]], -1)], -2)
            shift = _rwkv_cols(s_shift[j], wb, w["xg_off"], w["b_width"], w["lora_g"])
            ob, st_b = _rwkv(p, shift, w["vecs"], w["mus"], w["w2"], w["a2"], w["g2"], s0,
                             batch, seq, pairs, w["a_cols"], t_rwkv)
            mix = jnp.concatenate([oa, ob], axis=-1)
            outs.setdefault("hgrn", []).append(jnp.swapaxes(st_a, -1, -2))
            outs.setdefault("rwkv", []).append(
                jnp.stack([st_b[:, :, :nb, :nb], st_b[:, :, nb:, nb:]], axis=2).reshape(batch, hb, nb, nb))
            last = p.reshape(batch, seq, -1)[:, seq - 1:, w["a_cols"]:]
            n_head = 3 * wb + 2 * LANES
            outs.setdefault("shift", []).append(
                jnp.concatenate([last[..., :n_head], last[..., w["xg_off"]:w["xg_off"] + w["lora_g"]]], axis=-1))
            w_out = w["w_out"]
        else:
            w = wts["gdn"][j]
            p = _norm_matmul(x2, sc1, sh1, w["w_in"], seq, tm, 512)
            hc = w["hc"]
            mix, st_c = _gdn(p, w["conv_w"], s_conv[j], w["par"], w["norm_w"], s_gdn[j], batch, seq, hc, t_gdn)
            outs.setdefault("gdn", []).append(st_c)
            cw = w["conv_w"].shape[0]
            raw = p.reshape(batch, seq, -1)[:, :, :3 * hc * LANES]
            prev = jnp.concatenate([s_conv[j], raw[:, max(seq - (cw - 1), 0):]], axis=1)
            outs.setdefault("conv", []).append(prev[:, -(cw - 1):])
            w_out = w["w_out"]
        x2 = _out_residual(mix, w_out, x2, g1, seq, tm, 512)
        x2 = _mlp(x2, sc2, sh2, g2, wts["mlp_w1"][layer], wts["mlp_w2"][layer], wts["final_w"],
                  seq, min(tm, 512), 512, final_norm=(layer == depth - 1))
    st = lambda name: jnp.stack(outs[name])
    return x2.reshape(batch, seq, d), st("hgrn"), st("rwkv"), st("shift"), st("gdn"), st("conv")


def kernel(x_prompt, x_sample, c_prompt, c_sample, state_hgrn, state_rwkv, state_rwkv_shift, state_gdn,
           state_gdn_conv, ada_w, ada_b, mlp_w1, mlp_w2, final_norm_w, ab_w_in, ab_w_out, hgrn_lb_logits,
           hgrn_norm_w, rwkv_mu, rwkv_w0, rwkv_w2, rwkv_a0, rwkv_a2, rwkv_g2, rwkv_k_k, rwkv_k_a, rwkv_r_k,
           rwkv_ln_w, rwkv_ln_b, gdn_w_in, gdn_w_out, gdn_conv_w, gdn_a_log, gdn_dt_bias, gdn_norm_w):
    d = x_prompt.shape[-1]
    nb_p, nb_s = x_prompt.shape[0], x_sample.shape[0]
    n_ab, n_c = ab_w_in.shape[0], gdn_w_in.shape[0]
    ha = state_hgrn.shape[2]
    a_cols = 4 * ha * LANES
    wb = rwkv_w0.shape[1]
    lora_w, lora_a, lora_g = rwkv_w2.shape[1], rwkv_a2.shape[1], rwkv_g2.shape[1]
    xg_off, b_width = _ab_layout(a_cols, wb, lora_w, lora_a, lora_g)
    hc = state_gdn.shape[2]

    rows = nb_p + nb_s
    rows_pad = -(-rows // 8) * 8
    c_all = jnp.pad(jnp.concatenate([c_prompt, c_sample], axis=0), ((0, rows_pad - rows), (0, 0)))
    mod = _adaln(c_all, ada_w, ada_b)

    lbs = jnp.cumsum(jax.nn.softmax(hgrn_lb_logits.astype(F32), axis=0), axis=0)
    wts = {"ab": [], "gdn": [], "final_w": final_norm_w.reshape(1, d),
           "mlp_w1": mlp_w1.astype(BF16), "mlp_w2": mlp_w2.astype(BF16)}
    for j in range(n_ab):
        w_in = ab_w_in[j]
        w_in = jnp.concatenate([w_in[:, :a_cols], _rwkv_cols(w_in[:, a_cols:], wb, xg_off, b_width, lora_g)], axis=-1)
        mu = _rwkv_cols(rwkv_mu[j][None], wb, xg_off, b_width, lora_g)[0]
        vec_rows = [mu[:wb], mu[wb:2 * wb], mu[2 * wb:3 * wb], rwkv_w0[j], rwkv_a0[j], rwkv_k_k[j], rwkv_k_a[j],
                    rwkv_r_k[j].reshape(-1), rwkv_ln_w[j], rwkv_ln_b[j]]
        vecs = jnp.pad(jnp.stack(vec_rows), ((0, 16 - len(vec_rows)), (0, 0)))
        mus = jnp.stack([_pad_cols(mu[3 * wb:3 * wb + LANES], 512), _pad_cols(mu[3 * wb + LANES:3 * wb + 2 * LANES], 512),
                         mu[xg_off:xg_off + 512]])
        wts["ab"].append({
            "w_in": w_in.astype(BF16), "w_out": ab_w_out[j].astype(BF16), "lb": lbs[j][None],
            "hgrn_norm_w": hgrn_norm_w[j][None], "vecs": vecs, "mus": jnp.pad(mus, ((0, 5), (0, 0))),
            "w2": rwkv_w2[j].astype(BF16), "a2": rwkv_a2[j].astype(BF16),
            "g2": jnp.pad(rwkv_g2[j], ((0, 512 - lora_g), (0, 0))).astype(BF16),
            "ha": ha, "wb": wb, "a_cols": a_cols, "xg_off": xg_off, "b_width": b_width, "lora_g": lora_g})
    for j in range(n_c):
        cols = gdn_w_in.shape[-1]
        cols_pad = -(-cols // 512) * 512
        par = jnp.zeros((8, LANES), F32)
        par = par.at[0, hc:2 * hc].set(gdn_a_log[j]).at[1, hc:2 * hc].set(gdn_dt_bias[j])
        wts["gdn"].append({
            "w_in": _pad_cols(gdn_w_in[j], cols_pad).astype(BF16), "w_out": gdn_w_out[j].astype(BF16),
            "conv_w": gdn_conv_w[j], "par": par, "norm_w": gdn_norm_w[j][None], "hc": hc})

    zeros = lambda s: jnp.zeros((s.shape[0], nb_p) + s.shape[2:], x_prompt.dtype)
    y_p, hg_p, rw_p, sh_p, gd_p, cv_p = _trunk(
        x_prompt, mod[:, :nb_p], zeros(state_hgrn), zeros(state_rwkv), zeros(state_rwkv_shift),
        zeros(state_gdn), zeros(state_gdn_conv), wts, PROMPT_TILES)
    y_s, hg_s, rw_s, sh_s, gd_s, cv_s = _trunk(
        x_sample, mod[:, nb_p:rows], state_hgrn, state_rwkv, state_rwkv_shift, state_gdn, state_gdn_conv,
        wts, SAMPLE_TILES)
    return (y_p, y_s, hg_p, rw_p, sh_p, gd_p, cv_p, hg_s, rw_s, sh_s, gd_s, cv_s)
```

```python
import functools
import math

import jax
import jax.numpy as jnp
from jax import lax
from jax.experimental import pallas as pl
from jax.experimental.pallas import tpu as pltpu

F32 = jnp.float32
BF16 = jnp.bfloat16
HI = lax.Precision.HIGHEST

NORM_EPS = 1e-6
RWKV_LN_EPS = 64e-5
CHUNK = 64
DIAG = 8
LANES = 128
VMEM_LIMIT = 56 * 1024 * 1024
MLP_VMEM_LIMIT = 60 * 1024 * 1024
PROMPT_TILES = (1024, 512, 256, 256)
SAMPLE_TILES = (512, 64, 64, 64)
HGRN_HEADS_PER_STEP = 4
RWKV_PAIRS_PER_STEP = 4
GDN_HEADS_PER_STEP = 8


def _mm(a, b, ca, cb, exact=False):
    dims = (((ca,), (cb,)), ((), ()))
    if exact:
        return lax.dot_general(a, b, dims, precision=HI, preferred_element_type=F32)
    return lax.dot_general(a.astype(BF16), b.astype(BF16), dims, preferred_element_type=F32)


def _nn(a, b, exact=False):
    return _mm(a, b, 1, 0, exact)


def _nt(a, b, exact=False):
    return _mm(a, b, 1, 1, exact)


def _tn(a, b, exact=False):
    return _mm(a, b, 0, 0, exact)


def _sigmoid(x):
    return 1.0 / (1.0 + jnp.exp(-x))


def _silu(x):
    return x * _sigmoid(x)


def _softplus(x):
    return jnp.maximum(x, 0.0) + jnp.log(1.0 + jnp.exp(-jnp.abs(x)))


def _iota(shape, dim):
    return lax.broadcasted_iota(jnp.int32, shape, dim)


def _split2(x):
    hi = x.astype(BF16)
    return hi, (x - hi.astype(F32)).astype(BF16)


def _split3(x):
    hi = x.astype(BF16)
    r = x - hi.astype(F32)
    lo = r.astype(BF16)
    return hi, lo, (r - lo.astype(F32)).astype(BF16)


def _mm_hi(a, b, a_parts=None, b_parts=None):
    a_hi, a_lo = a_parts if a_parts is not None else _split2(a)
    b_hi, b_lo = b_parts if b_parts is not None else _split2(b)
    return jnp.dot(jnp.concatenate([a_hi, a_lo, a_hi], axis=1), jnp.concatenate([b_hi, b_hi, b_lo], axis=0),
                   preferred_element_type=F32)


def _mask_mm_left(mask3, x):
    return jnp.dot(mask3, jnp.concatenate(_split3(x), axis=0), preferred_element_type=F32)


def _mask_mm_right(x, mask3):
    return jnp.dot(jnp.concatenate(_split3(x), axis=1), mask3, preferred_element_type=F32)


def _nilpotent_solve(m, x, stages):
    n = m.shape[0]
    for k in range(stages):
        last = k == stages - 1
        y = _mm_hi(m, x if last else jnp.concatenate([m, x], axis=1))
        yield
        if last:
            x = x + y
        else:
            m = y[:, :n]
            x = x + y[:, n:]
    return x


def _lockstep(generators):
    generators = list(generators)
    results = [None] * len(generators)
    live = list(range(len(generators)))
    while live:
        for i in list(live):
            try:
                next(generators[i])
            except StopIteration as stop:
                results[i] = stop.value
                live.remove(i)
    return results


def _params(sem, vmem=VMEM_LIMIT):
    return pltpu.CompilerParams(dimension_semantics=sem, vmem_limit_bytes=vmem)


def _rows_per_tile(rows_per_batch, tile):
    if tile <= rows_per_batch:
        assert rows_per_batch % tile == 0
        return 1
    assert tile % rows_per_batch == 0
    return tile // rows_per_batch


def _adaln_kernel(c_ref, w_ref, b_ref, o_ref):
    c = _silu(c_ref[...]).astype(BF16)
    o_ref[...] = jnp.dot(c, w_ref[...].astype(BF16), preferred_element_type=F32) + b_ref[...]


def _adaln(c, ada_w, ada_b):
    depth, d, n = ada_w.shape
    rows = c.shape[0]
    tn = 512 if n % 512 == 0 else n
    return pl.pallas_call(
        _adaln_kernel,
        grid=(depth, n // tn),
        in_specs=[pl.BlockSpec((rows, d), lambda l, j: (0, 0)),
                  pl.BlockSpec((None, d, tn), lambda l, j: (l, 0, j)),
                  pl.BlockSpec((None, 1, tn), lambda l, j: (l, 0, j))],
        out_specs=pl.BlockSpec((None, rows, tn), lambda l, j: (l, 0, j)),
        out_shape=jax.ShapeDtypeStruct((depth, rows, n), F32),
        compiler_params=_params(("parallel", "parallel")),
        name="adaln",
    )(c, ada_w, ada_b.reshape(depth, 1, n))


def _modulated_norm(x_ref, sc_ref, sh_ref, h_ref, groups):
    rows = x_ref.shape[0] // groups
    step = min(rows, 128)

    for gi in range(groups):
        sc = 1.0 + sc_ref[gi]
        sh = sh_ref[gi]

        def body(i, carry, gi=gi, sc=sc, sh=sh):
            sl = pl.ds(pl.multiple_of(gi * rows + i * step, step), step)
            x = x_ref[sl, :]
            ms = jnp.mean(x * x, axis=-1, keepdims=True)
            h_ref[sl, :] = (x * lax.rsqrt(ms + NORM_EPS) * sc + sh).astype(BF16)
            return carry

        lax.fori_loop(0, rows // step, body, 0)


def _norm_mm_kernel(x_ref, sc_ref, sh_ref, w_ref, o_ref, h_ref, *, groups):
    @pl.when(pl.program_id(1) == 0)
    def _():
        _modulated_norm(x_ref, sc_ref, sh_ref, h_ref, groups)

    o_ref[...] = jnp.dot(h_ref[...], w_ref[...], preferred_element_type=F32).astype(o_ref.dtype)


def _norm_matmul(x, scale, shift, w, rows_per_batch, tm, tn):
    m, d = x.shape
    n = w.shape[1]
    tm = min(tm, m)
    groups = _rows_per_tile(rows_per_batch, tm)
    bidx = (lambda i, j: (i * tm // rows_per_batch, 0, 0)) if groups == 1 else (lambda i, j: (i, 0, 0))
    return pl.pallas_call(
        functools.partial(_norm_mm_kernel, groups=groups),
        grid=(m // tm, n // tn),
        in_specs=[pl.BlockSpec((tm, d), lambda i, j: (i, 0), pipeline_mode=pl.Buffered(1)),
                  pl.BlockSpec((groups, 1, d), bidx),
                  pl.BlockSpec((groups, 1, d), bidx),
                  pl.BlockSpec((d, tn), lambda i, j: (0, j))],
        out_specs=pl.BlockSpec((tm, tn), lambda i, j: (i, j)),
        out_shape=jax.ShapeDtypeStruct((m, n), F32),
        scratch_shapes=[pltpu.VMEM((tm, d), BF16)],
        compiler_params=_params(("parallel", "arbitrary")),
        name="norm_matmul",
    )(x, scale, shift, w)


def _out_res_kernel(a_ref, w_ref, x_ref, g_ref, o_ref, *, groups):
    acc = jnp.dot(a_ref[...], w_ref[...], preferred_element_type=F32)
    rows = acc.shape[0] // groups
    for gi in range(groups):
        sl = slice(gi * rows, (gi + 1) * rows)
        o_ref[sl, :] = x_ref[sl, :] + g_ref[gi] * acc[sl, :]


def _out_residual(a, w, x, gate, rows_per_batch, tm, tn):
    m, k = a.shape
    n = w.shape[1]
    tm = min(tm, m)
    groups = _rows_per_tile(rows_per_batch, tm)
    bidx = (lambda i, j: (i * tm // rows_per_batch, 0, j)) if groups == 1 else (lambda i, j: (i, 0, j))
    return pl.pallas_call(
        functools.partial(_out_res_kernel, groups=groups),
        grid=(m // tm, n // tn),
        in_specs=[pl.BlockSpec((tm, k), lambda i, j: (i, 0)),
                  pl.BlockSpec((k, tn), lambda i, j: (0, j)),
                  pl.BlockSpec((tm, tn), lambda i, j: (i, j)),
                  pl.BlockSpec((groups, 1, tn), bidx)],
        out_specs=pl.BlockSpec((tm, tn), lambda i, j: (i, j)),
        out_shape=jax.ShapeDtypeStruct((m, n), F32),
        compiler_params=_params(("parallel", "parallel")),
        name="out_residual",
    )(a, w, x, gate)


def _mlp_kernel(x_ref, sc_ref, sh_ref, xr_ref, g_ref, w1_ref, w2_ref, o_ref, h_ref, u_ref, *, groups, nf):
    j = pl.program_id(1)

    @pl.when(j == 0)
    def _():
        _modulated_norm(x_ref, sc_ref, sh_ref, h_ref, groups)

    @pl.when(j < nf)
    def _():
        u = jnp.dot(h_ref[...], w1_ref[...], preferred_element_type=F32)
        u_ref[j] = jnp.square(jnp.maximum(u, 0.0)).astype(BF16)

    @pl.when(j >= nf)
    def _():
        tf = u_ref.shape[2]
        acc = jnp.dot(u_ref[0], w2_ref[0:tf, :], preferred_element_type=F32)
        for f in range(1, nf):
            acc += jnp.dot(u_ref[f], w2_ref[f * tf:(f + 1) * tf, :], preferred_element_type=F32)
        rows = acc.shape[0] // groups
        for gi in range(groups):
            sl = slice(gi * rows, (gi + 1) * rows)
            o_ref[sl, :] = xr_ref[sl, :] + g_ref[gi] * acc[sl, :]


def _mlp(x, scale, shift, gate, w1, w2, rows_per_batch, tm, tf, tn):
    m, d = x.shape
    ff = w1.shape[1]
    tm = min(tm, m)
    nf, nn = ff // tf, d // tn
    groups = _rows_per_tile(rows_per_batch, tm)
    batch_of = (lambda i: i * tm // rows_per_batch) if groups == 1 else (lambda i: i)
    vec = pl.BlockSpec((groups, 1, d), lambda i, j: (batch_of(i), 0, 0))
    col = lambda j: jnp.maximum(j - nf, 0)
    return pl.pallas_call(
        functools.partial(_mlp_kernel, groups=groups, nf=nf),
        grid=(m // tm, nf + nn),
        in_specs=[pl.BlockSpec((tm, d), lambda i, j: (i, 0), pipeline_mode=pl.Buffered(1)),
                  vec, vec,
                  pl.BlockSpec((tm, tn), lambda i, j: (i, col(j))),
                  pl.BlockSpec((groups, 1, tn), lambda i, j: (batch_of(i), 0, col(j))),
                  pl.BlockSpec((d, tf), lambda i, j: (0, jnp.minimum(j, nf - 1))),
                  pl.BlockSpec((ff, tn), lambda i, j: (0, col(j)))],
        out_specs=pl.BlockSpec((tm, tn), lambda i, j: (i, col(j))),
        out_shape=jax.ShapeDtypeStruct((m, d), F32),
        scratch_shapes=[pltpu.VMEM((tm, d), BF16), pltpu.VMEM((nf, tm, tf), BF16)],
        compiler_params=_params(("parallel", "arbitrary"), MLP_VMEM_LIMIT),
        name="mlp",
    )(x, scale, shift, x, gate, w1, w2)


def _final_norm_kernel(x_ref, w_ref, o_ref):
    x = x_ref[...]
    ms = jnp.mean(x * x, axis=-1, keepdims=True)
    o_ref[...] = x * lax.rsqrt(ms + NORM_EPS) * w_ref[...]


def _final_norm(x, w, tm):
    m, d = x.shape
    tm = min(tm, m)
    return pl.pallas_call(
        _final_norm_kernel,
        grid=(m // tm,),
        in_specs=[pl.BlockSpec((tm, d), lambda i: (i, 0)), pl.BlockSpec((1, d), lambda i: (0, 0))],
        out_specs=pl.BlockSpec((tm, d), lambda i: (i, 0)),
        out_shape=jax.ShapeDtypeStruct((m, d), F32),
        compiler_params=_params(("parallel",)),
        name="final_norm",
    )(x, w)


def _hgrn_consts(c):
    row = _iota((c, c), 0)
    col = _iota((c, c), 1)
    ltri = (_iota((c, 3 * c), 0) >= _iota((c, 3 * c), 1) % c).astype(BF16)
    levels = []
    gs = 2 * DIAG
    while gs <= c:
        half = gs // 2
        pair = (row // gs == col // gs) & (row % gs >= half) & (col % gs < half)
        levels.append((gs, pair))
        gs *= 2
    rix = _iota((c, 1), 0)
    return ltri, levels, rix


def _hgrn_chunk(q, k, v, g, st, consts):
    ltri, levels, rix = consts
    c = q.shape[0]
    gc = _mask_mm_left(ltri, g)
    yield
    lhs, rhs = [q * jnp.exp(gc)], [st]
    for gs, _ in levels:
        half = gs // 2
        ref = jnp.concatenate(
            [jnp.broadcast_to(gc[m0 * gs + half - 1:m0 * gs + half, :], (gs, gc.shape[1])) for m0 in range(c // gs)],
            axis=0)
        is_q = (rix % gs) >= half
        lhs.append(jnp.where(is_q, q * jnp.exp(jnp.minimum(gc - ref, 0.0)), 0.0))
        rhs.append(jnp.where(is_q, 0.0, k * jnp.exp(jnp.minimum(ref - gc, 0.0))))
        rhs.append(jnp.zeros((LANES - c, k.shape[1]), F32))
    prod = _nt(jnp.concatenate(lhs, axis=0), jnp.concatenate(rhs, axis=0))
    gl = gc[c - 1:c, :]
    upd = _tn(v, k * jnp.exp(gl - gc))
    yield
    att = jnp.zeros((c, c), F32)
    for li, (_, pair) in enumerate(levels):
        att = att + jnp.where(pair, prod[(li + 1) * c:(li + 2) * c, (li + 1) * LANES:(li + 1) * LANES + c], 0.0)
    o = prod[:c, :LANES] + _nn(att, v)
    for dist in range(DIAG):
        if dist == 0:
            kd, gd, vd = k, gc, v
        else:
            kd = pltpu.roll(k, dist, 0)
            gd = pltpu.roll(gc, dist, 0)
            vd = pltpu.roll(v, dist, 0)
        w = jnp.sum(q * kd * jnp.exp(jnp.minimum(gc - gd, 0.0)), axis=-1, keepdims=True)
        o = o + jnp.where((rix % DIAG) >= dist, w, 0.0) * vd
    yield
    return o, st * jnp.exp(gl) + upd


def _hgrn_kernel(q_ref, f_ref, i_ref, g_ref, lb_ref, nw_ref, s0_ref, o_ref, so_ref, s_ref, *, dk):
    l = pl.program_id(2)

    @pl.when(l == 0)
    def _():
        s_ref[...] = s0_ref[...]

    t, width = q_ref.shape
    c = min(CHUNK, t)
    consts = _hgrn_consts(c)
    lb = lb_ref[...]
    nw = nw_ref[...]

    def body(ci, carry):
        sl = pl.ds(pl.multiple_of(ci * c, c), c)
        f = lb + (1.0 - lb) * _sigmoid(f_ref[sl, :])
        q = _silu(q_ref[sl, :]) * dk ** -0.5
        k = 1.0 - f
        v = i_ref[sl, :]
        g = jnp.log(f)
        gate = _silu(g_ref[sl, :])
        heads = [slice(h * dk, (h + 1) * dk) for h in range(width // dk)]
        results = _lockstep(_hgrn_chunk(q[:, hs], k[:, hs], v[:, hs], g[:, hs], s_ref[h], consts)
                            for h, hs in enumerate(heads))
        for h, (o, st) in enumerate(results):
            s_ref[h] = st
            ms = jnp.mean(o * o, axis=-1, keepdims=True)
            o_ref[sl, heads[h]] = (o * lax.rsqrt(ms + NORM_EPS) * nw * gate[:, heads[h]]).astype(o_ref.dtype)
        return carry

    lax.fori_loop(0, t // c, body, 0)

    @pl.when(l == pl.num_programs(2) - 1)
    def _():
        so_ref[...] = s_ref[...]


def _hgrn(p, lb, norm_w, s0t, batch, seq, heads, tile):
    dk = LANES
    tile = min(tile, seq)
    nl = seq // tile
    hb = HGRN_HEADS_PER_STEP
    assert heads % hb == 0
    groups = heads // hb
    width = hb * dk
    blk = lambda off: pl.BlockSpec((tile, width), lambda b, h, l, off=off: (b * nl + l, off * groups + h))
    st_spec = pl.BlockSpec((None, hb, dk, dk), lambda b, h, l: (b, h, 0, 0))
    return pl.pallas_call(
        functools.partial(_hgrn_kernel, dk=dk),
        grid=(batch, groups, nl),
        in_specs=[blk(0), blk(1), blk(2), blk(3),
                  pl.BlockSpec((1, width), lambda b, h, l: (0, h)),
                  pl.BlockSpec((1, dk), lambda b, h, l: (0, 0)),
                  st_spec],
        out_specs=[pl.BlockSpec((tile, width), lambda b, h, l: (b * nl + l, h)), st_spec],
        out_shape=[jax.ShapeDtypeStruct((batch * seq, heads * dk), BF16),
                   jax.ShapeDtypeStruct(s0t.shape, F32)],
        scratch_shapes=[pltpu.VMEM((hb, dk, dk), F32)],
        compiler_params=_params(("parallel", "parallel", "arbitrary")),
        name="hgrn2",
    )(p, p, p, p, lb, norm_w, s0t)


def _rwkv_consts(c):
    n = 4 * c
    row = _iota((n, n), 0)
    col = _iota((n, n), 1)
    same_head = ((row // c) % 2) == ((col // c) % 2)
    strict = (row % c) > (col % c)
    incl = (row % c) >= (col % c)
    keep = same_head & (strict | ((row >= 2 * c) & incl))
    ltri = (_iota((c, 3 * c), 0) >= _iota((c, 3 * c), 1) % c).astype(BF16)
    head0 = _iota((1, LANES), 1) < (LANES // 2)
    rs = _iota((LANES, LANES), 0)
    cs = _iota((LANES, LANES), 1)
    bdiag = (rs // (LANES // 2)) == (cs // (LANES // 2))
    return keep, ltri, head0, bdiag


def _rwkv_chunk(r, lg, k, v, a, b, s, consts):
    keep, ltri, head0, bdiag = consts
    c = r.shape[0]
    gc = _mask_mm_left(ltri, lg)
    yield
    e_incl = jnp.exp(gc)
    e_inv = jnp.exp(-gc)
    at = a * jnp.exp(gc - lg)
    rt = r * e_incl
    bt = b * e_inv
    kt = k * e_inv
    split = lambda x: [jnp.where(head0, x, 0.0), jnp.where(head0, 0.0, x)]
    lhs = jnp.concatenate(split(at) + split(rt), axis=0)
    rhs = jnp.concatenate([bt, bt, kt, kt, s], axis=0)
    pls = _nt(lhs, rhs)
    yield
    p = jnp.where(keep, pls[:, :4 * c], 0.0)
    ls = pls[:, 4 * c:]
    vst = jnp.concatenate(split(v), axis=0)
    pv = _nn(p[:, 2 * c:], vst)
    yield
    ust = yield from _nilpotent_solve(p[:2 * c, :2 * c], ls[:2 * c] + pv[:2 * c], int(math.log2(c)))
    ost = ls[2 * c:] + pv[2 * c:] + _nn(p[2 * c:, :2 * c], ust)
    yield
    o = ost[:c] + ost[c:]
    u = ust[:c] + ust[c:]
    gl = gc[c - 1:c, :]
    e_tail = jnp.exp(gl - gc)
    upd = _tn(jnp.concatenate([u, v], axis=0), jnp.concatenate([b * e_tail, k * e_tail], axis=0))
    yield
    return o, s * jnp.exp(gl) + jnp.where(bdiag, upd, 0.0)


def _rwkv_kernel(xr_ref, xk_ref, xv_ref, xw_ref, xa_ref, xg_ref,
                 hr_ref, hk_ref, hv_ref, hw_ref, ha_ref, hg_ref,
                 vec_ref, mus_ref, w2_ref, a2_ref, g2_ref, s0_ref,
                 o_ref, so_ref,
                 s_ref, buf_ref, bufs_ref, bufg_ref, r_ref, lg_ref, k_ref, v_ref, a_ref, b_ref, y_ref):
    l = pl.program_id(2)
    t, width = xr_ref.shape
    c = min(CHUNK, t)
    pad = 8

    @pl.when(l == 0)
    def _():
        s_ref[...] = s0_ref[...]
        for i, h_ref in enumerate((hr_ref, hk_ref, hv_ref)):
            buf_ref[i, pad - 1:pad, :] = h_ref[...]
        for i, h_ref in enumerate((hw_ref, ha_ref)):
            bufs_ref[i, pad - 1:pad, :] = h_ref[...]
        bufg_ref[pad - 1:pad, :] = hg_ref[...]

    def shifted(buf, x_ref, mu):
        x = x_ref[...]
        buf[pad:pad + t, :] = x
        prev = buf[pad - 1:pad - 1 + t, :]
        buf[pad - 1:pad, :] = x[t - 1:t, :]
        return x + (prev - x) * mu

    vec = vec_ref[...]
    mus = mus_ref[...]
    r = shifted(buf_ref.at[0], xr_ref, vec[0:1])
    k = shifted(buf_ref.at[1], xk_ref, vec[1:2])
    v = shifted(buf_ref.at[2], xv_ref, vec[2:3])
    xw = shifted(bufs_ref.at[0], xw_ref, mus[0:1, :LANES])
    xa = shifted(bufs_ref.at[1], xa_ref, mus[1:2, :LANES])
    xg = shifted(bufg_ref, xg_ref, mus[2:3])

    w0, a0, k_k, k_a, r_k, ln_w, ln_b = (vec[i:i + 1] for i in range(3, 10))
    half = LANES // 2
    ones3 = ((_iota((3 * width, width), 0) % width) // half == _iota((3 * width, width), 1) // half).astype(BF16)
    w_log = -_softplus(-(w0 + _nn(jnp.tanh(xw), w2_ref[...]))) - 0.5
    a_lr = _sigmoid(a0 + _nn(xa, a2_ref[...]))
    gate = _nn(_sigmoid(xg), g2_ref[...])
    kk = k * k_k
    kk = kk * lax.rsqrt(_mask_mm_right(kk * kk, ones3) + 1e-12)
    k = k * (1.0 + (a_lr - 1.0) * k_a)
    r_ref[...] = r
    lg_ref[...] = -jnp.exp(w_log)
    k_ref[...] = k
    v_ref[...] = v
    a_ref[...] = -kk
    b_ref[...] = kk * a_lr

    consts = _rwkv_consts(c)

    def body(ci, carry):
        sl = pl.ds(pl.multiple_of(ci * c, c), c)
        operands = []
        for pi in range(width // LANES):
            cols = slice(pi * LANES, (pi + 1) * LANES)
            operands.append((r_ref[sl, cols], lg_ref[sl, cols], k_ref[sl, cols], v_ref[sl, cols],
                             a_ref[sl, cols], b_ref[sl, cols], s_ref[pi]))
        results = _lockstep(_rwkv_chunk(*ops, consts) for ops in operands)
        for pi, (o, s_new) in enumerate(results):
            s_ref[pi] = s_new
            y_ref[sl, pi * LANES:(pi + 1) * LANES] = o
        return carry

    lax.fori_loop(0, t // c, body, 0)

    o = y_ref[...]
    inv_n = 1.0 / half
    mean = _mask_mm_right(o, ones3) * inv_n
    dev = o - mean
    var = _mask_mm_right(dev * dev, ones3) * inv_n
    o = dev * lax.rsqrt(var + RWKV_LN_EPS) * ln_w + ln_b
    o = o + _mask_mm_right(r * k * r_k, ones3) * v
    o_ref[...] = (o * gate).astype(o_ref.dtype)

    @pl.when(l == pl.num_programs(2) - 1)
    def _():
        so_ref[...] = s_ref[...]


def _rwkv(p, shift, vecs, mus, w2, a2, g2, s0, batch, seq, pairs, col0, tile):
    tile = min(tile, seq)
    nl = seq // tile
    wb = pairs * LANES
    pb = RWKV_PAIRS_PER_STEP
    width = pb * LANES
    groups = pairs // pb
    assert pairs % pb == 0 and col0 % width == 0
    cb0 = col0 // LANES
    gb0 = col0 // width
    xg_off = -(-(3 * wb + 2 * LANES) // 512) * 512
    assert (col0 + xg_off) % 512 == 0
    pblk = lambda off: pl.BlockSpec((tile, width), lambda b, h, l, off=off: (b * nl + l, gb0 + off * groups + h))
    pfix = lambda cb: pl.BlockSpec((tile, LANES), lambda b, h, l, cb=cb: (b * nl + l, cb))
    hblk = lambda off: pl.BlockSpec((None, 1, width), lambda b, h, l, off=off: (b, 0, off * groups + h))
    hfix = lambda cb: pl.BlockSpec((None, 1, LANES), lambda b, h, l, cb=cb: (b, 0, cb))
    st_spec = pl.BlockSpec((None, pb, LANES, LANES), lambda b, h, l: (b, h, 0, 0))
    col = lambda rows: pl.BlockSpec((rows, width), lambda b, h, l: (0, h))
    full = lambda a: pl.BlockSpec(a.shape, lambda b, h, l: (0,) * a.ndim)
    tbuf = pltpu.VMEM((tile, width), F32)
    return pl.pallas_call(
        _rwkv_kernel,
        grid=(batch, groups, nl),
        in_specs=[pblk(0), pblk(1), pblk(2), pfix(cb0 + 3 * pairs), pfix(cb0 + 3 * pairs + 1),
                  pl.BlockSpec((tile, 512), lambda b, h, l: (b * nl + l, (col0 + xg_off) // 512)),
                  hblk(0), hblk(1), hblk(2), hfix(3 * pairs), hfix(3 * pairs + 1),
                  pl.BlockSpec((None, 1, 512), lambda b, h, l: (b, 0, xg_off // 512)),
                  col(vecs.shape[0]), full(mus), col(w2.shape[0]), col(a2.shape[0]), col(g2.shape[0]), st_spec],
        out_specs=[pl.BlockSpec((tile, width), lambda b, h, l: (b * nl + l, h)), st_spec],
        out_shape=[jax.ShapeDtypeStruct((batch * seq, wb), BF16), jax.ShapeDtypeStruct(s0.shape, F32)],
        scratch_shapes=[pltpu.VMEM((pb, LANES, LANES), F32),
                        pltpu.VMEM((3, tile + 8, width), F32),
                        pltpu.VMEM((2, tile + 8, LANES), F32),
                        pltpu.VMEM((tile + 8, 512), F32),
                        tbuf, tbuf, tbuf, tbuf, tbuf, tbuf, tbuf],
        compiler_params=_params(("parallel", "parallel", "arbitrary")),
        name="rwkv7",
    )(p, p, p, p, p, p, shift, shift, shift, shift, shift, shift, vecs, mus, w2, a2, g2, s0)


def _gdn_consts(c):
    n = 2 * c
    row = _iota((n, n), 0)
    col = _iota((n, n), 1)
    same = (row // c) == (col // c)
    incl = same & ((row % c) >= (col % c))
    strict = same & ((row % c) > (col % c))
    r3 = _iota((n, 3 * n), 0)
    c3 = _iota((n, 3 * n), 1) % n
    ltri3 = (((r3 // c) == (c3 // c)) & ((r3 % c) >= (c3 % c))).astype(BF16)
    return ltri3, incl, strict, strict.astype(F32)


def _gdn_pair_chunk(q, k, v, g, beta, states, consts):
    ltri3, incl, strict, strict_f = consts
    n = q.shape[0]
    c = n // 2
    cum = _mask_mm_left(ltri3, jnp.concatenate([g * strict_f, jnp.broadcast_to(g, (n, LANES))], axis=1))
    yield
    gc = cum[:, n:]
    dec = jnp.exp(jnp.where(incl, cum[:, :n], -1e30))
    kq = _nt(jnp.concatenate([k, q], axis=0), k)
    yield
    a_mat = jnp.where(strict, beta * kq[:n] * dec, 0.0)
    eg = jnp.exp(gc)
    uw = yield from _nilpotent_solve(-a_mat, jnp.concatenate([beta * v, beta * eg * k], axis=1), int(math.log2(c)))
    u, w = uw[:, :LANES], uw[:, LANES:]
    head0 = _iota((n, 1), 0) < c
    head0_2 = (_iota((2 * n, 1), 0) % n) < c
    by_head = lambda x, m: jnp.concatenate([jnp.where(m, x, 0.0), jnp.where(m, 0.0, x)], axis=1)
    wq = jnp.concatenate([w, q * eg], axis=0)
    ws = _nn(by_head(wq, head0_2), jnp.concatenate(states, axis=0))
    yield
    v_new = u - ws[:n]
    o = ws[n:] + _nn(kq[n:] * dec, v_new)
    yield
    gl = jnp.where(head0, gc[c - 1:c, :], gc[n - 1:n, :])
    upd = _tn(k * jnp.exp(gl - gc), by_head(v_new, head0))
    yield
    new_states = [s * jnp.exp(gc[(hi + 1) * c - 1:(hi + 1) * c, :]) + upd[:, hi * LANES:(hi + 1) * LANES]
                  for hi, s in enumerate(states)]
    return o, new_states


def _gdn_kernel(q_ref, k_ref, v_ref, z_ref, ba_ref, cwq_ref, cwk_ref, cwv_ref, cq_ref, ck_ref, cv_ref,
                par_ref, nw_ref, s0_ref, o_ref, so_ref, s_ref, buf_ref, qs_ref, ks_ref, vs_ref,
                *, heads, conv_w):
    hg = pl.program_id(1)
    l = pl.program_id(2)
    t, width = q_ref.shape
    hb = width // LANES
    c = min(CHUNK, t)
    pad = 8
    hist = conv_w - 1
    streams = ((q_ref, cq_ref, cwq_ref, qs_ref), (k_ref, ck_ref, cwk_ref, ks_ref), (v_ref, cv_ref, cwv_ref, vs_ref))

    @pl.when(l == 0)
    def _():
        s_ref[...] = s0_ref[...]
        for i, (_, c_ref, _, _) in enumerate(streams):
            buf_ref[i, pad - hist:pad, :] = c_ref[...]

    for i, (x_ref, _, w_ref, dst_ref) in enumerate(streams):
        buf_ref[i, pad:pad + t, :] = x_ref[...]
        w = w_ref[...]
        y = jnp.zeros((t, width), F32)
        for j in range(conv_w):
            y = y + buf_ref[i, pad - hist + j:pad - hist + j + t, :] * w[j:j + 1]
        tail = buf_ref[i, pad + t - hist:pad + t, :]
        buf_ref[i, pad - hist:pad, :] = tail
        dst_ref[...] = _silu(y)

    par = par_ref[...]
    lane = _iota((1, LANES), 1)
    consts = _gdn_consts(c)
    nw = nw_ref[...]

    def body(ci, carry):
        sl = pl.ds(pl.multiple_of(ci * c, c), c)
        bac = ba_ref[sl, :]
        bac = jnp.where(lane < heads, _sigmoid(bac), -jnp.exp(par[0:1]) * _softplus(bac + par[1:2]))
        column = lambda idx: jnp.sum(jnp.where(lane == idx, bac, 0.0), axis=-1, keepdims=True)
        operands = []
        for pi in range(hb // 2):
            local = (2 * pi, 2 * pi + 1)
            cols = [slice(h * LANES, (h + 1) * LANES) for h in local]
            stack = lambda ref: jnp.concatenate([ref[sl, cs] for cs in cols], axis=0)
            beta = jnp.concatenate([column(hg * hb + h) for h in local], axis=0)
            g = jnp.concatenate([column(heads + hg * hb + h) for h in local], axis=0)
            operands.append((stack(qs_ref), stack(ks_ref), stack(vs_ref), g, beta, [s_ref[h] for h in local],
                             [z_ref[sl, cs] for cs in cols]))
        chains = []
        for q, k, v, g, beta, states, _ in operands:
            q = q * lax.rsqrt(jnp.sum(q * q, axis=-1, keepdims=True) + 1e-12) * LANES ** -0.5
            k = k * lax.rsqrt(jnp.sum(k * k, axis=-1, keepdims=True) + 1e-12)
            chains.append(_gdn_pair_chunk(q, k, v, g, beta, states, consts))
        results = _lockstep(chains)
        for pi, (o, new_states) in enumerate(results):
            z = operands[pi][6]
            for i in range(2):
                h = 2 * pi + i
                s_ref[h] = new_states[i]
                oh = o[i * c:(i + 1) * c]
                ms = jnp.mean(oh * oh, axis=-1, keepdims=True)
                o_ref[sl, h * LANES:(h + 1) * LANES] = (
                    oh * lax.rsqrt(ms + NORM_EPS) * nw * _silu(z[i])).astype(o_ref.dtype)
        return carry

    lax.fori_loop(0, t // c, body, 0)

    @pl.when(l == pl.num_programs(2) - 1)
    def _():
        so_ref[...] = s_ref[...]


def _gdn(p, conv_w, conv_state, par, norm_w, s0, batch, seq, heads, tile):
    tile = min(tile, seq)
    nl = seq // tile
    cw = conv_w.shape[0]
    hb = GDN_HEADS_PER_STEP
    assert heads % hb == 0 and hb % 2 == 0
    groups = heads // hb
    width = hb * LANES
    pblk = lambda off: pl.BlockSpec((tile, width), lambda b, h, l, off=off: (b * nl + l, off * groups + h))
    wblk = lambda off: pl.BlockSpec((cw, width), lambda b, h, l, off=off: (0, off * groups + h))
    cblk = lambda off: pl.BlockSpec((None, cw - 1, width), lambda b, h, l, off=off: (b, 0, off * groups + h))
    st_spec = pl.BlockSpec((None, hb, LANES, LANES), lambda b, h, l: (b, h, 0, 0))
    tbuf = pltpu.VMEM((tile, width), F32)
    return pl.pallas_call(
        functools.partial(_gdn_kernel, heads=heads, conv_w=cw),
        grid=(batch, groups, nl),
        in_specs=[pblk(0), pblk(1), pblk(2), pblk(3),
                  pl.BlockSpec((tile, LANES), lambda b, h, l: (b * nl + l, 4 * heads)),
                  wblk(0), wblk(1), wblk(2), cblk(0), cblk(1), cblk(2),
                  pl.BlockSpec(par.shape, lambda b, h, l: (0, 0)),
                  pl.BlockSpec((1, LANES), lambda b, h, l: (0, 0)),
                  st_spec],
        out_specs=[pl.BlockSpec((tile, width), lambda b, h, l: (b * nl + l, h)), st_spec],
        out_shape=[jax.ShapeDtypeStruct((batch * seq, heads * LANES), BF16), jax.ShapeDtypeStruct(s0.shape, F32)],
        scratch_shapes=[pltpu.VMEM((hb, LANES, LANES), F32),
                        pltpu.VMEM((3, tile + 8, width), F32),
                        tbuf, tbuf, tbuf],
        compiler_params=_params(("parallel", "parallel", "arbitrary")),
        name="gdn",
    )(p, p, p, p, p, conv_w, conv_w, conv_w, conv_state, conv_state, conv_state, par, norm_w, s0)


def _pad_cols(w, total):
    return jnp.pad(w, ((0, 0),) * (w.ndim - 1) + ((0, total - w.shape[-1]),))


def _ab_layout(a_cols, wb, lora_w, lora_a, lora_g):
    assert lora_w == LANES and lora_a == LANES and lora_g <= 512
    xg_off = -(-(3 * wb + 2 * LANES) // 512) * 512
    b_width = xg_off + 512
    assert a_cols % 512 == 0
    return xg_off, b_width


def _rwkv_cols(t, wb, xg_off, b_width, lora_g):
    head = t[..., :3 * wb + 2 * LANES]
    tail = t[..., 3 * wb + 2 * LANES:]
    z = lambda n: jnp.zeros(t.shape[:-1] + (n,), t.dtype)
    return jnp.concatenate([head, z(xg_off - head.shape[-1]), tail, z(b_width - xg_off - lora_g)], axis=-1)


def _trunk(x, mod, s_hgrn, s_rwkv, s_shift, s_gdn, s_conv, wts, tiles):
    batch, seq, d = x.shape
    m = batch * seq
    x2 = x.reshape(m, d)
    tm, t_hgrn, t_rwkv, t_gdn = tiles
    outs = {}
    depth = mod.shape[0]
    for layer in range(depth):
        j = layer // 2
        vecs6 = [mod[layer, :, i * d:(i + 1) * d].reshape(batch, 1, d) for i in range(6)]
        sh1, sc1, g1, sh2, sc2, g2 = vecs6
        if layer % 2 == 0:
            w = wts["ab"][j]
            p = _norm_matmul(x2, sc1, sh1, w["w_in"], seq, tm, 512)
            ha, wb = w["ha"], w["wb"]
            oa, st_a = _hgrn(p, w["lb"], w["hgrn_norm_w"], jnp.swapaxes(s_hgrn[j], -1, -2), batch, seq, ha, t_hgrn)
            pairs = wb // LANES
            hb = s_rwkv.shape[2]
            nb = s_rwkv.shape[-1]
            sp = s_rwkv[j].reshape(batch, pairs, 2, nb, nb)
            zero = jnp.zeros_like(sp[:, :, 0])
            s0 = jnp.concatenate([jnp.concatenate([sp[:, :, 0], zero], -1),
                                  jnp.concatenate([zero, sp[:, :, 1]], -1)], -2)
            shift = _rwkv_cols(s_shift[j], wb, w["xg_off"], w["b_width"], w["lora_g"])
            ob, st_b = _rwkv(p, shift, w["vecs"], w["mus"], w["w2"], w["a2"], w["g2"], s0,
                             batch, seq, pairs, w["a_cols"], t_rwkv)
            mix = jnp.concatenate([oa, ob], axis=-1)
            outs.setdefault("hgrn", []).append(jnp.swapaxes(st_a, -1, -2))
            outs.setdefault("rwkv", []).append(
                jnp.stack([st_b[:, :, :nb, :nb], st_b[:, :, nb:, nb:]], axis=2).reshape(batch, hb, nb, nb))
            last = p.reshape(batch, seq, -1)[:, seq - 1:, w["a_cols"]:]
            n_head = 3 * wb + 2 * LANES
            outs.setdefault("shift", []).append(
                jnp.concatenate([last[..., :n_head], last[..., w["xg_off"]:w["xg_off"] + w["lora_g"]]], axis=-1))
            w_out = w["w_out"]
        else:
            w = wts["gdn"][j]
            p = _norm_matmul(x2, sc1, sh1, w["w_in"], seq, tm, 512)
            hc = w["hc"]
            mix, st_c = _gdn(p, w["conv_w"], s_conv[j], w["par"], w["norm_w"], s_gdn[j], batch, seq, hc, t_gdn)
            outs.setdefault("gdn", []).append(st_c)
            cw = w["conv_w"].shape[0]
            raw = p.reshape(batch, seq, -1)[:, :, :3 * hc * LANES]
            prev = jnp.concatenate([s_conv[j], raw[:, max(seq - (cw - 1), 0):]], axis=1)
            outs.setdefault("conv", []).append(prev[:, -(cw - 1):])
            w_out = w["w_out"]
        x2 = _out_residual(mix, w_out, x2, g1, seq, tm, 512)
        x2 = _mlp(x2, sc2, sh2, g2, wts["mlp_w1"][layer], wts["mlp_w2"][layer], seq, min(tm, 512), 512, 256)
    x2 = _final_norm(x2, wts["final_w"], 256)
    st = lambda name: jnp.stack(outs[name])
    return x2.reshape(batch, seq, d), st("hgrn"), st("rwkv"), st("shift"), st("gdn"), st("conv")


def kernel(x_prompt, x_sample, c_prompt, c_sample, state_hgrn, state_rwkv, state_rwkv_shift, state_gdn,
           state_gdn_conv, ada_w, ada_b, mlp_w1, mlp_w2, final_norm_w, ab_w_in, ab_w_out, hgrn_lb_logits,
           hgrn_norm_w, rwkv_mu, rwkv_w0, rwkv_w2, rwkv_a0, rwkv_a2, rwkv_g2, rwkv_k_k, rwkv_k_a, rwkv_r_k,
           rwkv_ln_w, rwkv_ln_b, gdn_w_in, gdn_w_out, gdn_conv_w, gdn_a_log, gdn_dt_bias, gdn_norm_w):
    d = x_prompt.shape[-1]
    nb_p, nb_s = x_prompt.shape[0], x_sample.shape[0]
    n_ab, n_c = ab_w_in.shape[0], gdn_w_in.shape[0]
    ha = state_hgrn.shape[2]
    a_cols = 4 * ha * LANES
    wb = rwkv_w0.shape[1]
    lora_w, lora_a, lora_g = rwkv_w2.shape[1], rwkv_a2.shape[1], rwkv_g2.shape[1]
    xg_off, b_width = _ab_layout(a_cols, wb, lora_w, lora_a, lora_g)
    hc = state_gdn.shape[2]

    rows = nb_p + nb_s
    rows_pad = -(-rows // 8) * 8
    c_all = jnp.pad(jnp.concatenate([c_prompt, c_sample], axis=0), ((0, rows_pad - rows), (0, 0)))
    mod = _adaln(c_all, ada_w, ada_b)

    lbs = jnp.cumsum(jax.nn.softmax(hgrn_lb_logits.astype(F32), axis=0), axis=0)
    wts = {"ab": [], "gdn": [], "final_w": final_norm_w.reshape(1, d),
           "mlp_w1": mlp_w1.astype(BF16), "mlp_w2": mlp_w2.astype(BF16)}
    for j in range(n_ab):
        w_in = ab_w_in[j]
        w_in = jnp.concatenate([w_in[:, :a_cols], _rwkv_cols(w_in[:, a_cols:], wb, xg_off, b_width, lora_g)], axis=-1)
        mu = _rwkv_cols(rwkv_mu[j][None], wb, xg_off, b_width, lora_g)[0]
        vec_rows = [mu[:wb], mu[wb:2 * wb], mu[2 * wb:3 * wb], rwkv_w0[j], rwkv_a0[j], rwkv_k_k[j], rwkv_k_a[j],
                    rwkv_r_k[j].reshape(-1), rwkv_ln_w[j], rwkv_ln_b[j]]
        vecs = jnp.pad(jnp.stack(vec_rows), ((0, 16 - len(vec_rows)), (0, 0)))
        mus = jnp.stack([_pad_cols(mu[3 * wb:3 * wb + LANES], 512), _pad_cols(mu[3 * wb + LANES:3 * wb + 2 * LANES], 512),
                         mu[xg_off:xg_off + 512]])
        wts["ab"].append({
            "w_in": w_in.astype(BF16), "w_out": ab_w_out[j].astype(BF16), "lb": lbs[j][None],
            "hgrn_norm_w": hgrn_norm_w[j][None], "vecs": vecs, "mus": jnp.pad(mus, ((0, 5), (0, 0))),
            "w2": rwkv_w2[j].astype(BF16), "a2": rwkv_a2[j].astype(BF16),
            "g2": jnp.pad(rwkv_g2[j], ((0, 512 - lora_g), (0, 0))).astype(BF16),
            "ha": ha, "wb": wb, "a_cols": a_cols, "xg_off": xg_off, "b_width": b_width, "lora_g": lora_g})
    for j in range(n_c):
        cols = gdn_w_in.shape[-1]
        cols_pad = -(-cols // 512) * 512
        par = jnp.zeros((8, LANES), F32)
        par = par.at[0, hc:2 * hc].set(gdn_a_log[j]).at[1, hc:2 * hc].set(gdn_dt_bias[j])
        wts["gdn"].append({
            "w_in": _pad_cols(gdn_w_in[j], cols_pad).astype(BF16), "w_out": gdn_w_out[j].astype(BF16),
            "conv_w": gdn_conv_w[j], "par": par, "norm_w": gdn_norm_w[j][None], "hc": hc})

    zeros = lambda s: jnp.zeros((s.shape[0], nb_p) + s.shape[2:], x_prompt.dtype)
    y_p, hg_p, rw_p, sh_p, gd_p, cv_p = _trunk(
        x_prompt, mod[:, :nb_p], zeros(state_hgrn), zeros(state_rwkv), zeros(state_rwkv_shift),
        zeros(state_gdn), zeros(state_gdn_conv), wts, PROMPT_TILES)
    y_s, hg_s, rw_s, sh_s, gd_s, cv_s = _trunk(
        x_sample, mod[:, nb_p:rows], state_hgrn, state_rwkv, state_rwkv_shift, state_gdn, state_gdn_conv,
        wts, SAMPLE_TILES)
    return (y_p, y_s, hg_p, rw_p, sh_p, gd_p, cv_p, hg_s, rw_s, sh_s, gd_s, cv_s)
```

```python
import functools
import math

import jax
import jax.numpy as jnp
from jax import lax
from jax.experimental import pallas as pl
from jax.experimental.pallas import tpu as pltpu

F32 = jnp.float32
BF16 = jnp.bfloat16
HI = lax.Precision.HIGHEST

NORM_EPS = 1e-6
RWKV_LN_EPS = 64e-5
CHUNK = 64
DIAG = 8
LANES = 128
VMEM_LIMIT = 56 * 1024 * 1024
MLP_VMEM_LIMIT = 60 * 1024 * 1024
PROMPT_TILES = (1024, 512, 256, 256)
SAMPLE_TILES = (512, 64, 64, 64)
HGRN_HEADS_PER_STEP = 4
RWKV_PAIRS_PER_STEP = 4
GDN_HEADS_PER_STEP = 8


def _mm(a, b, ca, cb, exact=False):
    dims = (((ca,), (cb,)), ((), ()))
    if exact:
        return lax.dot_general(a, b, dims, precision=HI, preferred_element_type=F32)
    return lax.dot_general(a.astype(BF16), b.astype(BF16), dims, preferred_element_type=F32)


def _nn(a, b, exact=False):
    return _mm(a, b, 1, 0, exact)


def _nt(a, b, exact=False):
    return _mm(a, b, 1, 1, exact)


def _tn(a, b, exact=False):
    return _mm(a, b, 0, 0, exact)


def _sigmoid(x):
    return 1.0 / (1.0 + jnp.exp(-x))


def _silu(x):
    return x * _sigmoid(x)


def _softplus(x):
    return jnp.maximum(x, 0.0) + jnp.log(1.0 + jnp.exp(-jnp.abs(x)))


def _iota(shape, dim):
    return lax.broadcasted_iota(jnp.int32, shape, dim)


def _split2(x):
    hi = x.astype(BF16)
    return hi, (x - hi.astype(F32)).astype(BF16)


def _split3(x):
    hi = x.astype(BF16)
    r = x - hi.astype(F32)
    lo = r.astype(BF16)
    return hi, lo, (r - lo.astype(F32)).astype(BF16)


def _mm_hi(a, b, a_parts=None, b_parts=None):
    a_hi, a_lo = a_parts if a_parts is not None else _split2(a)
    b_hi, b_lo = b_parts if b_parts is not None else _split2(b)
    return jnp.dot(jnp.concatenate([a_hi, a_lo, a_hi], axis=1), jnp.concatenate([b_hi, b_hi, b_lo], axis=0),
                   preferred_element_type=F32)


def _mask_mm_left(mask3, x):
    return jnp.dot(mask3, jnp.concatenate(_split3(x), axis=0), preferred_element_type=F32)


def _mask_mm_right(x, mask3):
    return jnp.dot(jnp.concatenate(_split3(x), axis=1), mask3, preferred_element_type=F32)


def _nilpotent_solve(m, x, stages, hi_lo=True):
    n = m.shape[0]
    mm = _mm_hi if hi_lo else _nn
    for k in range(stages):
        last = k == stages - 1
        y = mm(m, x if last else jnp.concatenate([m, x], axis=1))
        yield
        if last:
            x = x + y
        else:
            m = y[:, :n]
            x = x + y[:, n:]
    return x


def _lockstep(generators):
    generators = list(generators)
    results = [None] * len(generators)
    live = list(range(len(generators)))
    while live:
        for i in list(live):
            try:
                next(generators[i])
            except StopIteration as stop:
                results[i] = stop.value
                live.remove(i)
    return results


def _params(sem, vmem=VMEM_LIMIT):
    return pltpu.CompilerParams(dimension_semantics=sem, vmem_limit_bytes=vmem)


def _rows_per_tile(rows_per_batch, tile):
    if tile <= rows_per_batch:
        assert rows_per_batch % tile == 0
        return 1
    assert tile % rows_per_batch == 0
    return tile // rows_per_batch


def _adaln_kernel(c_ref, w_ref, b_ref, o_ref):
    c = _silu(c_ref[...]).astype(BF16)
    o_ref[...] = jnp.dot(c, w_ref[...].astype(BF16), preferred_element_type=F32) + b_ref[...]


def _adaln(c, ada_w, ada_b):
    depth, d, n = ada_w.shape
    rows = c.shape[0]
    tn = 512 if n % 512 == 0 else n
    return pl.pallas_call(
        _adaln_kernel,
        grid=(depth, n // tn),
        in_specs=[pl.BlockSpec((rows, d), lambda l, j: (0, 0)),
                  pl.BlockSpec((None, d, tn), lambda l, j: (l, 0, j)),
                  pl.BlockSpec((None, 1, tn), lambda l, j: (l, 0, j))],
        out_specs=pl.BlockSpec((None, rows, tn), lambda l, j: (l, 0, j)),
        out_shape=jax.ShapeDtypeStruct((depth, rows, n), F32),
        compiler_params=_params(("parallel", "parallel")),
        name="adaln",
    )(c, ada_w, ada_b.reshape(depth, 1, n))


def _modulated_norm(x_ref, sc_ref, sh_ref, h_ref, groups):
    rows = x_ref.shape[0] // groups
    step = min(rows, 128)

    for gi in range(groups):
        sc = 1.0 + sc_ref[gi]
        sh = sh_ref[gi]

        def body(i, carry, gi=gi, sc=sc, sh=sh):
            sl = pl.ds(pl.multiple_of(gi * rows + i * step, step), step)
            x = x_ref[sl, :]
            ms = jnp.mean(x * x, axis=-1, keepdims=True)
            h_ref[sl, :] = (x * lax.rsqrt(ms + NORM_EPS) * sc + sh).astype(BF16)
            return carry

        lax.fori_loop(0, rows // step, body, 0)


def _norm_mm_kernel(x_ref, sc_ref, sh_ref, w_ref, o_ref, h_ref, *, groups):
    @pl.when(pl.program_id(1) == 0)
    def _():
        _modulated_norm(x_ref, sc_ref, sh_ref, h_ref, groups)

    o_ref[...] = jnp.dot(h_ref[...], w_ref[...], preferred_element_type=F32).astype(o_ref.dtype)


def _norm_matmul(x, scale, shift, w, rows_per_batch, tm, tn):
    m, d = x.shape
    n = w.shape[1]
    tm = min(tm, m)
    groups = _rows_per_tile(rows_per_batch, tm)
    bidx = (lambda i, j: (i * tm // rows_per_batch, 0, 0)) if groups == 1 else (lambda i, j: (i, 0, 0))
    return pl.pallas_call(
        functools.partial(_norm_mm_kernel, groups=groups),
        grid=(m // tm, n // tn),
        in_specs=[pl.BlockSpec((tm, d), lambda i, j: (i, 0), pipeline_mode=pl.Buffered(1)),
                  pl.BlockSpec((groups, 1, d), bidx),
                  pl.BlockSpec((groups, 1, d), bidx),
                  pl.BlockSpec((d, tn), lambda i, j: (0, j))],
        out_specs=pl.BlockSpec((tm, tn), lambda i, j: (i, j)),
        out_shape=jax.ShapeDtypeStruct((m, n), F32),
        scratch_shapes=[pltpu.VMEM((tm, d), BF16)],
        compiler_params=_params(("parallel", "arbitrary")),
        name="norm_matmul",
    )(x, scale, shift, w)


def _out_res_kernel(a_ref, w_ref, x_ref, g_ref, o_ref, *, groups):
    acc = jnp.dot(a_ref[...], w_ref[...], preferred_element_type=F32)
    rows = acc.shape[0] // groups
    for gi in range(groups):
        sl = slice(gi * rows, (gi + 1) * rows)
        o_ref[sl, :] = x_ref[sl, :] + g_ref[gi] * acc[sl, :]


def _out_residual(a, w, x, gate, rows_per_batch, tm, tn):
    m, k = a.shape
    n = w.shape[1]
    tm = min(tm, m)
    groups = _rows_per_tile(rows_per_batch, tm)
    bidx = (lambda i, j: (i * tm // rows_per_batch, 0, j)) if groups == 1 else (lambda i, j: (i, 0, j))
    return pl.pallas_call(
        functools.partial(_out_res_kernel, groups=groups),
        grid=(m // tm, n // tn),
        in_specs=[pl.BlockSpec((tm, k), lambda i, j: (i, 0)),
                  pl.BlockSpec((k, tn), lambda i, j: (0, j)),
                  pl.BlockSpec((tm, tn), lambda i, j: (i, j)),
                  pl.BlockSpec((groups, 1, tn), bidx)],
        out_specs=pl.BlockSpec((tm, tn), lambda i, j: (i, j)),
        out_shape=jax.ShapeDtypeStruct((m, n), F32),
        compiler_params=_params(("parallel", "parallel")),
        name="out_residual",
    )(a, w, x, gate)


def _mlp_kernel(x_ref, sc_ref, sh_ref, xr_ref, g_ref, w1_ref, w2_ref, o_ref, h_ref, u_ref, *, groups, nf):
    j = pl.program_id(1)

    @pl.when(j == 0)
    def _():
        _modulated_norm(x_ref, sc_ref, sh_ref, h_ref, groups)

    @pl.when(j < nf)
    def _():
        u = jnp.dot(h_ref[...], w1_ref[...], preferred_element_type=F32)
        u_ref[j] = jnp.square(jnp.maximum(u, 0.0)).astype(BF16)

    @pl.when(j >= nf)
    def _():
        tf = u_ref.shape[2]
        acc = jnp.dot(u_ref[0], w2_ref[0:tf, :], preferred_element_type=F32)
        for f in range(1, nf):
            acc += jnp.dot(u_ref[f], w2_ref[f * tf:(f + 1) * tf, :], preferred_element_type=F32)
        rows = acc.shape[0] // groups
        for gi in range(groups):
            sl = slice(gi * rows, (gi + 1) * rows)
            o_ref[sl, :] = xr_ref[sl, :] + g_ref[gi] * acc[sl, :]


def _mlp(x, scale, shift, gate, w1, w2, rows_per_batch, tm, tf, tn):
    m, d = x.shape
    ff = w1.shape[1]
    tm = min(tm, m)
    nf, nn = ff // tf, d // tn
    groups = _rows_per_tile(rows_per_batch, tm)
    batch_of = (lambda i: i * tm // rows_per_batch) if groups == 1 else (lambda i: i)
    vec = pl.BlockSpec((groups, 1, d), lambda i, j: (batch_of(i), 0, 0))
    col = lambda j: jnp.maximum(j - nf, 0)
    return pl.pallas_call(
        functools.partial(_mlp_kernel, groups=groups, nf=nf),
        grid=(m // tm, nf + nn),
        in_specs=[pl.BlockSpec((tm, d), lambda i, j: (i, 0), pipeline_mode=pl.Buffered(1)),
                  vec, vec,
                  pl.BlockSpec((tm, tn), lambda i, j: (i, col(j))),
                  pl.BlockSpec((groups, 1, tn), lambda i, j: (batch_of(i), 0, col(j))),
                  pl.BlockSpec((d, tf), lambda i, j: (0, jnp.minimum(j, nf - 1))),
                  pl.BlockSpec((ff, tn), lambda i, j: (0, col(j)))],
        out_specs=pl.BlockSpec((tm, tn), lambda i, j: (i, col(j))),
        out_shape=jax.ShapeDtypeStruct((m, d), F32),
        scratch_shapes=[pltpu.VMEM((tm, d), BF16), pltpu.VMEM((nf, tm, tf), BF16)],
        compiler_params=_params(("parallel", "arbitrary"), MLP_VMEM_LIMIT),
        name="mlp",
    )(x, scale, shift, x, gate, w1, w2)


def _final_norm_kernel(x_ref, w_ref, o_ref):
    x = x_ref[...]
    ms = jnp.mean(x * x, axis=-1, keepdims=True)
    o_ref[...] = x * lax.rsqrt(ms + NORM_EPS) * w_ref[...]


def _final_norm(x, w, tm):
    m, d = x.shape
    tm = min(tm, m)
    return pl.pallas_call(
        _final_norm_kernel,
        grid=(m // tm,),
        in_specs=[pl.BlockSpec((tm, d), lambda i: (i, 0)), pl.BlockSpec((1, d), lambda i: (0, 0))],
        out_specs=pl.BlockSpec((tm, d), lambda i: (i, 0)),
        out_shape=jax.ShapeDtypeStruct((m, d), F32),
        compiler_params=_params(("parallel",)),
        name="final_norm",
    )(x, w)


def _hgrn_consts(c):
    row = _iota((c, c), 0)
    col = _iota((c, c), 1)
    ltri = (_iota((c, 3 * c), 0) >= _iota((c, 3 * c), 1) % c).astype(BF16)
    levels = []
    gs = 2 * DIAG
    while gs <= c:
        half = gs // 2
        pair = (row // gs == col // gs) & (row % gs >= half) & (col % gs < half)
        levels.append((gs, pair))
        gs *= 2
    rix = _iota((c, 1), 0)
    return ltri, levels, rix


def _hgrn_chunk(q, k, v, g, st, consts):
    ltri, levels, rix = consts
    c = q.shape[0]
    gc = _mask_mm_left(ltri, g)
    yield
    lhs, rhs = [q * jnp.exp(gc)], [st]
    for gs, _ in levels:
        half = gs // 2
        ref = jnp.concatenate(
            [jnp.broadcast_to(gc[m0 * gs + half - 1:m0 * gs + half, :], (gs, gc.shape[1])) for m0 in range(c // gs)],
            axis=0)
        is_q = (rix % gs) >= half
        lhs.append(jnp.where(is_q, q * jnp.exp(jnp.minimum(gc - ref, 0.0)), 0.0))
        rhs.append(jnp.where(is_q, 0.0, k * jnp.exp(jnp.minimum(ref - gc, 0.0))))
        rhs.append(jnp.zeros((LANES - c, k.shape[1]), F32))
    prod = _nt(jnp.concatenate(lhs, axis=0), jnp.concatenate(rhs, axis=0))
    gl = gc[c - 1:c, :]
    upd = _tn(v, k * jnp.exp(gl - gc))
    yield
    att = jnp.zeros((c, c), F32)
    for li, (_, pair) in enumerate(levels):
        att = att + jnp.where(pair, prod[(li + 1) * c:(li + 2) * c, (li + 1) * LANES:(li + 1) * LANES + c], 0.0)
    o = prod[:c, :LANES] + _nn(att, v)
    for dist in range(DIAG):
        if dist == 0:
            kd, gd, vd = k, gc, v
        else:
            kd = pltpu.roll(k, dist, 0)
            gd = pltpu.roll(gc, dist, 0)
            vd = pltpu.roll(v, dist, 0)
        w = jnp.sum(q * kd * jnp.exp(jnp.minimum(gc - gd, 0.0)), axis=-1, keepdims=True)
        o = o + jnp.where((rix % DIAG) >= dist, w, 0.0) * vd
    yield
    return o, st * jnp.exp(gl) + upd


def _hgrn_kernel(q_ref, f_ref, i_ref, g_ref, lb_ref, nw_ref, s0_ref, o_ref, so_ref, s_ref, *, dk):
    l = pl.program_id(2)

    @pl.when(l == 0)
    def _():
        s_ref[...] = s0_ref[...]

    t, width = q_ref.shape
    c = min(CHUNK, t)
    consts = _hgrn_consts(c)
    lb = lb_ref[...]
    nw = nw_ref[...]

    def body(ci, carry):
        sl = pl.ds(pl.multiple_of(ci * c, c), c)
        f = lb + (1.0 - lb) * _sigmoid(f_ref[sl, :])
        q = _silu(q_ref[sl, :]) * dk ** -0.5
        k = 1.0 - f
        v = i_ref[sl, :]
        g = jnp.log(f)
        gate = _silu(g_ref[sl, :])
        heads = [slice(h * dk, (h + 1) * dk) for h in range(width // dk)]
        results = _lockstep(_hgrn_chunk(q[:, hs], k[:, hs], v[:, hs], g[:, hs], s_ref[h], consts)
                            for h, hs in enumerate(heads))
        for h, (o, st) in enumerate(results):
            s_ref[h] = st
            ms = jnp.mean(o * o, axis=-1, keepdims=True)
            o_ref[sl, heads[h]] = (o * lax.rsqrt(ms + NORM_EPS) * nw * gate[:, heads[h]]).astype(o_ref.dtype)
        return carry

    lax.fori_loop(0, t // c, body, 0)

    @pl.when(l == pl.num_programs(2) - 1)
    def _():
        so_ref[...] = s_ref[...]


def _hgrn(p, lb, norm_w, s0t, batch, seq, heads, tile):
    dk = LANES
    tile = min(tile, seq)
    nl = seq // tile
    hb = HGRN_HEADS_PER_STEP
    assert heads % hb == 0
    groups = heads // hb
    width = hb * dk
    blk = lambda off: pl.BlockSpec((tile, width), lambda b, h, l, off=off: (b * nl + l, off * groups + h))
    st_spec = pl.BlockSpec((None, hb, dk, dk), lambda b, h, l: (b, h, 0, 0))
    return pl.pallas_call(
        functools.partial(_hgrn_kernel, dk=dk),
        grid=(batch, groups, nl),
        in_specs=[blk(0), blk(1), blk(2), blk(3),
                  pl.BlockSpec((1, width), lambda b, h, l: (0, h)),
                  pl.BlockSpec((1, dk), lambda b, h, l: (0, 0)),
                  st_spec],
        out_specs=[pl.BlockSpec((tile, width), lambda b, h, l: (b * nl + l, h)), st_spec],
        out_shape=[jax.ShapeDtypeStruct((batch * seq, heads * dk), BF16),
                   jax.ShapeDtypeStruct(s0t.shape, F32)],
        scratch_shapes=[pltpu.VMEM((hb, dk, dk), F32)],
        compiler_params=_params(("parallel", "parallel", "arbitrary")),
        name="hgrn2",
    )(p, p, p, p, lb, norm_w, s0t)


def _rwkv_consts(c):
    n = 4 * c
    row = _iota((n, n), 0)
    col = _iota((n, n), 1)
    same_head = ((row // c) % 2) == ((col // c) % 2)
    strict = (row % c) > (col % c)
    incl = (row % c) >= (col % c)
    keep = same_head & (strict | ((row >= 2 * c) & incl))
    ltri = (_iota((c, 3 * c), 0) >= _iota((c, 3 * c), 1) % c).astype(BF16)
    head0 = _iota((1, LANES), 1) < (LANES // 2)
    rs = _iota((LANES, LANES), 0)
    cs = _iota((LANES, LANES), 1)
    bdiag = (rs // (LANES // 2)) == (cs // (LANES // 2))
    return keep, ltri, head0, bdiag


def _rwkv_chunk(r, lg, k, v, a, b, s, consts):
    keep, ltri, head0, bdiag = consts
    c = r.shape[0]
    gc = _mask_mm_left(ltri, lg)
    yield
    e_incl = jnp.exp(gc)
    e_inv = jnp.exp(-gc)
    at = a * jnp.exp(gc - lg)
    rt = r * e_incl
    bt = b * e_inv
    kt = k * e_inv
    split = lambda x: [jnp.where(head0, x, 0.0), jnp.where(head0, 0.0, x)]
    lhs = jnp.concatenate(split(at) + split(rt), axis=0)
    rhs = jnp.concatenate([bt, bt, kt, kt, s], axis=0)
    pls = _nt(lhs, rhs)
    yield
    p = jnp.where(keep, pls[:, :4 * c], 0.0)
    ls = pls[:, 4 * c:]
    vst = jnp.concatenate(split(v), axis=0)
    pv = _nn(p[:, 2 * c:], vst)
    yield
    ust = yield from _nilpotent_solve(p[:2 * c, :2 * c], ls[:2 * c] + pv[:2 * c], int(math.log2(c)), hi_lo=False)
    ost = ls[2 * c:] + pv[2 * c:] + _nn(p[2 * c:, :2 * c], ust)
    yield
    o = ost[:c] + ost[c:]
    u = ust[:c] + ust[c:]
    gl = gc[c - 1:c, :]
    e_tail = jnp.exp(gl - gc)
    upd = _tn(jnp.concatenate([u, v], axis=0), jnp.concatenate([b * e_tail, k * e_tail], axis=0))
    yield
    return o, s * jnp.exp(gl) + jnp.where(bdiag, upd, 0.0)


def _rwkv_kernel(xr_ref, xk_ref, xv_ref, xw_ref, xa_ref, xg_ref,
                 hr_ref, hk_ref, hv_ref, hw_ref, ha_ref, hg_ref,
                 vec_ref, mus_ref, w2_ref, a2_ref, g2_ref, s0_ref,
                 o_ref, so_ref,
                 s_ref, buf_ref, bufs_ref, bufg_ref, r_ref, lg_ref, k_ref, v_ref, a_ref, b_ref, y_ref):
    l = pl.program_id(2)
    t, width = xr_ref.shape
    c = min(CHUNK, t)

    @pl.when(l == 0)
    def _():
        s_ref[...] = s0_ref[...]
        for i, h_ref in enumerate((hr_ref, hk_ref, hv_ref)):
            buf_ref[i, 0:1, :] = h_ref[...]
        for i, h_ref in enumerate((hw_ref, ha_ref)):
            bufs_ref[i, 0:1, :] = h_ref[...]
        bufg_ref[0:1, :] = hg_ref[...]

    first_row = _iota((t, 1), 0) == 0

    def shifted(carry, x_ref, mu):
        x = x_ref[...]
        prev = jnp.where(first_row, carry[0:1, :], pltpu.roll(x, 1, 0))
        carry[0:1, :] = x[t - 1:t, :]
        return x + (prev - x) * mu

    vec = vec_ref[...]
    mus = mus_ref[...]
    r = shifted(buf_ref.at[0], xr_ref, vec[0:1])
    k = shifted(buf_ref.at[1], xk_ref, vec[1:2])
    v = shifted(buf_ref.at[2], xv_ref, vec[2:3])
    xw = shifted(bufs_ref.at[0], xw_ref, mus[0:1, :LANES])
    xa = shifted(bufs_ref.at[1], xa_ref, mus[1:2, :LANES])
    xg = shifted(bufg_ref, xg_ref, mus[2:3])

    w0, a0, k_k, k_a, r_k, ln_w, ln_b = (vec[i:i + 1] for i in range(3, 10))
    half = LANES // 2
    ones2 = ((_iota((2 * LANES, LANES), 0) % LANES) // half == _iota((2 * LANES, LANES), 1) // half).astype(BF16)

    def head_sums(x):
        outs = []
        for blk in range(width // LANES):
            outs.append(jnp.dot(jnp.concatenate(_split2(x[:, blk * LANES:(blk + 1) * LANES]), axis=1), ones2,
                                preferred_element_type=F32))
        return jnp.concatenate(outs, axis=1)

    w_log = -_softplus(-(w0 + _nn(jnp.tanh(xw), w2_ref[...]))) - 0.5
    a_lr = _sigmoid(a0 + _nn(xa, a2_ref[...]))
    gate = _nn(_sigmoid(xg), g2_ref[...])
    kk = k * k_k
    kk = kk * lax.rsqrt(head_sums(kk * kk) + 1e-12)
    k = k * (1.0 + (a_lr - 1.0) * k_a)
    r_ref[...] = r
    lg_ref[...] = -jnp.exp(w_log)
    k_ref[...] = k
    v_ref[...] = v
    a_ref[...] = -kk
    b_ref[...] = kk * a_lr

    consts = _rwkv_consts(c)

    def body(ci, carry):
        sl = pl.ds(pl.multiple_of(ci * c, c), c)
        operands = []
        for pi in range(width // LANES):
            cols = slice(pi * LANES, (pi + 1) * LANES)
            operands.append((r_ref[sl, cols], lg_ref[sl, cols], k_ref[sl, cols], v_ref[sl, cols],
                             a_ref[sl, cols], b_ref[sl, cols], s_ref[pi]))
        results = _lockstep(_rwkv_chunk(*ops, consts) for ops in operands)
        for pi, (o, s_new) in enumerate(results):
            s_ref[pi] = s_new
            y_ref[sl, pi * LANES:(pi + 1) * LANES] = o
        return carry

    lax.fori_loop(0, t // c, body, 0)

    o = y_ref[...]
    inv_n = 1.0 / half
    mean = head_sums(o) * inv_n
    dev = o - mean
    var = head_sums(dev * dev) * inv_n
    o = dev * lax.rsqrt(var + RWKV_LN_EPS) * ln_w + ln_b
    o = o + head_sums(r * k * r_k) * v
    o_ref[...] = (o * gate).astype(o_ref.dtype)

    @pl.when(l == pl.num_programs(2) - 1)
    def _():
        so_ref[...] = s_ref[...]


def _rwkv(p, shift, vecs, mus, w2, a2, g2, s0, batch, seq, pairs, col0, tile):
    tile = min(tile, seq)
    nl = seq // tile
    wb = pairs * LANES
    pb = RWKV_PAIRS_PER_STEP
    width = pb * LANES
    groups = pairs // pb
    assert pairs % pb == 0 and col0 % width == 0
    cb0 = col0 // LANES
    gb0 = col0 // width
    xg_off = -(-(3 * wb + 2 * LANES) // 512) * 512
    assert (col0 + xg_off) % 512 == 0
    pblk = lambda off: pl.BlockSpec((tile, width), lambda b, h, l, off=off: (b * nl + l, gb0 + off * groups + h))
    pfix = lambda cb: pl.BlockSpec((tile, LANES), lambda b, h, l, cb=cb: (b * nl + l, cb))
    hblk = lambda off: pl.BlockSpec((None, 1, width), lambda b, h, l, off=off: (b, 0, off * groups + h))
    hfix = lambda cb: pl.BlockSpec((None, 1, LANES), lambda b, h, l, cb=cb: (b, 0, cb))
    st_spec = pl.BlockSpec((None, pb, LANES, LANES), lambda b, h, l: (b, h, 0, 0))
    col = lambda rows: pl.BlockSpec((rows, width), lambda b, h, l: (0, h))
    full = lambda a: pl.BlockSpec(a.shape, lambda b, h, l: (0,) * a.ndim)
    tbuf = pltpu.VMEM((tile, width), F32)
    return pl.pallas_call(
        _rwkv_kernel,
        grid=(batch, groups, nl),
        in_specs=[pblk(0), pblk(1), pblk(2), pfix(cb0 + 3 * pairs), pfix(cb0 + 3 * pairs + 1),
                  pl.BlockSpec((tile, 512), lambda b, h, l: (b * nl + l, (col0 + xg_off) // 512)),
                  hblk(0), hblk(1), hblk(2), hfix(3 * pairs), hfix(3 * pairs + 1),
                  pl.BlockSpec((None, 1, 512), lambda b, h, l: (b, 0, xg_off // 512)),
                  col(vecs.shape[0]), full(mus), col(w2.shape[0]), col(a2.shape[0]), col(g2.shape[0]), st_spec],
        out_specs=[pl.BlockSpec((tile, width), lambda b, h, l: (b * nl + l, h)), st_spec],
        out_shape=[jax.ShapeDtypeStruct((batch * seq, wb), BF16), jax.ShapeDtypeStruct(s0.shape, F32)],
        scratch_shapes=[pltpu.VMEM((pb, LANES, LANES), F32),
                        pltpu.VMEM((3, 8, width), F32),
                        pltpu.VMEM((2, 8, LANES), F32),
                        pltpu.VMEM((8, 512), F32),
                        tbuf, tbuf, tbuf, tbuf, tbuf, tbuf, tbuf],
        compiler_params=_params(("parallel", "parallel", "arbitrary")),
        name="rwkv7",
    )(p, p, p, p, p, p, shift, shift, shift, shift, shift, shift, vecs, mus, w2, a2, g2, s0)


def _gdn_consts(c):
    n = 2 * c
    row = _iota((n, n), 0)
    col = _iota((n, n), 1)
    same = (row // c) == (col // c)
    incl = same & ((row % c) >= (col % c))
    strict = same & ((row % c) > (col % c))
    r3 = _iota((n, 3 * n), 0)
    c3 = _iota((n, 3 * n), 1) % n
    ltri3 = (((r3 // c) == (c3 // c)) & ((r3 % c) >= (c3 % c))).astype(BF16)
    return ltri3, incl, strict, strict.astype(F32)


def _gdn_pair_chunk(q, k, v, g, beta, states, consts):
    ltri3, incl, strict, strict_f = consts
    n = q.shape[0]
    c = n // 2
    cum = _mask_mm_left(ltri3, jnp.concatenate([g * strict_f, jnp.broadcast_to(g, (n, LANES))], axis=1))
    yield
    gc = cum[:, n:]
    dec = jnp.exp(jnp.where(incl, cum[:, :n], -1e30))
    kq = _nt(jnp.concatenate([k, q], axis=0), k)
    yield
    a_mat = jnp.where(strict, beta * kq[:n] * dec, 0.0)
    eg = jnp.exp(gc)
    uw = yield from _nilpotent_solve(-a_mat, jnp.concatenate([beta * v, beta * eg * k], axis=1), int(math.log2(c)))
    u, w = uw[:, :LANES], uw[:, LANES:]
    head0 = _iota((n, 1), 0) < c
    head0_2 = (_iota((2 * n, 1), 0) % n) < c
    by_head = lambda x, m: jnp.concatenate([jnp.where(m, x, 0.0), jnp.where(m, 0.0, x)], axis=1)
    wq = jnp.concatenate([w, q * eg], axis=0)
    ws = _nn(by_head(wq, head0_2), jnp.concatenate(states, axis=0))
    yield
    v_new = u - ws[:n]
    o = ws[n:] + _nn(kq[n:] * dec, v_new)
    yield
    gl = jnp.where(head0, gc[c - 1:c, :], gc[n - 1:n, :])
    upd = _tn(k * jnp.exp(gl - gc), by_head(v_new, head0))
    yield
    new_states = [s * jnp.exp(gc[(hi + 1) * c - 1:(hi + 1) * c, :]) + upd[:, hi * LANES:(hi + 1) * LANES]
                  for hi, s in enumerate(states)]
    return o, new_states


def _gdn_kernel(q_ref, k_ref, v_ref, z_ref, ba_ref, cwq_ref, cwk_ref, cwv_ref, cq_ref, ck_ref, cv_ref,
                par_ref, nw_ref, s0_ref, o_ref, so_ref, s_ref, buf_ref, qs_ref, ks_ref, vs_ref,
                *, heads, conv_w):
    hg = pl.program_id(1)
    l = pl.program_id(2)
    t, width = q_ref.shape
    hb = width // LANES
    c = min(CHUNK, t)
    pad = 8
    hist = conv_w - 1
    streams = ((q_ref, cq_ref, cwq_ref, qs_ref), (k_ref, ck_ref, cwk_ref, ks_ref), (v_ref, cv_ref, cwv_ref, vs_ref))

    @pl.when(l == 0)
    def _():
        s_ref[...] = s0_ref[...]
        for i, (_, c_ref, _, _) in enumerate(streams):
            buf_ref[i, pad - hist:pad, :] = c_ref[...]

    for i, (x_ref, _, w_ref, dst_ref) in enumerate(streams):
        x = x_ref[...]
        w = w_ref[...]
        y = x * w[hist:hist + 1]
        for shift in range(1, conv_w):
            y = y + pltpu.roll(x, shift, 0) * w[hist - shift:hist - shift + 1]
        dst_ref[...] = _silu(y)
        buf_ref[i, pad:2 * pad, :] = x[0:pad]
        head = jnp.zeros((pad, width), F32)
        for j in range(conv_w):
            head = head + buf_ref[i, pad - hist + j:2 * pad - hist + j, :] * w[j:j + 1]
        dst_ref[0:pad, :] = _silu(head)
        buf_ref[i, pad - hist:pad, :] = x[t - hist:t]

    par = par_ref[...]
    lane = _iota((1, LANES), 1)
    consts = _gdn_consts(c)
    nw = nw_ref[...]

    def body(ci, carry):
        sl = pl.ds(pl.multiple_of(ci * c, c), c)
        bac = ba_ref[sl, :]
        bac = jnp.where(lane < heads, _sigmoid(bac), -jnp.exp(par[0:1]) * _softplus(bac + par[1:2]))
        column = lambda idx: jnp.sum(jnp.where(lane == idx, bac, 0.0), axis=-1, keepdims=True)
        operands = []
        for pi in range(hb // 2):
            local = (2 * pi, 2 * pi + 1)
            cols = [slice(h * LANES, (h + 1) * LANES) for h in local]
            stack = lambda ref: jnp.concatenate([ref[sl, cs] for cs in cols], axis=0)
            beta = jnp.concatenate([column(hg * hb + h) for h in local], axis=0)
            g = jnp.concatenate([column(heads + hg * hb + h) for h in local], axis=0)
            operands.append((stack(qs_ref), stack(ks_ref), stack(vs_ref), g, beta, [s_ref[h] for h in local],
                             [z_ref[sl, cs] for cs in cols]))
        chains = []
        for q, k, v, g, beta, states, _ in operands:
            q = q * lax.rsqrt(jnp.sum(q * q, axis=-1, keepdims=True) + 1e-12) * LANES ** -0.5
            k = k * lax.rsqrt(jnp.sum(k * k, axis=-1, keepdims=True) + 1e-12)
            chains.append(_gdn_pair_chunk(q, k, v, g, beta, states, consts))
        results = _lockstep(chains)
        for pi, (o, new_states) in enumerate(results):
            z = operands[pi][6]
            for i in range(2):
                h = 2 * pi + i
                s_ref[h] = new_states[i]
                oh = o[i * c:(i + 1) * c]
                ms = jnp.mean(oh * oh, axis=-1, keepdims=True)
                o_ref[sl, h * LANES:(h + 1) * LANES] = (
                    oh * lax.rsqrt(ms + NORM_EPS) * nw * _silu(z[i])).astype(o_ref.dtype)
        return carry

    lax.fori_loop(0, t // c, body, 0)

    @pl.when(l == pl.num_programs(2) - 1)
    def _():
        so_ref[...] = s_ref[...]


def _gdn(p, conv_w, conv_state, par, norm_w, s0, batch, seq, heads, tile):
    tile = min(tile, seq)
    nl = seq // tile
    cw = conv_w.shape[0]
    hb = GDN_HEADS_PER_STEP
    assert heads % hb == 0 and hb % 2 == 0
    groups = heads // hb
    width = hb * LANES
    pblk = lambda off: pl.BlockSpec((tile, width), lambda b, h, l, off=off: (b * nl + l, off * groups + h))
    wblk = lambda off: pl.BlockSpec((cw, width), lambda b, h, l, off=off: (0, off * groups + h))
    cblk = lambda off: pl.BlockSpec((None, cw - 1, width), lambda b, h, l, off=off: (b, 0, off * groups + h))
    st_spec = pl.BlockSpec((None, hb, LANES, LANES), lambda b, h, l: (b, h, 0, 0))
    tbuf = pltpu.VMEM((tile, width), F32)
    return pl.pallas_call(
        functools.partial(_gdn_kernel, heads=heads, conv_w=cw),
        grid=(batch, groups, nl),
        in_specs=[pblk(0), pblk(1), pblk(2), pblk(3),
                  pl.BlockSpec((tile, LANES), lambda b, h, l: (b * nl + l, 4 * heads)),
                  wblk(0), wblk(1), wblk(2), cblk(0), cblk(1), cblk(2),
                  pl.BlockSpec(par.shape, lambda b, h, l: (0, 0)),
                  pl.BlockSpec((1, LANES), lambda b, h, l: (0, 0)),
                  st_spec],
        out_specs=[pl.BlockSpec((tile, width), lambda b, h, l: (b * nl + l, h)), st_spec],
        out_shape=[jax.ShapeDtypeStruct((batch * seq, heads * LANES), BF16), jax.ShapeDtypeStruct(s0.shape, F32)],
        scratch_shapes=[pltpu.VMEM((hb, LANES, LANES), F32),
                        pltpu.VMEM((3, 16, width), F32),
                        tbuf, tbuf, tbuf],
        compiler_params=_params(("parallel", "parallel", "arbitrary")),
        name="gdn",
    )(p, p, p, p, p, conv_w, conv_w, conv_w, conv_state, conv_state, conv_state, par, norm_w, s0)


def _pad_cols(w, total):
    return jnp.pad(w, ((0, 0),) * (w.ndim - 1) + ((0, total - w.shape[-1]),))


def _ab_layout(a_cols, wb, lora_w, lora_a, lora_g):
    assert lora_w == LANES and lora_a == LANES and lora_g <= 512
    xg_off = -(-(3 * wb + 2 * LANES) // 512) * 512
    b_width = xg_off + 512
    assert a_cols % 512 == 0
    return xg_off, b_width


def _rwkv_cols(t, wb, xg_off, b_width, lora_g):
    head = t[..., :3 * wb + 2 * LANES]
    tail = t[..., 3 * wb + 2 * LANES:]
    z = lambda n: jnp.zeros(t.shape[:-1] + (n,), t.dtype)
    return jnp.concatenate([head, z(xg_off - head.shape[-1]), tail, z(b_width - xg_off - lora_g)], axis=-1)


def _trunk(x, mod, s_hgrn, s_rwkv, s_shift, s_gdn, s_conv, wts, tiles):
    batch, seq, d = x.shape
    m = batch * seq
    x2 = x.reshape(m, d)
    tm, t_hgrn, t_rwkv, t_gdn = tiles
    outs = {}
    depth = mod.shape[0]
    for layer in range(depth):
        j = layer // 2
        vecs6 = [mod[layer, :, i * d:(i + 1) * d].reshape(batch, 1, d) for i in range(6)]
        sh1, sc1, g1, sh2, sc2, g2 = vecs6
        if layer % 2 == 0:
            w = wts["ab"][j]
            p = _norm_matmul(x2, sc1, sh1, w["w_in"], seq, tm, 512)
            ha, wb = w["ha"], w["wb"]
            oa, st_a = _hgrn(p, w["lb"], w["hgrn_norm_w"], jnp.swapaxes(s_hgrn[j], -1, -2), batch, seq, ha, t_hgrn)
            pairs = wb // LANES
            hb = s_rwkv.shape[2]
            nb = s_rwkv.shape[-1]
            sp = s_rwkv[j].reshape(batch, pairs, 2, nb, nb)
            zero = jnp.zeros_like(sp[:, :, 0])
            s0 = jnp.concatenate([jnp.concatenate([sp[:, :, 0], zero], -1),
                                  jnp.concatenate([zero, sp[:, :, 1]], -1)], -2)
            shift = _rwkv_cols(s_shift[j], wb, w["xg_off"], w["b_width"], w["lora_g"])
            ob, st_b = _rwkv(p, shift, w["vecs"], w["mus"], w["w2"], w["a2"], w["g2"], s0,
                             batch, seq, pairs, w["a_cols"], t_rwkv)
            mix = jnp.concatenate([oa, ob], axis=-1)
            outs.setdefault("hgrn", []).append(jnp.swapaxes(st_a, -1, -2))
            outs.setdefault("rwkv", []).append(
                jnp.stack([st_b[:, :, :nb, :nb], st_b[:, :, nb:, nb:]], axis=2).reshape(batch, hb, nb, nb))
            last = p.reshape(batch, seq, -1)[:, seq - 1:, w["a_cols"]:]
            n_head = 3 * wb + 2 * LANES
            outs.setdefault("shift", []).append(
                jnp.concatenate([last[..., :n_head], last[..., w["xg_off"]:w["xg_off"] + w["lora_g"]]], axis=-1))
            w_out = w["w_out"]
        else:
            w = wts["gdn"][j]
            p = _norm_matmul(x2, sc1, sh1, w["w_in"], seq, tm, 512)
            hc = w["hc"]
            mix, st_c = _gdn(p, w["conv_w"], s_conv[j], w["par"], w["norm_w"], s_gdn[j], batch, seq, hc, t_gdn)
            outs.setdefault("gdn", []).append(st_c)
            cw = w["conv_w"].shape[0]
            raw = p.reshape(batch, seq, -1)[:, :, :3 * hc * LANES]
            prev = jnp.concatenate([s_conv[j], raw[:, max(seq - (cw - 1), 0):]], axis=1)
            outs.setdefault("conv", []).append(prev[:, -(cw - 1):])
            w_out = w["w_out"]
        x2 = _out_residual(mix, w_out, x2, g1, seq, tm, 512)
        x2 = _mlp(x2, sc2, sh2, g2, wts["mlp_w1"][layer], wts["mlp_w2"][layer], seq, min(tm, 512), 512, 256)
    x2 = _final_norm(x2, wts["final_w"], 256)
    st = lambda name: jnp.stack(outs[name])
    return x2.reshape(batch, seq, d), st("hgrn"), st("rwkv"), st("shift"), st("gdn"), st("conv")


def kernel(x_prompt, x_sample, c_prompt, c_sample, state_hgrn, state_rwkv, state_rwkv_shift, state_gdn,
           state_gdn_conv, ada_w, ada_b, mlp_w1, mlp_w2, final_norm_w, ab_w_in, ab_w_out, hgrn_lb_logits,
           hgrn_norm_w, rwkv_mu, rwkv_w0, rwkv_w2, rwkv_a0, rwkv_a2, rwkv_g2, rwkv_k_k, rwkv_k_a, rwkv_r_k,
           rwkv_ln_w, rwkv_ln_b, gdn_w_in, gdn_w_out, gdn_conv_w, gdn_a_log, gdn_dt_bias, gdn_norm_w):
    d = x_prompt.shape[-1]
    nb_p, nb_s = x_prompt.shape[0], x_sample.shape[0]
    n_ab, n_c = ab_w_in.shape[0], gdn_w_in.shape[0]
    ha = state_hgrn.shape[2]
    a_cols = 4 * ha * LANES
    wb = rwkv_w0.shape[1]
    lora_w, lora_a, lora_g = rwkv_w2.shape[1], rwkv_a2.shape[1], rwkv_g2.shape[1]
    xg_off, b_width = _ab_layout(a_cols, wb, lora_w, lora_a, lora_g)
    hc = state_gdn.shape[2]

    rows = nb_p + nb_s
    rows_pad = -(-rows // 8) * 8
    c_all = jnp.pad(jnp.concatenate([c_prompt, c_sample], axis=0), ((0, rows_pad - rows), (0, 0)))
    mod = _adaln(c_all, ada_w, ada_b)

    lbs = jnp.cumsum(jax.nn.softmax(hgrn_lb_logits.astype(F32), axis=0), axis=0)
    wts = {"ab": [], "gdn": [], "final_w": final_norm_w.reshape(1, d),
           "mlp_w1": [mlp_w1[i].astype(BF16) for i in range(mlp_w1.shape[0])],
           "mlp_w2": [mlp_w2[i].astype(BF16) for i in range(mlp_w2.shape[0])]}
    for j in range(n_ab):
        w_in = ab_w_in[j]
        w_in = jnp.concatenate([w_in[:, :a_cols], _rwkv_cols(w_in[:, a_cols:], wb, xg_off, b_width, lora_g)], axis=-1)
        mu = _rwkv_cols(rwkv_mu[j][None], wb, xg_off, b_width, lora_g)[0]
        vec_rows = [mu[:wb], mu[wb:2 * wb], mu[2 * wb:3 * wb], rwkv_w0[j], rwkv_a0[j], rwkv_k_k[j], rwkv_k_a[j],
                    rwkv_r_k[j].reshape(-1), rwkv_ln_w[j], rwkv_ln_b[j]]
        vecs = jnp.pad(jnp.stack(vec_rows), ((0, 16 - len(vec_rows)), (0, 0)))
        mus = jnp.stack([_pad_cols(mu[3 * wb:3 * wb + LANES], 512), _pad_cols(mu[3 * wb + LANES:3 * wb + 2 * LANES], 512),
                         mu[xg_off:xg_off + 512]])
        wts["ab"].append({
            "w_in": w_in.astype(BF16), "w_out": ab_w_out[j].astype(BF16), "lb": lbs[j][None],
            "hgrn_norm_w": hgrn_norm_w[j][None], "vecs": vecs, "mus": jnp.pad(mus, ((0, 5), (0, 0))),
            "w2": rwkv_w2[j].astype(BF16), "a2": rwkv_a2[j].astype(BF16),
            "g2": jnp.pad(rwkv_g2[j], ((0, 512 - lora_g), (0, 0))).astype(BF16),
            "ha": ha, "wb": wb, "a_cols": a_cols, "xg_off": xg_off, "b_width": b_width, "lora_g": lora_g})
    for j in range(n_c):
        cols = gdn_w_in.shape[-1]
        cols_pad = -(-cols // 512) * 512
        par = jnp.zeros((8, LANES), F32)
        par = par.at[0, hc:2 * hc].set(gdn_a_log[j]).at[1, hc:2 * hc].set(gdn_dt_bias[j])
        wts["gdn"].append({
            "w_in": _pad_cols(gdn_w_in[j], cols_pad).astype(BF16), "w_out": gdn_w_out[j].astype(BF16),
            "conv_w": gdn_conv_w[j], "par": par, "norm_w": gdn_norm_w[j][None], "hc": hc})

    zeros = lambda s: jnp.zeros((s.shape[0], nb_p) + s.shape[2:], x_prompt.dtype)
    y_p, hg_p, rw_p, sh_p, gd_p, cv_p = _trunk(
        x_prompt, mod[:, :nb_p], zeros(state_hgrn), zeros(state_rwkv), zeros(state_rwkv_shift),
        zeros(state_gdn), zeros(state_gdn_conv), wts, PROMPT_TILES)
    y_s, hg_s, rw_s, sh_s, gd_s, cv_s = _trunk(
        x_sample, mod[:, nb_p:rows], state_hgrn, state_rwkv, state_rwkv_shift, state_gdn, state_gdn_conv,
        wts, SAMPLE_TILES)
    return (y_p, y_s, hg_p, rw_p, sh_p, gd_p, cv_p, hg_s, rw_s, sh_s, gd_s, cv_s)
```

```python
import functools
import math

import jax
import jax.numpy as jnp
from jax import lax
from jax.experimental import pallas as pl
from jax.experimental.pallas import tpu as pltpu

F32 = jnp.float32
BF16 = jnp.bfloat16
HI = lax.Precision.HIGHEST

NORM_EPS = 1e-6
RWKV_LN_EPS = 64e-5
CHUNK = 64
DIAG = 8
LANES = 128
VMEM_LIMIT = 56 * 1024 * 1024
MLP_VMEM_LIMIT = 60 * 1024 * 1024
PROMPT_TILES = (1024, 512, 256, 256)
SAMPLE_TILES = (512, 64, 64, 64)
HGRN_HEADS_PER_STEP = 8
RWKV_PAIRS_PER_STEP = 8
GDN_HEADS_PER_STEP = 16


def _mm(a, b, ca, cb, exact=False):
    dims = (((ca,), (cb,)), ((), ()))
    if exact:
        return lax.dot_general(a, b, dims, precision=HI, preferred_element_type=F32)
    return lax.dot_general(a.astype(BF16), b.astype(BF16), dims, preferred_element_type=F32)


def _nn(a, b, exact=False):
    return _mm(a, b, 1, 0, exact)


def _nt(a, b, exact=False):
    return _mm(a, b, 1, 1, exact)


def _tn(a, b, exact=False):
    return _mm(a, b, 0, 0, exact)


def _sigmoid(x):
    return 1.0 / (1.0 + jnp.exp(-x))


def _silu(x):
    return x * _sigmoid(x)


def _softplus(x):
    return jnp.maximum(x, 0.0) + jnp.log(1.0 + jnp.exp(-jnp.abs(x)))


def _iota(shape, dim):
    return lax.broadcasted_iota(jnp.int32, shape, dim)


def _split2(x):
    hi = x.astype(BF16)
    return hi, (x - hi.astype(F32)).astype(BF16)


def _split3(x):
    hi = x.astype(BF16)
    r = x - hi.astype(F32)
    lo = r.astype(BF16)
    return hi, lo, (r - lo.astype(F32)).astype(BF16)


def _mm_hi(a, b, a_parts=None, b_parts=None):
    a_hi, a_lo = a_parts if a_parts is not None else _split2(a)
    b_hi, b_lo = b_parts if b_parts is not None else _split2(b)
    return jnp.dot(jnp.concatenate([a_hi, a_lo, a_hi], axis=1), jnp.concatenate([b_hi, b_hi, b_lo], axis=0),
                   preferred_element_type=F32)


def _mask_mm_left(mask3, x):
    return jnp.dot(mask3, jnp.concatenate(_split3(x), axis=0), preferred_element_type=F32)


def _mask_mm_right(x, mask3):
    return jnp.dot(jnp.concatenate(_split3(x), axis=1), mask3, preferred_element_type=F32)


def _nilpotent_solve(m, x, stages, hi_lo=True):
    n = m.shape[0]
    mm = _mm_hi if hi_lo else _nn
    for k in range(stages):
        last = k == stages - 1
        y = mm(m, x if last else jnp.concatenate([m, x], axis=1))
        yield
        if last:
            x = x + y
        else:
            m = y[:, :n]
            x = x + y[:, n:]
    return x


def _lockstep(generators):
    generators = list(generators)
    results = [None] * len(generators)
    live = list(range(len(generators)))
    while live:
        for i in list(live):
            try:
                next(generators[i])
            except StopIteration as stop:
                results[i] = stop.value
                live.remove(i)
    return results


def _params(sem, vmem=VMEM_LIMIT):
    return pltpu.CompilerParams(dimension_semantics=sem, vmem_limit_bytes=vmem)


def _rows_per_tile(rows_per_batch, tile):
    if tile <= rows_per_batch:
        assert rows_per_batch % tile == 0
        return 1
    assert tile % rows_per_batch == 0
    return tile // rows_per_batch


def _adaln_kernel(c_ref, w_ref, b_ref, o_ref):
    c = _silu(c_ref[...]).astype(BF16)
    o_ref[...] = jnp.dot(c, w_ref[...].astype(BF16), preferred_element_type=F32) + b_ref[...]


def _adaln(c, ada_w, ada_b):
    depth, d, n = ada_w.shape
    rows = c.shape[0]
    tn = 512 if n % 512 == 0 else n
    return pl.pallas_call(
        _adaln_kernel,
        grid=(depth, n // tn),
        in_specs=[pl.BlockSpec((rows, d), lambda l, j: (0, 0)),
                  pl.BlockSpec((None, d, tn), lambda l, j: (l, 0, j)),
                  pl.BlockSpec((None, 1, tn), lambda l, j: (l, 0, j))],
        out_specs=pl.BlockSpec((None, rows, tn), lambda l, j: (l, 0, j)),
        out_shape=jax.ShapeDtypeStruct((depth, rows, n), F32),
        compiler_params=_params(("parallel", "parallel")),
        name="adaln",
    )(c, ada_w, ada_b.reshape(depth, 1, n))


def _modulated_norm(x_ref, sc_ref, sh_ref, h_ref, groups):
    rows = x_ref.shape[0] // groups
    step = min(rows, 128)

    for gi in range(groups):
        sc = 1.0 + sc_ref[gi]
        sh = sh_ref[gi]

        def body(i, carry, gi=gi, sc=sc, sh=sh):
            sl = pl.ds(pl.multiple_of(gi * rows + i * step, step), step)
            x = x_ref[sl, :]
            ms = jnp.mean(x * x, axis=-1, keepdims=True)
            h_ref[sl, :] = (x * lax.rsqrt(ms + NORM_EPS) * sc + sh).astype(BF16)
            return carry

        lax.fori_loop(0, rows // step, body, 0)


def _norm_mm_kernel(x_ref, sc_ref, sh_ref, w_ref, o_ref, h_ref, *, groups):
    @pl.when(pl.program_id(1) == 0)
    def _():
        _modulated_norm(x_ref, sc_ref, sh_ref, h_ref, groups)

    o_ref[...] = jnp.dot(h_ref[...], w_ref[...], preferred_element_type=F32).astype(o_ref.dtype)


def _norm_matmul(x, scale, shift, w, rows_per_batch, tm, tn):
    m, d = x.shape
    n = w.shape[1]
    tm = min(tm, m)
    groups = _rows_per_tile(rows_per_batch, tm)
    bidx = (lambda i, j: (i * tm // rows_per_batch, 0, 0)) if groups == 1 else (lambda i, j: (i, 0, 0))
    return pl.pallas_call(
        functools.partial(_norm_mm_kernel, groups=groups),
        grid=(m // tm, n // tn),
        in_specs=[pl.BlockSpec((tm, d), lambda i, j: (i, 0), pipeline_mode=pl.Buffered(1)),
                  pl.BlockSpec((groups, 1, d), bidx),
                  pl.BlockSpec((groups, 1, d), bidx),
                  pl.BlockSpec((d, tn), lambda i, j: (0, j))],
        out_specs=pl.BlockSpec((tm, tn), lambda i, j: (i, j)),
        out_shape=jax.ShapeDtypeStruct((m, n), F32),
        scratch_shapes=[pltpu.VMEM((tm, d), BF16)],
        compiler_params=_params(("parallel", "arbitrary")),
        name="norm_matmul",
    )(x, scale, shift, w)


def _out_res_kernel(a_ref, w_ref, x_ref, g_ref, o_ref, *, groups):
    acc = jnp.dot(a_ref[...], w_ref[...], preferred_element_type=F32)
    rows = acc.shape[0] // groups
    for gi in range(groups):
        sl = slice(gi * rows, (gi + 1) * rows)
        o_ref[sl, :] = x_ref[sl, :] + g_ref[gi] * acc[sl, :]


def _out_residual(a, w, x, gate, rows_per_batch, tm, tn):
    m, k = a.shape
    n = w.shape[1]
    tm = min(tm, m)
    groups = _rows_per_tile(rows_per_batch, tm)
    bidx = (lambda i, j: (i * tm // rows_per_batch, 0, j)) if groups == 1 else (lambda i, j: (i, 0, j))
    return pl.pallas_call(
        functools.partial(_out_res_kernel, groups=groups),
        grid=(m // tm, n // tn),
        in_specs=[pl.BlockSpec((tm, k), lambda i, j: (i, 0)),
                  pl.BlockSpec((k, tn), lambda i, j: (0, j)),
                  pl.BlockSpec((tm, tn), lambda i, j: (i, j)),
                  pl.BlockSpec((groups, 1, tn), bidx)],
        out_specs=pl.BlockSpec((tm, tn), lambda i, j: (i, j)),
        out_shape=jax.ShapeDtypeStruct((m, n), F32),
        compiler_params=_params(("parallel", "parallel")),
        name="out_residual",
    )(a, w, x, gate)


def _mlp_kernel(x_ref, sc_ref, sh_ref, xr_ref, g_ref, w1_ref, w2_ref, o_ref, h_ref, u_ref, *, groups, nf):
    j = pl.program_id(1)

    @pl.when(j == 0)
    def _():
        _modulated_norm(x_ref, sc_ref, sh_ref, h_ref, groups)

    @pl.when(j < nf)
    def _():
        u = jnp.dot(h_ref[...], w1_ref[...], preferred_element_type=F32)
        u_ref[j] = jnp.square(jnp.maximum(u, 0.0)).astype(BF16)

    @pl.when(j >= nf)
    def _():
        tf = u_ref.shape[2]
        acc = jnp.dot(u_ref[0], w2_ref[0:tf, :], preferred_element_type=F32)
        for f in range(1, nf):
            acc += jnp.dot(u_ref[f], w2_ref[f * tf:(f + 1) * tf, :], preferred_element_type=F32)
        rows = acc.shape[0] // groups
        for gi in range(groups):
            sl = slice(gi * rows, (gi + 1) * rows)
            o_ref[sl, :] = xr_ref[sl, :] + g_ref[gi] * acc[sl, :]


def _mlp(x, scale, shift, gate, w1, w2, rows_per_batch, tm, tf, tn):
    m, d = x.shape
    ff = w1.shape[1]
    tm = min(tm, m)
    nf, nn = ff // tf, d // tn
    groups = _rows_per_tile(rows_per_batch, tm)
    batch_of = (lambda i: i * tm // rows_per_batch) if groups == 1 else (lambda i: i)
    vec = pl.BlockSpec((groups, 1, d), lambda i, j: (batch_of(i), 0, 0))
    col = lambda j: jnp.maximum(j - nf, 0)
    return pl.pallas_call(
        functools.partial(_mlp_kernel, groups=groups, nf=nf),
        grid=(m // tm, nf + nn),
        in_specs=[pl.BlockSpec((tm, d), lambda i, j: (i, 0), pipeline_mode=pl.Buffered(1)),
                  vec, vec,
                  pl.BlockSpec((tm, tn), lambda i, j: (i, col(j))),
                  pl.BlockSpec((groups, 1, tn), lambda i, j: (batch_of(i), 0, col(j))),
                  pl.BlockSpec((d, tf), lambda i, j: (0, jnp.minimum(j, nf - 1))),
                  pl.BlockSpec((ff, tn), lambda i, j: (0, col(j)))],
        out_specs=pl.BlockSpec((tm, tn), lambda i, j: (i, col(j))),
        out_shape=jax.ShapeDtypeStruct((m, d), F32),
        scratch_shapes=[pltpu.VMEM((tm, d), BF16), pltpu.VMEM((nf, tm, tf), BF16)],
        compiler_params=_params(("parallel", "arbitrary"), MLP_VMEM_LIMIT),
        name="mlp",
    )(x, scale, shift, x, gate, w1, w2)


def _final_norm_kernel(x_ref, w_ref, o_ref):
    x = x_ref[...]
    ms = jnp.mean(x * x, axis=-1, keepdims=True)
    o_ref[...] = x * lax.rsqrt(ms + NORM_EPS) * w_ref[...]


def _final_norm(x, w, tm):
    m, d = x.shape
    tm = min(tm, m)
    return pl.pallas_call(
        _final_norm_kernel,
        grid=(m // tm,),
        in_specs=[pl.BlockSpec((tm, d), lambda i: (i, 0)), pl.BlockSpec((1, d), lambda i: (0, 0))],
        out_specs=pl.BlockSpec((tm, d), lambda i: (i, 0)),
        out_shape=jax.ShapeDtypeStruct((m, d), F32),
        compiler_params=_params(("parallel",)),
        name="final_norm",
    )(x, w)


def _hgrn_consts(c):
    row = _iota((c, c), 0)
    col = _iota((c, c), 1)
    ltri = (_iota((c, 3 * c), 0) >= _iota((c, 3 * c), 1) % c).astype(BF16)
    levels = []
    gs = 2 * DIAG
    while gs <= c:
        half = gs // 2
        pair = (row // gs == col // gs) & (row % gs >= half) & (col % gs < half)
        levels.append((gs, pair))
        gs *= 2
    rix = _iota((c, 1), 0)
    return ltri, levels, rix


def _hgrn_chunk(q, k, v, g, st, consts):
    ltri, levels, rix = consts
    c = q.shape[0]
    gc = _mask_mm_left(ltri, g)
    yield
    lhs, rhs = [q * jnp.exp(gc)], [st]
    for gs, _ in levels:
        half = gs // 2
        ref = jnp.concatenate(
            [jnp.broadcast_to(gc[m0 * gs + half - 1:m0 * gs + half, :], (gs, gc.shape[1])) for m0 in range(c // gs)],
            axis=0)
        is_q = (rix % gs) >= half
        lhs.append(jnp.where(is_q, q * jnp.exp(gc - ref), 0.0))
        rhs.append(jnp.where(is_q, 0.0, k * jnp.exp(ref - gc)))
        rhs.append(jnp.zeros((LANES - c, k.shape[1]), F32))
    prod = _nt(jnp.concatenate(lhs, axis=0), jnp.concatenate(rhs, axis=0))
    gl = gc[c - 1:c, :]
    upd = _tn(v, k * jnp.exp(gl - gc))
    yield
    att = jnp.zeros((c, c), F32)
    for li, (_, pair) in enumerate(levels):
        att = att + jnp.where(pair, prod[(li + 1) * c:(li + 2) * c, (li + 1) * LANES:(li + 1) * LANES + c], 0.0)
    o = prod[:c, :LANES] + _nn(att, v)
    for dist in range(DIAG):
        if dist == 0:
            kd, gd, vd = k, gc, v
        else:
            kd = pltpu.roll(k, dist, 0)
            gd = pltpu.roll(gc, dist, 0)
            vd = pltpu.roll(v, dist, 0)
        w = jnp.sum(q * kd * jnp.exp(gc - gd), axis=-1, keepdims=True)
        o = o + jnp.where((rix % DIAG) >= dist, w, 0.0) * vd
    yield
    return o, st * jnp.exp(gl) + upd


def _hgrn_kernel(q_ref, f_ref, i_ref, g_ref, lb_ref, nw_ref, s0_ref, o_ref, so_ref, s_ref, *, dk):
    l = pl.program_id(2)

    @pl.when(l == 0)
    def _():
        s_ref[...] = s0_ref[...]

    t, width = q_ref.shape
    c = min(CHUNK, t)
    consts = _hgrn_consts(c)
    lb = lb_ref[...]
    nw = nw_ref[...]

    def body(ci, carry):
        sl = pl.ds(pl.multiple_of(ci * c, c), c)
        f = lb + (1.0 - lb) * _sigmoid(f_ref[sl, :])
        q = _silu(q_ref[sl, :]) * dk ** -0.5
        k = 1.0 - f
        v = i_ref[sl, :]
        g = jnp.log(f)
        gate = _silu(g_ref[sl, :])
        heads = [slice(h * dk, (h + 1) * dk) for h in range(width // dk)]
        results = _lockstep(_hgrn_chunk(q[:, hs], k[:, hs], v[:, hs], g[:, hs], s_ref[h], consts)
                            for h, hs in enumerate(heads))
        for h, (o, st) in enumerate(results):
            s_ref[h] = st
            ms = jnp.mean(o * o, axis=-1, keepdims=True)
            o_ref[sl, heads[h]] = (o * lax.rsqrt(ms + NORM_EPS) * nw * gate[:, heads[h]]).astype(o_ref.dtype)
        return carry

    lax.fori_loop(0, t // c, body, 0)

    @pl.when(l == pl.num_programs(2) - 1)
    def _():
        so_ref[...] = s_ref[...]


def _hgrn(p, lb, norm_w, s0t, batch, seq, heads, tile):
    dk = LANES
    tile = min(tile, seq)
    nl = seq // tile
    hb = HGRN_HEADS_PER_STEP
    assert heads % hb == 0
    groups = heads // hb
    width = hb * dk
    blk = lambda off: pl.BlockSpec((tile, width), lambda b, h, l, off=off: (b * nl + l, off * groups + h))
    st_spec = pl.BlockSpec((None, hb, dk, dk), lambda b, h, l: (b, h, 0, 0))
    return pl.pallas_call(
        functools.partial(_hgrn_kernel, dk=dk),
        grid=(batch, groups, nl),
        in_specs=[blk(0), blk(1), blk(2), blk(3),
                  pl.BlockSpec((1, width), lambda b, h, l: (0, h)),
                  pl.BlockSpec((1, dk), lambda b, h, l: (0, 0)),
                  st_spec],
        out_specs=[pl.BlockSpec((tile, width), lambda b, h, l: (b * nl + l, h)), st_spec],
        out_shape=[jax.ShapeDtypeStruct((batch * seq, heads * dk), BF16),
                   jax.ShapeDtypeStruct(s0t.shape, F32)],
        scratch_shapes=[pltpu.VMEM((hb, dk, dk), F32)],
        compiler_params=_params(("parallel", "parallel", "arbitrary")),
        name="hgrn2",
    )(p, p, p, p, lb, norm_w, s0t)


def _rwkv_consts(c):
    n = 4 * c
    row = _iota((n, n), 0)
    col = _iota((n, n), 1)
    same_head = ((row // c) % 2) == ((col // c) % 2)
    strict = (row % c) > (col % c)
    incl = (row % c) >= (col % c)
    keep = same_head & (strict | ((row >= 2 * c) & incl))
    ltri = (_iota((c, 3 * c), 0) >= _iota((c, 3 * c), 1) % c).astype(BF16)
    head0 = _iota((1, LANES), 1) < (LANES // 2)
    rs = _iota((LANES, LANES), 0)
    cs = _iota((LANES, LANES), 1)
    bdiag = (rs // (LANES // 2)) == (cs // (LANES // 2))
    return keep, ltri, head0, bdiag


def _rwkv_chunk(r, lg, k, v, a, b, s, consts):
    keep, ltri, head0, bdiag = consts
    c = r.shape[0]
    gc = _mask_mm_left(ltri, lg)
    yield
    e_incl = jnp.exp(gc)
    e_inv = jnp.exp(-gc)
    at = a * jnp.exp(gc - lg)
    rt = r * e_incl
    bt = b * e_inv
    kt = k * e_inv
    split = lambda x: [jnp.where(head0, x, 0.0), jnp.where(head0, 0.0, x)]
    lhs = jnp.concatenate(split(at) + split(rt), axis=0)
    rhs = jnp.concatenate([bt, bt, kt, kt, s], axis=0)
    pls = _nt(lhs, rhs)
    yield
    p = jnp.where(keep, pls[:, :4 * c], 0.0)
    ls = pls[:, 4 * c:]
    vst = jnp.concatenate(split(v), axis=0)
    pv = _nn(p[:, 2 * c:], vst)
    yield
    ust = yield from _nilpotent_solve(p[:2 * c, :2 * c], ls[:2 * c] + pv[:2 * c], int(math.log2(c)), hi_lo=False)
    ost = ls[2 * c:] + pv[2 * c:] + _nn(p[2 * c:, :2 * c], ust)
    yield
    o = ost[:c] + ost[c:]
    u = ust[:c] + ust[c:]
    gl = gc[c - 1:c, :]
    e_tail = jnp.exp(gl - gc)
    upd = _tn(jnp.concatenate([u, v], axis=0), jnp.concatenate([b * e_tail, k * e_tail], axis=0))
    yield
    return o, s * jnp.exp(gl) + jnp.where(bdiag, upd, 0.0)


def _rwkv_kernel(xr_ref, xk_ref, xv_ref, xw_ref, xa_ref, xg_ref,
                 hr_ref, hk_ref, hv_ref, hw_ref, ha_ref, hg_ref,
                 vec_ref, mus_ref, w2_ref, a2_ref, g2_ref, s0_ref,
                 o_ref, so_ref,
                 s_ref, buf_ref, bufs_ref, bufg_ref, r_ref, lg_ref, k_ref, v_ref, a_ref, b_ref, y_ref):
    l = pl.program_id(2)
    t, width = xr_ref.shape
    c = min(CHUNK, t)

    @pl.when(l == 0)
    def _():
        s_ref[...] = s0_ref[...]
        for i, h_ref in enumerate((hr_ref, hk_ref, hv_ref)):
            buf_ref[i, 0:1, :] = h_ref[...]
        for i, h_ref in enumerate((hw_ref, ha_ref)):
            bufs_ref[i, 0:1, :] = h_ref[...]
        bufg_ref[0:1, :] = hg_ref[...]

    first_row = _iota((t, 1), 0) == 0

    def shifted(carry, x_ref, mu):
        x = x_ref[...]
        prev = jnp.where(first_row, carry[0:1, :], pltpu.roll(x, 1, 0))
        carry[0:1, :] = x[t - 1:t, :]
        return x + (prev - x) * mu

    vec = vec_ref[...]
    mus = mus_ref[...]
    r = shifted(buf_ref.at[0], xr_ref, vec[0:1])
    k = shifted(buf_ref.at[1], xk_ref, vec[1:2])
    v = shifted(buf_ref.at[2], xv_ref, vec[2:3])
    xw = shifted(bufs_ref.at[0], xw_ref, mus[0:1, :LANES])
    xa = shifted(bufs_ref.at[1], xa_ref, mus[1:2, :LANES])
    xg = shifted(bufg_ref, xg_ref, mus[2:3])

    w0, a0, k_k, k_a, r_k, ln_w, ln_b = (vec[i:i + 1] for i in range(3, 10))
    half = LANES // 2
    ones2 = ((_iota((2 * LANES, LANES), 0) % LANES) // half == _iota((2 * LANES, LANES), 1) // half).astype(BF16)

    def head_sums(x):
        outs = []
        for blk in range(width // LANES):
            outs.append(jnp.dot(jnp.concatenate(_split2(x[:, blk * LANES:(blk + 1) * LANES]), axis=1), ones2,
                                preferred_element_type=F32))
        return jnp.concatenate(outs, axis=1)

    w_log = -_softplus(-(w0 + _nn(jnp.tanh(xw), w2_ref[...]))) - 0.5
    a_lr = _sigmoid(a0 + _nn(xa, a2_ref[...]))
    gate = _nn(_sigmoid(xg), g2_ref[...])
    kk = k * k_k
    kk = kk * lax.rsqrt(head_sums(kk * kk) + 1e-12)
    k = k * (1.0 + (a_lr - 1.0) * k_a)
    r_ref[...] = r
    lg_ref[...] = -jnp.exp(w_log)
    k_ref[...] = k
    v_ref[...] = v
    a_ref[...] = -kk
    b_ref[...] = kk * a_lr

    consts = _rwkv_consts(c)

    def body(ci, carry):
        sl = pl.ds(pl.multiple_of(ci * c, c), c)
        operands = []
        for pi in range(width // LANES):
            cols = slice(pi * LANES, (pi + 1) * LANES)
            operands.append((r_ref[sl, cols], lg_ref[sl, cols], k_ref[sl, cols], v_ref[sl, cols],
                             a_ref[sl, cols], b_ref[sl, cols], s_ref[pi]))
        results = _lockstep(_rwkv_chunk(*ops, consts) for ops in operands)
        for pi, (o, s_new) in enumerate(results):
            s_ref[pi] = s_new
            y_ref[sl, pi * LANES:(pi + 1) * LANES] = o
        return carry

    lax.fori_loop(0, t // c, body, 0)

    o = y_ref[...]
    inv_n = 1.0 / half
    mean = head_sums(o) * inv_n
    dev = o - mean
    var = head_sums(dev * dev) * inv_n
    o = dev * lax.rsqrt(var + RWKV_LN_EPS) * ln_w + ln_b
    o = o + head_sums(r * k * r_k) * v
    o_ref[...] = (o * gate).astype(o_ref.dtype)

    @pl.when(l == pl.num_programs(2) - 1)
    def _():
        so_ref[...] = s_ref[...]


def _rwkv(p, shift, vecs, mus, w2, a2, g2, s0, batch, seq, pairs, col0, tile):
    tile = min(tile, seq)
    nl = seq // tile
    wb = pairs * LANES
    pb = RWKV_PAIRS_PER_STEP
    width = pb * LANES
    groups = pairs // pb
    assert pairs % pb == 0 and col0 % width == 0
    cb0 = col0 // LANES
    gb0 = col0 // width
    xg_off = -(-(3 * wb + 2 * LANES) // 512) * 512
    assert (col0 + xg_off) % 512 == 0
    pblk = lambda off: pl.BlockSpec((tile, width), lambda b, h, l, off=off: (b * nl + l, gb0 + off * groups + h))
    pfix = lambda cb: pl.BlockSpec((tile, LANES), lambda b, h, l, cb=cb: (b * nl + l, cb))
    hblk = lambda off: pl.BlockSpec((None, 1, width), lambda b, h, l, off=off: (b, 0, off * groups + h))
    hfix = lambda cb: pl.BlockSpec((None, 1, LANES), lambda b, h, l, cb=cb: (b, 0, cb))
    st_spec = pl.BlockSpec((None, pb, LANES, LANES), lambda b, h, l: (b, h, 0, 0))
    col = lambda rows: pl.BlockSpec((rows, width), lambda b, h, l: (0, h))
    full = lambda a: pl.BlockSpec(a.shape, lambda b, h, l: (0,) * a.ndim)
    tbuf = pltpu.VMEM((tile, width), F32)
    return pl.pallas_call(
        _rwkv_kernel,
        grid=(batch, groups, nl),
        in_specs=[pblk(0), pblk(1), pblk(2), pfix(cb0 + 3 * pairs), pfix(cb0 + 3 * pairs + 1),
                  pl.BlockSpec((tile, 512), lambda b, h, l: (b * nl + l, (col0 + xg_off) // 512)),
                  hblk(0), hblk(1), hblk(2), hfix(3 * pairs), hfix(3 * pairs + 1),
                  pl.BlockSpec((None, 1, 512), lambda b, h, l: (b, 0, xg_off // 512)),
                  col(vecs.shape[0]), full(mus), col(w2.shape[0]), col(a2.shape[0]), col(g2.shape[0]), st_spec],
        out_specs=[pl.BlockSpec((tile, width), lambda b, h, l: (b * nl + l, h)), st_spec],
        out_shape=[jax.ShapeDtypeStruct((batch * seq, wb), BF16), jax.ShapeDtypeStruct(s0.shape, F32)],
        scratch_shapes=[pltpu.VMEM((pb, LANES, LANES), F32),
                        pltpu.VMEM((3, 8, width), F32),
                        pltpu.VMEM((2, 8, LANES), F32),
                        pltpu.VMEM((8, 512), F32),
                        tbuf, tbuf, tbuf, tbuf, tbuf, tbuf, tbuf],
        compiler_params=_params(("parallel", "parallel", "arbitrary")),
        name="rwkv7",
    )(p, p, p, p, p, p, shift, shift, shift, shift, shift, shift, vecs, mus, w2, a2, g2, s0)


def _gdn_consts(c):
    n = 2 * c
    row = _iota((n, n), 0)
    col = _iota((n, n), 1)
    same = (row // c) == (col // c)
    incl = same & ((row % c) >= (col % c))
    strict = same & ((row % c) > (col % c))
    r3 = _iota((n, 3 * n), 0)
    c3 = _iota((n, 3 * n), 1) % n
    ltri3 = (((r3 // c) == (c3 // c)) & ((r3 % c) >= (c3 % c))).astype(BF16)
    return ltri3, incl, strict, strict.astype(F32)


def _gdn_pair_chunk(q, k, v, g, beta, states, consts):
    ltri3, incl, strict, strict_f = consts
    n = q.shape[0]
    c = n // 2
    cum = _mask_mm_left(ltri3, jnp.concatenate([g * strict_f, jnp.broadcast_to(g, (n, LANES))], axis=1))
    yield
    gc = cum[:, n:]
    dec = jnp.exp(jnp.where(incl, cum[:, :n], -1e30))
    kq = _nt(jnp.concatenate([k, q], axis=0), k)
    yield
    a_mat = jnp.where(strict, beta * kq[:n] * dec, 0.0)
    eg = jnp.exp(gc)
    uw = yield from _nilpotent_solve(-a_mat, jnp.concatenate([beta * v, beta * eg * k], axis=1), int(math.log2(c)))
    u, w = uw[:, :LANES], uw[:, LANES:]
    head0 = _iota((n, 1), 0) < c
    head0_2 = (_iota((2 * n, 1), 0) % n) < c
    by_head = lambda x, m: jnp.concatenate([jnp.where(m, x, 0.0), jnp.where(m, 0.0, x)], axis=1)
    wq = jnp.concatenate([w, q * eg], axis=0)
    ws = _nn(by_head(wq, head0_2), jnp.concatenate(states, axis=0))
    yield
    v_new = u - ws[:n]
    o = ws[n:] + _nn(kq[n:] * dec, v_new)
    yield
    gl = jnp.where(head0, gc[c - 1:c, :], gc[n - 1:n, :])
    upd = _tn(k * jnp.exp(gl - gc), by_head(v_new, head0))
    yield
    new_states = [s * jnp.exp(gc[(hi + 1) * c - 1:(hi + 1) * c, :]) + upd[:, hi * LANES:(hi + 1) * LANES]
                  for hi, s in enumerate(states)]
    return o, new_states


def _gdn_kernel(q_ref, k_ref, v_ref, z_ref, ba_ref, cwq_ref, cwk_ref, cwv_ref, cq_ref, ck_ref, cv_ref,
                par_ref, nw_ref, s0_ref, o_ref, so_ref, s_ref, buf_ref, qs_ref, ks_ref, vs_ref,
                *, heads, conv_w):
    hg = pl.program_id(1)
    l = pl.program_id(2)
    t, width = q_ref.shape
    hb = width // LANES
    c = min(CHUNK, t)
    pad = 8
    hist = conv_w - 1
    streams = ((q_ref, cq_ref, cwq_ref, qs_ref), (k_ref, ck_ref, cwk_ref, ks_ref), (v_ref, cv_ref, cwv_ref, vs_ref))

    @pl.when(l == 0)
    def _():
        s_ref[...] = s0_ref[...]
        for i, (_, c_ref, _, _) in enumerate(streams):
            buf_ref[i, pad - hist:pad, :] = c_ref[...]

    for i, (x_ref, _, w_ref, dst_ref) in enumerate(streams):
        x = x_ref[...]
        w = w_ref[...]
        y = x * w[hist:hist + 1]
        for shift in range(1, conv_w):
            y = y + pltpu.roll(x, shift, 0) * w[hist - shift:hist - shift + 1]
        dst_ref[...] = _silu(y)
        buf_ref[i, pad:2 * pad, :] = x[0:pad]
        head = jnp.zeros((pad, width), F32)
        for j in range(conv_w):
            head = head + buf_ref[i, pad - hist + j:2 * pad - hist + j, :] * w[j:j + 1]
        dst_ref[0:pad, :] = _silu(head)
        buf_ref[i, pad - hist:pad, :] = x[t - hist:t]

    par = par_ref[...]
    lane = _iota((1, LANES), 1)
    consts = _gdn_consts(c)
    nw = nw_ref[...]

    def body(ci, carry):
        sl = pl.ds(pl.multiple_of(ci * c, c), c)
        bac = ba_ref[sl, :]
        bac = jnp.where(lane < heads, _sigmoid(bac), -jnp.exp(par[0:1]) * _softplus(bac + par[1:2]))
        column = lambda idx: jnp.sum(jnp.where(lane == idx, bac, 0.0), axis=-1, keepdims=True)
        operands = []
        for pi in range(hb // 2):
            local = (2 * pi, 2 * pi + 1)
            cols = [slice(h * LANES, (h + 1) * LANES) for h in local]
            stack = lambda ref: jnp.concatenate([ref[sl, cs] for cs in cols], axis=0)
            beta = jnp.concatenate([column(hg * hb + h) for h in local], axis=0)
            g = jnp.concatenate([column(heads + hg * hb + h) for h in local], axis=0)
            operands.append((stack(qs_ref), stack(ks_ref), stack(vs_ref), g, beta, [s_ref[h] for h in local],
                             [z_ref[sl, cs] for cs in cols]))
        chains = []
        for q, k, v, g, beta, states, _ in operands:
            q = q * lax.rsqrt(jnp.sum(q * q, axis=-1, keepdims=True) + 1e-12) * LANES ** -0.5
            k = k * lax.rsqrt(jnp.sum(k * k, axis=-1, keepdims=True) + 1e-12)
            chains.append(_gdn_pair_chunk(q, k, v, g, beta, states, consts))
        results = _lockstep(chains)
        for pi, (o, new_states) in enumerate(results):
            z = operands[pi][6]
            for i in range(2):
                h = 2 * pi + i
                s_ref[h] = new_states[i]
                oh = o[i * c:(i + 1) * c]
                ms = jnp.mean(oh * oh, axis=-1, keepdims=True)
                o_ref[sl, h * LANES:(h + 1) * LANES] = (
                    oh * lax.rsqrt(ms + NORM_EPS) * nw * _silu(z[i])).astype(o_ref.dtype)
        return carry

    lax.fori_loop(0, t // c, body, 0)

    @pl.when(l == pl.num_programs(2) - 1)
    def _():
        so_ref[...] = s_ref[...]


def _gdn(p, conv_w, conv_state, par, norm_w, s0, batch, seq, heads, tile):
    tile = min(tile, seq)
    nl = seq // tile
    cw = conv_w.shape[0]
    hb = GDN_HEADS_PER_STEP
    assert heads % hb == 0 and hb % 2 == 0
    groups = heads // hb
    width = hb * LANES
    pblk = lambda off: pl.BlockSpec((tile, width), lambda b, h, l, off=off: (b * nl + l, off * groups + h))
    wblk = lambda off: pl.BlockSpec((cw, width), lambda b, h, l, off=off: (0, off * groups + h))
    cblk = lambda off: pl.BlockSpec((None, cw - 1, width), lambda b, h, l, off=off: (b, 0, off * groups + h))
    st_spec = pl.BlockSpec((None, hb, LANES, LANES), lambda b, h, l: (b, h, 0, 0))
    tbuf = pltpu.VMEM((tile, width), F32)
    return pl.pallas_call(
        functools.partial(_gdn_kernel, heads=heads, conv_w=cw),
        grid=(batch, groups, nl),
        in_specs=[pblk(0), pblk(1), pblk(2), pblk(3),
                  pl.BlockSpec((tile, LANES), lambda b, h, l: (b * nl + l, 4 * heads)),
                  wblk(0), wblk(1), wblk(2), cblk(0), cblk(1), cblk(2),
                  pl.BlockSpec(par.shape, lambda b, h, l: (0, 0)),
                  pl.BlockSpec((1, LANES), lambda b, h, l: (0, 0)),
                  st_spec],
        out_specs=[pl.BlockSpec((tile, width), lambda b, h, l: (b * nl + l, h)), st_spec],
        out_shape=[jax.ShapeDtypeStruct((batch * seq, heads * LANES), BF16), jax.ShapeDtypeStruct(s0.shape, F32)],
        scratch_shapes=[pltpu.VMEM((hb, LANES, LANES), F32),
                        pltpu.VMEM((3, 16, width), F32),
                        tbuf, tbuf, tbuf],
        compiler_params=_params(("parallel", "parallel", "arbitrary")),
        name="gdn",
    )(p, p, p, p, p, conv_w, conv_w, conv_w, conv_state, conv_state, conv_state, par, norm_w, s0)


def _pad_cols(w, total):
    return jnp.pad(w, ((0, 0),) * (w.ndim - 1) + ((0, total - w.shape[-1]),))


def _ab_layout(a_cols, wb, lora_w, lora_a, lora_g):
    assert lora_w == LANES and lora_a == LANES and lora_g <= 512
    xg_off = -(-(3 * wb + 2 * LANES) // 512) * 512
    b_width = xg_off + 512
    assert a_cols % 512 == 0
    return xg_off, b_width


def _rwkv_cols(t, wb, xg_off, b_width, lora_g):
    head = t[..., :3 * wb + 2 * LANES]
    tail = t[..., 3 * wb + 2 * LANES:]
    z = lambda n: jnp.zeros(t.shape[:-1] + (n,), t.dtype)
    return jnp.concatenate([head, z(xg_off - head.shape[-1]), tail, z(b_width - xg_off - lora_g)], axis=-1)


def _trunk(x, mod, s_hgrn, s_rwkv, s_shift, s_gdn, s_conv, wts, tiles):
    batch, seq, d = x.shape
    m = batch * seq
    x2 = x.reshape(m, d)
    tm, t_hgrn, t_rwkv, t_gdn = tiles
    outs = {}
    depth = mod.shape[0]
    for layer in range(depth):
        j = layer // 2
        vecs6 = [mod[layer, :, i * d:(i + 1) * d].reshape(batch, 1, d) for i in range(6)]
        sh1, sc1, g1, sh2, sc2, g2 = vecs6
        if layer % 2 == 0:
            w = wts["ab"][j]
            p = _norm_matmul(x2, sc1, sh1, w["w_in"], seq, tm, 512)
            ha, wb = w["ha"], w["wb"]
            oa, st_a = _hgrn(p, w["lb"], w["hgrn_norm_w"], jnp.swapaxes(s_hgrn[j], -1, -2), batch, seq, ha, t_hgrn)
            pairs = wb // LANES
            hb = s_rwkv.shape[2]
            nb = s_rwkv.shape[-1]
            sp = s_rwkv[j].reshape(batch, pairs, 2, nb, nb)
            zero = jnp.zeros_like(sp[:, :, 0])
            s0 = jnp.concatenate([jnp.concatenate([sp[:, :, 0], zero], -1),
                                  jnp.concatenate([zero, sp[:, :, 1]], -1)], -2)
            shift = _rwkv_cols(s_shift[j], wb, w["xg_off"], w["b_width"], w["lora_g"])
            ob, st_b = _rwkv(p, shift, w["vecs"], w["mus"], w["w2"], w["a2"], w["g2"], s0,
                             batch, seq, pairs, w["a_cols"], t_rwkv)
            mix = jnp.concatenate([oa, ob], axis=-1)
            outs.setdefault("hgrn", []).append(jnp.swapaxes(st_a, -1, -2))
            outs.setdefault("rwkv", []).append(
                jnp.stack([st_b[:, :, :nb, :nb], st_b[:, :, nb:, nb:]], axis=2).reshape(batch, hb, nb, nb))
            last = p.reshape(batch, seq, -1)[:, seq - 1:, w["a_cols"]:]
            n_head = 3 * wb + 2 * LANES
            outs.setdefault("shift", []).append(
                jnp.concatenate([last[..., :n_head], last[..., w["xg_off"]:w["xg_off"] + w["lora_g"]]], axis=-1))
            w_out = w["w_out"]
        else:
            w = wts["gdn"][j]
            p = _norm_matmul(x2, sc1, sh1, w["w_in"], seq, tm, 512)
            hc = w["hc"]
            mix, st_c = _gdn(p, w["conv_w"], s_conv[j], w["par"], w["norm_w"], s_gdn[j], batch, seq, hc, t_gdn)
            outs.setdefault("gdn", []).append(st_c)
            cw = w["conv_w"].shape[0]
            raw = p.reshape(batch, seq, -1)[:, :, :3 * hc * LANES]
            prev = jnp.concatenate([s_conv[j], raw[:, max(seq - (cw - 1), 0):]], axis=1)
            outs.setdefault("conv", []).append(prev[:, -(cw - 1):])
            w_out = w["w_out"]
        x2 = _out_residual(mix, w_out, x2, g1, seq, tm, 512)
        x2 = _mlp(x2, sc2, sh2, g2, wts["mlp_w1"][layer], wts["mlp_w2"][layer], seq, min(tm, 512), 512, 256)
    x2 = _final_norm(x2, wts["final_w"], 256)
    st = lambda name: jnp.stack(outs[name])
    return x2.reshape(batch, seq, d), st("hgrn"), st("rwkv"), st("shift"), st("gdn"), st("conv")


def kernel(x_prompt, x_sample, c_prompt, c_sample, state_hgrn, state_rwkv, state_rwkv_shift, state_gdn,
           state_gdn_conv, ada_w, ada_b, mlp_w1, mlp_w2, final_norm_w, ab_w_in, ab_w_out, hgrn_lb_logits,
           hgrn_norm_w, rwkv_mu, rwkv_w0, rwkv_w2, rwkv_a0, rwkv_a2, rwkv_g2, rwkv_k_k, rwkv_k_a, rwkv_r_k,
           rwkv_ln_w, rwkv_ln_b, gdn_w_in, gdn_w_out, gdn_conv_w, gdn_a_log, gdn_dt_bias, gdn_norm_w):
    d = x_prompt.shape[-1]
    nb_p, nb_s = x_prompt.shape[0], x_sample.shape[0]
    n_ab, n_c = ab_w_in.shape[0], gdn_w_in.shape[0]
    ha = state_hgrn.shape[2]
    a_cols = 4 * ha * LANES
    wb = rwkv_w0.shape[1]
    lora_w, lora_a, lora_g = rwkv_w2.shape[1], rwkv_a2.shape[1], rwkv_g2.shape[1]
    xg_off, b_width = _ab_layout(a_cols, wb, lora_w, lora_a, lora_g)
    hc = state_gdn.shape[2]

    rows = nb_p + nb_s
    rows_pad = -(-rows // 8) * 8
    c_all = jnp.pad(jnp.concatenate([c_prompt, c_sample], axis=0), ((0, rows_pad - rows), (0, 0)))
    mod = _adaln(c_all, ada_w, ada_b)

    lbs = jnp.cumsum(jax.nn.softmax(hgrn_lb_logits.astype(F32), axis=0), axis=0)
    wts = {"ab": [], "gdn": [], "final_w": final_norm_w.reshape(1, d),
           "mlp_w1": [mlp_w1[i].astype(BF16) for i in range(mlp_w1.shape[0])],
           "mlp_w2": [mlp_w2[i].astype(BF16) for i in range(mlp_w2.shape[0])]}
    for j in range(n_ab):
        w_in = ab_w_in[j]
        w_in = jnp.concatenate([w_in[:, :a_cols], _rwkv_cols(w_in[:, a_cols:], wb, xg_off, b_width, lora_g)], axis=-1)
        mu = _rwkv_cols(rwkv_mu[j][None], wb, xg_off, b_width, lora_g)[0]
        vec_rows = [mu[:wb], mu[wb:2 * wb], mu[2 * wb:3 * wb], rwkv_w0[j], rwkv_a0[j], rwkv_k_k[j], rwkv_k_a[j],
                    rwkv_r_k[j].reshape(-1), rwkv_ln_w[j], rwkv_ln_b[j]]
        vecs = jnp.pad(jnp.stack(vec_rows), ((0, 16 - len(vec_rows)), (0, 0)))
        mus = jnp.stack([_pad_cols(mu[3 * wb:3 * wb + LANES], 512), _pad_cols(mu[3 * wb + LANES:3 * wb + 2 * LANES], 512),
                         mu[xg_off:xg_off + 512]])
        wts["ab"].append({
            "w_in": w_in.astype(BF16), "w_out": ab_w_out[j].astype(BF16), "lb": lbs[j][None],
            "hgrn_norm_w": hgrn_norm_w[j][None], "vecs": vecs, "mus": jnp.pad(mus, ((0, 5), (0, 0))),
            "w2": rwkv_w2[j].astype(BF16), "a2": rwkv_a2[j].astype(BF16),
            "g2": jnp.pad(rwkv_g2[j], ((0, 512 - lora_g), (0, 0))).astype(BF16),
            "ha": ha, "wb": wb, "a_cols": a_cols, "xg_off": xg_off, "b_width": b_width, "lora_g": lora_g})
    for j in range(n_c):
        cols = gdn_w_in.shape[-1]
        cols_pad = -(-cols // 512) * 512
        par = jnp.zeros((8, LANES), F32)
        par = par.at[0, hc:2 * hc].set(gdn_a_log[j]).at[1, hc:2 * hc].set(gdn_dt_bias[j])
        wts["gdn"].append({
            "w_in": _pad_cols(gdn_w_in[j], cols_pad).astype(BF16), "w_out": gdn_w_out[j].astype(BF16),
            "conv_w": gdn_conv_w[j], "par": par, "norm_w": gdn_norm_w[j][None], "hc": hc})

    zeros = lambda s: jnp.zeros((s.shape[0], nb_p) + s.shape[2:], x_prompt.dtype)
    y_p, hg_p, rw_p, sh_p, gd_p, cv_p = _trunk(
        x_prompt, mod[:, :nb_p], zeros(state_hgrn), zeros(state_rwkv), zeros(state_rwkv_shift),
        zeros(state_gdn), zeros(state_gdn_conv), wts, PROMPT_TILES)
    y_s, hg_s, rw_s, sh_s, gd_s, cv_s = _trunk(
        x_sample, mod[:, nb_p:rows], state_hgrn, state_rwkv, state_rwkv_shift, state_gdn, state_gdn_conv,
        wts, SAMPLE_TILES)
    return (y_p, y_s, hg_p, rw_p, sh_p, gd_p, cv_p, hg_s, rw_s, sh_s, gd_s, cv_s)
```

```python
import functools
import math

import jax
import jax.numpy as jnp
from jax import lax
from jax.experimental import pallas as pl
from jax.experimental.pallas import tpu as pltpu

F32 = jnp.float32
BF16 = jnp.bfloat16
HI = lax.Precision.HIGHEST

NORM_EPS = 1e-6
RWKV_LN_EPS = 64e-5
CHUNK = 64
DIAG = 8
LANES = 128
VMEM_LIMIT = 56 * 1024 * 1024
MLP_VMEM_LIMIT = 60 * 1024 * 1024
PROMPT_TILES = (1024, 512, 256, 256)
SAMPLE_TILES = (512, 64, 64, 64)
HGRN_HEADS_PER_STEP = 8
GDN_SOLVE_HI_LO = True
RWKV_PAIRS_PER_STEP = 8
GDN_HEADS_PER_STEP = 16


def _mm(a, b, ca, cb, exact=False):
    dims = (((ca,), (cb,)), ((), ()))
    if exact:
        return lax.dot_general(a, b, dims, precision=HI, preferred_element_type=F32)
    return lax.dot_general(a.astype(BF16), b.astype(BF16), dims, preferred_element_type=F32)


def _nn(a, b, exact=False):
    return _mm(a, b, 1, 0, exact)


def _nt(a, b, exact=False):
    return _mm(a, b, 1, 1, exact)


def _tn(a, b, exact=False):
    return _mm(a, b, 0, 0, exact)


def _sigmoid(x):
    return 1.0 / (1.0 + jnp.exp(-x))


def _silu(x):
    return x * _sigmoid(x)


def _softplus(x):
    return jnp.maximum(x, 0.0) + jnp.log(1.0 + jnp.exp(-jnp.abs(x)))


def _iota(shape, dim):
    return lax.broadcasted_iota(jnp.int32, shape, dim)


def _split2(x):
    hi = x.astype(BF16)
    return hi, (x - hi.astype(F32)).astype(BF16)


def _split3(x):
    hi = x.astype(BF16)
    r = x - hi.astype(F32)
    lo = r.astype(BF16)
    return hi, lo, (r - lo.astype(F32)).astype(BF16)


def _mm_hi(a, b, a_parts=None, b_parts=None):
    a_hi, a_lo = a_parts if a_parts is not None else _split2(a)
    b_hi, b_lo = b_parts if b_parts is not None else _split2(b)
    return jnp.dot(jnp.concatenate([a_hi, a_lo, a_hi], axis=1), jnp.concatenate([b_hi, b_hi, b_lo], axis=0),
                   preferred_element_type=F32)


def _mask_mm_left(mask3, x):
    return jnp.dot(mask3, jnp.concatenate(_split3(x), axis=0), preferred_element_type=F32)


def _mask_mm_right(x, mask3):
    return jnp.dot(jnp.concatenate(_split3(x), axis=1), mask3, preferred_element_type=F32)


def _nilpotent_solve(m, x, stages, hi_lo=True):
    n = m.shape[0]
    mm = _mm_hi if hi_lo else _nn
    for k in range(stages):
        last = k == stages - 1
        y = mm(m, x if last else jnp.concatenate([m, x], axis=1))
        yield
        if last:
            x = x + y
        else:
            m = y[:, :n]
            x = x + y[:, n:]
    return x


def _lockstep(generators):
    generators = list(generators)
    results = [None] * len(generators)
    live = list(range(len(generators)))
    while live:
        for i in list(live):
            try:
                next(generators[i])
            except StopIteration as stop:
                results[i] = stop.value
                live.remove(i)
    return results


def _params(sem, vmem=VMEM_LIMIT):
    return pltpu.CompilerParams(dimension_semantics=sem, vmem_limit_bytes=vmem)


def _rows_per_tile(rows_per_batch, tile):
    if tile <= rows_per_batch:
        assert rows_per_batch % tile == 0
        return 1
    assert tile % rows_per_batch == 0
    return tile // rows_per_batch


def _adaln_kernel(c_ref, w_ref, b_ref, o_ref):
    c = _silu(c_ref[...]).astype(BF16)
    o_ref[...] = jnp.dot(c, w_ref[...].astype(BF16), preferred_element_type=F32) + b_ref[...]


def _adaln(c, ada_w, ada_b):
    depth, d, n = ada_w.shape
    rows = c.shape[0]
    tn = 512 if n % 512 == 0 else n
    return pl.pallas_call(
        _adaln_kernel,
        grid=(depth, n // tn),
        in_specs=[pl.BlockSpec((rows, d), lambda l, j: (0, 0)),
                  pl.BlockSpec((None, d, tn), lambda l, j: (l, 0, j)),
                  pl.BlockSpec((None, 1, tn), lambda l, j: (l, 0, j))],
        out_specs=pl.BlockSpec((None, rows, tn), lambda l, j: (l, 0, j)),
        out_shape=jax.ShapeDtypeStruct((depth, rows, n), F32),
        compiler_params=_params(("parallel", "parallel")),
        name="adaln",
    )(c, ada_w, ada_b.reshape(depth, 1, n))


def _modulated_norm(x_ref, sc_ref, sh_ref, h_ref, groups):
    rows = x_ref.shape[0] // groups
    step = min(rows, 128)

    for gi in range(groups):
        sc = 1.0 + sc_ref[gi]
        sh = sh_ref[gi]

        def body(i, carry, gi=gi, sc=sc, sh=sh):
            sl = pl.ds(pl.multiple_of(gi * rows + i * step, step), step)
            x = x_ref[sl, :]
            ms = jnp.mean(x * x, axis=-1, keepdims=True)
            h_ref[sl, :] = (x * lax.rsqrt(ms + NORM_EPS) * sc + sh).astype(BF16)
            return carry

        lax.fori_loop(0, rows // step, body, 0)


def _norm_mm_kernel(x_ref, sc_ref, sh_ref, w_ref, o_ref, h_ref, *, groups):
    @pl.when(pl.program_id(1) == 0)
    def _():
        _modulated_norm(x_ref, sc_ref, sh_ref, h_ref, groups)

    o_ref[...] = jnp.dot(h_ref[...], w_ref[...], preferred_element_type=F32).astype(o_ref.dtype)


def _norm_matmul(x, scale, shift, w, rows_per_batch, tm, tn):
    m, d = x.shape
    n = w.shape[1]
    tm = min(tm, m)
    groups = _rows_per_tile(rows_per_batch, tm)
    bidx = (lambda i, j: (i * tm // rows_per_batch, 0, 0)) if groups == 1 else (lambda i, j: (i, 0, 0))
    return pl.pallas_call(
        functools.partial(_norm_mm_kernel, groups=groups),
        grid=(m // tm, n // tn),
        in_specs=[pl.BlockSpec((tm, d), lambda i, j: (i, 0), pipeline_mode=pl.Buffered(1)),
                  pl.BlockSpec((groups, 1, d), bidx),
                  pl.BlockSpec((groups, 1, d), bidx),
                  pl.BlockSpec((d, tn), lambda i, j: (0, j))],
        out_specs=pl.BlockSpec((tm, tn), lambda i, j: (i, j)),
        out_shape=jax.ShapeDtypeStruct((m, n), F32),
        scratch_shapes=[pltpu.VMEM((tm, d), BF16)],
        compiler_params=_params(("parallel", "arbitrary")),
        name="norm_matmul",
    )(x, scale, shift, w)


def _out_res_kernel(a_ref, w_ref, x_ref, g_ref, o_ref, *, groups):
    acc = jnp.dot(a_ref[...], w_ref[...], preferred_element_type=F32)
    rows = acc.shape[0] // groups
    for gi in range(groups):
        sl = slice(gi * rows, (gi + 1) * rows)
        o_ref[sl, :] = x_ref[sl, :] + g_ref[gi] * acc[sl, :]


def _out_residual(a, w, x, gate, rows_per_batch, tm, tn):
    m, k = a.shape
    n = w.shape[1]
    tm = min(tm, m)
    groups = _rows_per_tile(rows_per_batch, tm)
    bidx = (lambda i, j: (i * tm // rows_per_batch, 0, j)) if groups == 1 else (lambda i, j: (i, 0, j))
    return pl.pallas_call(
        functools.partial(_out_res_kernel, groups=groups),
        grid=(m // tm, n // tn),
        in_specs=[pl.BlockSpec((tm, k), lambda i, j: (i, 0)),
                  pl.BlockSpec((k, tn), lambda i, j: (0, j)),
                  pl.BlockSpec((tm, tn), lambda i, j: (i, j)),
                  pl.BlockSpec((groups, 1, tn), bidx)],
        out_specs=pl.BlockSpec((tm, tn), lambda i, j: (i, j)),
        out_shape=jax.ShapeDtypeStruct((m, n), F32),
        compiler_params=_params(("parallel", "parallel")),
        name="out_residual",
    )(a, w, x, gate)


def _mlp_kernel(x_ref, sc_ref, sh_ref, xr_ref, g_ref, w1_ref, w2_ref, o_ref, h_ref, u_ref, *, groups, nf):
    j = pl.program_id(1)

    @pl.when(j == 0)
    def _():
        _modulated_norm(x_ref, sc_ref, sh_ref, h_ref, groups)

    @pl.when(j < nf)
    def _():
        u = jnp.dot(h_ref[...], w1_ref[...], preferred_element_type=F32)
        u_ref[j] = jnp.square(jnp.maximum(u, 0.0)).astype(BF16)

    @pl.when(j >= nf)
    def _():
        tf = u_ref.shape[2]
        acc = jnp.dot(u_ref[0], w2_ref[0:tf, :], preferred_element_type=F32)
        for f in range(1, nf):
            acc += jnp.dot(u_ref[f], w2_ref[f * tf:(f + 1) * tf, :], preferred_element_type=F32)
        rows = acc.shape[0] // groups
        for gi in range(groups):
            sl = slice(gi * rows, (gi + 1) * rows)
            o_ref[sl, :] = xr_ref[sl, :] + g_ref[gi] * acc[sl, :]


def _mlp(x, scale, shift, gate, w1, w2, rows_per_batch, tm, tf, tn):
    m, d = x.shape
    ff = w1.shape[1]
    tm = min(tm, m)
    nf, nn = ff // tf, d // tn
    groups = _rows_per_tile(rows_per_batch, tm)
    batch_of = (lambda i: i * tm // rows_per_batch) if groups == 1 else (lambda i: i)
    vec = pl.BlockSpec((groups, 1, d), lambda i, j: (batch_of(i), 0, 0))
    col = lambda j: jnp.maximum(j - nf, 0)
    return pl.pallas_call(
        functools.partial(_mlp_kernel, groups=groups, nf=nf),
        grid=(m // tm, nf + nn),
        in_specs=[pl.BlockSpec((tm, d), lambda i, j: (i, 0), pipeline_mode=pl.Buffered(1)),
                  vec, vec,
                  pl.BlockSpec((tm, tn), lambda i, j: (i, col(j))),
                  pl.BlockSpec((groups, 1, tn), lambda i, j: (batch_of(i), 0, col(j))),
                  pl.BlockSpec((d, tf), lambda i, j: (0, jnp.minimum(j, nf - 1))),
                  pl.BlockSpec((ff, tn), lambda i, j: (0, col(j)))],
        out_specs=pl.BlockSpec((tm, tn), lambda i, j: (i, col(j))),
        out_shape=jax.ShapeDtypeStruct((m, d), F32),
        scratch_shapes=[pltpu.VMEM((tm, d), BF16), pltpu.VMEM((nf, tm, tf), BF16)],
        compiler_params=_params(("parallel", "arbitrary"), MLP_VMEM_LIMIT),
        name="mlp",
    )(x, scale, shift, x, gate, w1, w2)


def _final_norm_kernel(x_ref, w_ref, o_ref):
    x = x_ref[...]
    ms = jnp.mean(x * x, axis=-1, keepdims=True)
    o_ref[...] = x * lax.rsqrt(ms + NORM_EPS) * w_ref[...]


def _final_norm(x, w, tm):
    m, d = x.shape
    tm = min(tm, m)
    return pl.pallas_call(
        _final_norm_kernel,
        grid=(m // tm,),
        in_specs=[pl.BlockSpec((tm, d), lambda i: (i, 0)), pl.BlockSpec((1, d), lambda i: (0, 0))],
        out_specs=pl.BlockSpec((tm, d), lambda i: (i, 0)),
        out_shape=jax.ShapeDtypeStruct((m, d), F32),
        compiler_params=_params(("parallel",)),
        name="final_norm",
    )(x, w)


def _hgrn_consts(c):
    row = _iota((c, c), 0)
    col = _iota((c, c), 1)
    ltri = (_iota((c, 3 * c), 0) >= _iota((c, 3 * c), 1) % c).astype(BF16)
    levels = []
    gs = 2 * DIAG
    while gs <= c:
        half = gs // 2
        pair = (row // gs == col // gs) & (row % gs >= half) & (col % gs < half)
        levels.append((gs, pair))
        gs *= 2
    rix = _iota((c, 1), 0)
    return ltri, levels, rix


def _hgrn_chunk(q, k, v, g, st, consts):
    ltri, levels, rix = consts
    c = q.shape[0]
    gc = _mask_mm_left(ltri, g)
    yield
    lhs, rhs = [q * jnp.exp(gc)], [st]
    for gs, _ in levels:
        half = gs // 2
        ref = jnp.concatenate(
            [jnp.broadcast_to(gc[m0 * gs + half - 1:m0 * gs + half, :], (gs, gc.shape[1])) for m0 in range(c // gs)],
            axis=0)
        is_q = (rix % gs) >= half
        lhs.append(jnp.where(is_q, q * jnp.exp(gc - ref), 0.0))
        rhs.append(jnp.where(is_q, 0.0, k * jnp.exp(ref - gc)))
        rhs.append(jnp.zeros((LANES - c, k.shape[1]), F32))
    prod = _nt(jnp.concatenate(lhs, axis=0), jnp.concatenate(rhs, axis=0))
    gl = gc[c - 1:c, :]
    upd = _tn(v, k * jnp.exp(gl - gc))
    yield
    att = jnp.zeros((c, c), F32)
    for li, (_, pair) in enumerate(levels):
        att = att + jnp.where(pair, prod[(li + 1) * c:(li + 2) * c, (li + 1) * LANES:(li + 1) * LANES + c], 0.0)
    o = prod[:c, :LANES] + _nn(att, v)
    for dist in range(DIAG):
        if dist == 0:
            kd, gd, vd = k, gc, v
        else:
            kd = pltpu.roll(k, dist, 0)
            gd = pltpu.roll(gc, dist, 0)
            vd = pltpu.roll(v, dist, 0)
        w = jnp.sum(q * kd * jnp.exp(gc - gd), axis=-1, keepdims=True)
        o = o + jnp.where((rix % DIAG) >= dist, w, 0.0) * vd
    yield
    return o, st * jnp.exp(gl) + upd


def _hgrn_kernel(q_ref, f_ref, i_ref, g_ref, lb_ref, nw_ref, s0_ref, o_ref, so_ref, s_ref, *, dk):
    l = pl.program_id(2)

    @pl.when(l == 0)
    def _():
        s_ref[...] = s0_ref[...]

    t, width = q_ref.shape
    c = min(CHUNK, t)
    consts = _hgrn_consts(c)
    lb = lb_ref[...]
    nw = nw_ref[...]

    def body(ci, carry):
        sl = pl.ds(pl.multiple_of(ci * c, c), c)
        f = lb + (1.0 - lb) * _sigmoid(f_ref[sl, :])
        q = _silu(q_ref[sl, :]) * dk ** -0.5
        k = 1.0 - f
        v = i_ref[sl, :]
        g = jnp.log(f)
        gate = _silu(g_ref[sl, :])
        heads = [slice(h * dk, (h + 1) * dk) for h in range(width // dk)]
        results = _lockstep(_hgrn_chunk(q[:, hs], k[:, hs], v[:, hs], g[:, hs], s_ref[h], consts)
                            for h, hs in enumerate(heads))
        for h, (o, st) in enumerate(results):
            s_ref[h] = st
            ms = jnp.mean(o * o, axis=-1, keepdims=True)
            o_ref[sl, heads[h]] = (o * lax.rsqrt(ms + NORM_EPS) * nw * gate[:, heads[h]]).astype(o_ref.dtype)
        return carry

    lax.fori_loop(0, t // c, body, 0)

    @pl.when(l == pl.num_programs(2) - 1)
    def _():
        so_ref[...] = s_ref[...]


def _hgrn(p, lb, norm_w, s0t, batch, seq, heads, tile):
    dk = LANES
    tile = min(tile, seq)
    nl = seq // tile
    hb = HGRN_HEADS_PER_STEP
    assert heads % hb == 0
    groups = heads // hb
    width = hb * dk
    blk = lambda off: pl.BlockSpec((tile, width), lambda b, h, l, off=off: (b * nl + l, off * groups + h))
    st_spec = pl.BlockSpec((None, hb, dk, dk), lambda b, h, l: (b, h, 0, 0))
    return pl.pallas_call(
        functools.partial(_hgrn_kernel, dk=dk),
        grid=(batch, groups, nl),
        in_specs=[blk(0), blk(1), blk(2), blk(3),
                  pl.BlockSpec((1, width), lambda b, h, l: (0, h)),
                  pl.BlockSpec((1, dk), lambda b, h, l: (0, 0)),
                  st_spec],
        out_specs=[pl.BlockSpec((tile, width), lambda b, h, l: (b * nl + l, h)), st_spec],
        out_shape=[jax.ShapeDtypeStruct((batch * seq, heads * dk), BF16),
                   jax.ShapeDtypeStruct(s0t.shape, F32)],
        scratch_shapes=[pltpu.VMEM((hb, dk, dk), F32)],
        compiler_params=_params(("parallel", "parallel", "arbitrary")),
        name="hgrn2",
    )(p, p, p, p, lb, norm_w, s0t)


def _rwkv_consts(c):
    n = 4 * c
    row = _iota((n, n), 0)
    col = _iota((n, n), 1)
    same_head = ((row // c) % 2) == ((col // c) % 2)
    strict = (row % c) > (col % c)
    incl = (row % c) >= (col % c)
    keep = same_head & (strict | ((row >= 2 * c) & incl))
    ltri = (_iota((c, 3 * c), 0) >= _iota((c, 3 * c), 1) % c).astype(BF16)
    head0 = _iota((1, LANES), 1) < (LANES // 2)
    rs = _iota((LANES, LANES), 0)
    cs = _iota((LANES, LANES), 1)
    bdiag = (rs // (LANES // 2)) == (cs // (LANES // 2))
    return keep, ltri, head0, bdiag


def _rwkv_chunk(r, lg, k, v, a, b, s, consts):
    keep, ltri, head0, bdiag = consts
    c = r.shape[0]
    gc = _mask_mm_left(ltri, lg)
    yield
    e_incl = jnp.exp(gc)
    e_inv = jnp.exp(-gc)
    at = a * jnp.exp(gc - lg)
    rt = r * e_incl
    bt = b * e_inv
    kt = k * e_inv
    split = lambda x: [jnp.where(head0, x, 0.0), jnp.where(head0, 0.0, x)]
    lhs = jnp.concatenate(split(at) + split(rt), axis=0)
    rhs = jnp.concatenate([bt, bt, kt, kt, s], axis=0)
    pls = _nt(lhs, rhs)
    yield
    p = jnp.where(keep, pls[:, :4 * c], 0.0)
    ls = pls[:, 4 * c:]
    vst = jnp.concatenate(split(v), axis=0)
    pv = _nn(p[:, 2 * c:], vst)
    yield
    ust = yield from _nilpotent_solve(p[:2 * c, :2 * c], ls[:2 * c] + pv[:2 * c], int(math.log2(c)), hi_lo=False)
    ost = ls[2 * c:] + pv[2 * c:] + _nn(p[2 * c:, :2 * c], ust)
    yield
    o = ost[:c] + ost[c:]
    u = ust[:c] + ust[c:]
    gl = gc[c - 1:c, :]
    e_tail = jnp.exp(gl - gc)
    upd = _tn(jnp.concatenate([u, v], axis=0), jnp.concatenate([b * e_tail, k * e_tail], axis=0))
    yield
    return o, s * jnp.exp(gl) + jnp.where(bdiag, upd, 0.0)


def _rwkv_kernel(xr_ref, xk_ref, xv_ref, xw_ref, xa_ref, xg_ref,
                 hr_ref, hk_ref, hv_ref, hw_ref, ha_ref, hg_ref,
                 vec_ref, mus_ref, w2_ref, a2_ref, g2_ref, s0_ref,
                 o_ref, so_ref,
                 s_ref, buf_ref, bufs_ref, bufg_ref, r_ref, lg_ref, k_ref, v_ref, a_ref, b_ref, y_ref):
    l = pl.program_id(2)
    t, width = xr_ref.shape
    c = min(CHUNK, t)

    @pl.when(l == 0)
    def _():
        s_ref[...] = s0_ref[...]
        for i, h_ref in enumerate((hr_ref, hk_ref, hv_ref)):
            buf_ref[i, 0:1, :] = h_ref[...]
        for i, h_ref in enumerate((hw_ref, ha_ref)):
            bufs_ref[i, 0:1, :] = h_ref[...]
        bufg_ref[0:1, :] = hg_ref[...]

    first_row = _iota((t, 1), 0) == 0

    def shifted(carry, x_ref, mu):
        x = x_ref[...]
        prev = jnp.where(first_row, carry[0:1, :], pltpu.roll(x, 1, 0))
        carry[0:1, :] = x[t - 1:t, :]
        return x + (prev - x) * mu

    vec = vec_ref[...]
    mus = mus_ref[...]
    r = shifted(buf_ref.at[0], xr_ref, vec[0:1])
    k = shifted(buf_ref.at[1], xk_ref, vec[1:2])
    v = shifted(buf_ref.at[2], xv_ref, vec[2:3])
    xw = shifted(bufs_ref.at[0], xw_ref, mus[0:1, :LANES])
    xa = shifted(bufs_ref.at[1], xa_ref, mus[1:2, :LANES])
    xg = shifted(bufg_ref, xg_ref, mus[2:3])

    w0, a0, k_k, k_a, r_k, ln_w, ln_b = (vec[i:i + 1] for i in range(3, 10))
    half = LANES // 2
    ones2 = ((_iota((2 * LANES, LANES), 0) % LANES) // half == _iota((2 * LANES, LANES), 1) // half).astype(BF16)

    def head_sums(x):
        outs = []
        for blk in range(width // LANES):
            outs.append(jnp.dot(jnp.concatenate(_split2(x[:, blk * LANES:(blk + 1) * LANES]), axis=1), ones2,
                                preferred_element_type=F32))
        return jnp.concatenate(outs, axis=1)

    w_log = -_softplus(-(w0 + _nn(jnp.tanh(xw), w2_ref[...]))) - 0.5
    a_lr = _sigmoid(a0 + _nn(xa, a2_ref[...]))
    gate = _nn(_sigmoid(xg), g2_ref[...])
    kk = k * k_k
    kk = kk * lax.rsqrt(head_sums(kk * kk) + 1e-12)
    k = k * (1.0 + (a_lr - 1.0) * k_a)
    r_ref[...] = r
    lg_ref[...] = -jnp.exp(w_log)
    k_ref[...] = k
    v_ref[...] = v
    a_ref[...] = -kk
    b_ref[...] = kk * a_lr

    consts = _rwkv_consts(c)

    def body(ci, carry):
        sl = pl.ds(pl.multiple_of(ci * c, c), c)
        operands = []
        for pi in range(width // LANES):
            cols = slice(pi * LANES, (pi + 1) * LANES)
            operands.append((r_ref[sl, cols], lg_ref[sl, cols], k_ref[sl, cols], v_ref[sl, cols],
                             a_ref[sl, cols], b_ref[sl, cols], s_ref[pi]))
        results = _lockstep(_rwkv_chunk(*ops, consts) for ops in operands)
        for pi, (o, s_new) in enumerate(results):
            s_ref[pi] = s_new
            y_ref[sl, pi * LANES:(pi + 1) * LANES] = o
        return carry

    lax.fori_loop(0, t // c, body, 0)

    o = y_ref[...]
    inv_n = 1.0 / half
    mean = head_sums(o) * inv_n
    dev = o - mean
    var = head_sums(dev * dev) * inv_n
    o = dev * lax.rsqrt(var + RWKV_LN_EPS) * ln_w + ln_b
    o = o + head_sums(r * k * r_k) * v
    o_ref[...] = (o * gate).astype(o_ref.dtype)

    @pl.when(l == pl.num_programs(2) - 1)
    def _():
        so_ref[...] = s_ref[...]


def _rwkv(p, shift, vecs, mus, w2, a2, g2, s0, batch, seq, pairs, col0, tile):
    tile = min(tile, seq)
    nl = seq // tile
    wb = pairs * LANES
    pb = RWKV_PAIRS_PER_STEP
    width = pb * LANES
    groups = pairs // pb
    assert pairs % pb == 0 and col0 % width == 0
    cb0 = col0 // LANES
    gb0 = col0 // width
    xg_off = -(-(3 * wb + 2 * LANES) // 512) * 512
    assert (col0 + xg_off) % 512 == 0
    pblk = lambda off: pl.BlockSpec((tile, width), lambda b, h, l, off=off: (b * nl + l, gb0 + off * groups + h))
    pfix = lambda cb: pl.BlockSpec((tile, LANES), lambda b, h, l, cb=cb: (b * nl + l, cb))
    hblk = lambda off: pl.BlockSpec((None, 1, width), lambda b, h, l, off=off: (b, 0, off * groups + h))
    hfix = lambda cb: pl.BlockSpec((None, 1, LANES), lambda b, h, l, cb=cb: (b, 0, cb))
    st_spec = pl.BlockSpec((None, pb, LANES, LANES), lambda b, h, l: (b, h, 0, 0))
    col = lambda rows: pl.BlockSpec((rows, width), lambda b, h, l: (0, h))
    full = lambda a: pl.BlockSpec(a.shape, lambda b, h, l: (0,) * a.ndim)
    tbuf = pltpu.VMEM((tile, width), F32)
    return pl.pallas_call(
        _rwkv_kernel,
        grid=(batch, groups, nl),
        in_specs=[pblk(0), pblk(1), pblk(2), pfix(cb0 + 3 * pairs), pfix(cb0 + 3 * pairs + 1),
                  pl.BlockSpec((tile, 512), lambda b, h, l: (b * nl + l, (col0 + xg_off) // 512)),
                  hblk(0), hblk(1), hblk(2), hfix(3 * pairs), hfix(3 * pairs + 1),
                  pl.BlockSpec((None, 1, 512), lambda b, h, l: (b, 0, xg_off // 512)),
                  col(vecs.shape[0]), full(mus), col(w2.shape[0]), col(a2.shape[0]), col(g2.shape[0]), st_spec],
        out_specs=[pl.BlockSpec((tile, width), lambda b, h, l: (b * nl + l, h)), st_spec],
        out_shape=[jax.ShapeDtypeStruct((batch * seq, wb), BF16), jax.ShapeDtypeStruct(s0.shape, F32)],
        scratch_shapes=[pltpu.VMEM((pb, LANES, LANES), F32),
                        pltpu.VMEM((3, 8, width), F32),
                        pltpu.VMEM((2, 8, LANES), F32),
                        pltpu.VMEM((8, 512), F32),
                        tbuf, tbuf, tbuf, tbuf, tbuf, tbuf, tbuf],
        compiler_params=_params(("parallel", "parallel", "arbitrary")),
        name="rwkv7",
    )(p, p, p, p, p, p, shift, shift, shift, shift, shift, shift, vecs, mus, w2, a2, g2, s0)


def _gdn_consts(c):
    n = 2 * c
    row = _iota((n, n), 0)
    col = _iota((n, n), 1)
    same = (row // c) == (col // c)
    incl = same & ((row % c) >= (col % c))
    strict = same & ((row % c) > (col % c))
    r3 = _iota((n, 3 * n), 0)
    c3 = _iota((n, 3 * n), 1) % n
    ltri3 = (((r3 // c) == (c3 // c)) & ((r3 % c) >= (c3 % c))).astype(BF16)
    return ltri3, incl, strict, strict.astype(F32)


def _gdn_pair_chunk(q, k, v, g, beta, states, consts):
    ltri3, incl, strict, strict_f = consts
    n = q.shape[0]
    c = n // 2
    cum = _mask_mm_left(ltri3, jnp.concatenate([g * strict_f, jnp.broadcast_to(g, (n, LANES))], axis=1))
    yield
    gc = cum[:, n:]
    dec = jnp.exp(jnp.where(incl, cum[:, :n], -1e30))
    eg = jnp.exp(gc)
    kq = _nt(jnp.concatenate([k, q], axis=0), k)
    head0 = _iota((n, 1), 0) < c
    head0_2 = (_iota((2 * n, 1), 0) % n) < c
    by_head = lambda x, m: jnp.concatenate([jnp.where(m, x, 0.0), jnp.where(m, 0.0, x)], axis=1)
    ks = _nn(by_head(jnp.concatenate([k * eg, q * eg], axis=0), head0_2), jnp.concatenate(states, axis=0))
    yield
    a_mat = jnp.where(strict, beta * kq[:n] * dec, 0.0)
    v_new = yield from _nilpotent_solve(-a_mat, beta * (v - ks[:n]), int(math.log2(c)), hi_lo=GDN_SOLVE_HI_LO)
    o = ks[n:] + _nn(kq[n:] * dec, v_new)
    yield
    gl = jnp.where(head0, gc[c - 1:c, :], gc[n - 1:n, :])
    upd = _tn(k * jnp.exp(gl - gc), by_head(v_new, head0))
    yield
    new_states = [s * jnp.exp(gc[(hi + 1) * c - 1:(hi + 1) * c, :]) + upd[:, hi * LANES:(hi + 1) * LANES]
                  for hi, s in enumerate(states)]
    return o, new_states


def _gdn_kernel(q_ref, k_ref, v_ref, z_ref, ba_ref, cwq_ref, cwk_ref, cwv_ref, cq_ref, ck_ref, cv_ref,
                par_ref, nw_ref, s0_ref, o_ref, so_ref, s_ref, buf_ref, qs_ref, ks_ref, vs_ref,
                *, heads, conv_w):
    hg = pl.program_id(1)
    l = pl.program_id(2)
    t, width = q_ref.shape
    hb = width // LANES
    c = min(CHUNK, t)
    pad = 8
    hist = conv_w - 1
    streams = ((q_ref, cq_ref, cwq_ref, qs_ref), (k_ref, ck_ref, cwk_ref, ks_ref), (v_ref, cv_ref, cwv_ref, vs_ref))

    @pl.when(l == 0)
    def _():
        s_ref[...] = s0_ref[...]
        for i, (_, c_ref, _, _) in enumerate(streams):
            buf_ref[i, pad - hist:pad, :] = c_ref[...]

    for i, (x_ref, _, w_ref, dst_ref) in enumerate(streams):
        x = x_ref[...]
        w = w_ref[...]
        y = x * w[hist:hist + 1]
        for shift in range(1, conv_w):
            y = y + pltpu.roll(x, shift, 0) * w[hist - shift:hist - shift + 1]
        dst_ref[...] = _silu(y)
        buf_ref[i, pad:2 * pad, :] = x[0:pad]
        head = jnp.zeros((pad, width), F32)
        for j in range(conv_w):
            head = head + buf_ref[i, pad - hist + j:2 * pad - hist + j, :] * w[j:j + 1]
        dst_ref[0:pad, :] = _silu(head)
        buf_ref[i, pad - hist:pad, :] = x[t - hist:t]

    par = par_ref[...]
    lane = _iota((1, LANES), 1)
    consts = _gdn_consts(c)
    nw = nw_ref[...]

    def body(ci, carry):
        sl = pl.ds(pl.multiple_of(ci * c, c), c)
        bac = ba_ref[sl, :]
        bac = jnp.where(lane < heads, _sigmoid(bac), -jnp.exp(par[0:1]) * _softplus(bac + par[1:2]))
        column = lambda idx: jnp.sum(jnp.where(lane == idx, bac, 0.0), axis=-1, keepdims=True)
        operands = []
        for pi in range(hb // 2):
            local = (2 * pi, 2 * pi + 1)
            cols = [slice(h * LANES, (h + 1) * LANES) for h in local]
            stack = lambda ref: jnp.concatenate([ref[sl, cs] for cs in cols], axis=0)
            beta = jnp.concatenate([column(hg * hb + h) for h in local], axis=0)
            g = jnp.concatenate([column(heads + hg * hb + h) for h in local], axis=0)
            operands.append((stack(qs_ref), stack(ks_ref), stack(vs_ref), g, beta, [s_ref[h] for h in local],
                             [z_ref[sl, cs] for cs in cols]))
        chains = []
        for q, k, v, g, beta, states, _ in operands:
            q = q * lax.rsqrt(jnp.sum(q * q, axis=-1, keepdims=True) + 1e-12) * LANES ** -0.5
            k = k * lax.rsqrt(jnp.sum(k * k, axis=-1, keepdims=True) + 1e-12)
            chains.append(_gdn_pair_chunk(q, k, v, g, beta, states, consts))
        results = _lockstep(chains)
        for pi, (o, new_states) in enumerate(results):
            z = operands[pi][6]
            for i in range(2):
                h = 2 * pi + i
                s_ref[h] = new_states[i]
                oh = o[i * c:(i + 1) * c]
                ms = jnp.mean(oh * oh, axis=-1, keepdims=True)
                o_ref[sl, h * LANES:(h + 1) * LANES] = (
                    oh * lax.rsqrt(ms + NORM_EPS) * nw * _silu(z[i])).astype(o_ref.dtype)
        return carry

    lax.fori_loop(0, t // c, body, 0)

    @pl.when(l == pl.num_programs(2) - 1)
    def _():
        so_ref[...] = s_ref[...]


def _gdn(p, conv_w, conv_state, par, norm_w, s0, batch, seq, heads, tile):
    tile = min(tile, seq)
    nl = seq // tile
    cw = conv_w.shape[0]
    hb = GDN_HEADS_PER_STEP
    assert heads % hb == 0 and hb % 2 == 0
    groups = heads // hb
    width = hb * LANES
    pblk = lambda off: pl.BlockSpec((tile, width), lambda b, h, l, off=off: (b * nl + l, off * groups + h))
    wblk = lambda off: pl.BlockSpec((cw, width), lambda b, h, l, off=off: (0, off * groups + h))
    cblk = lambda off: pl.BlockSpec((None, cw - 1, width), lambda b, h, l, off=off: (b, 0, off * groups + h))
    st_spec = pl.BlockSpec((None, hb, LANES, LANES), lambda b, h, l: (b, h, 0, 0))
    tbuf = pltpu.VMEM((tile, width), F32)
    return pl.pallas_call(
        functools.partial(_gdn_kernel, heads=heads, conv_w=cw),
        grid=(batch, groups, nl),
        in_specs=[pblk(0), pblk(1), pblk(2), pblk(3),
                  pl.BlockSpec((tile, LANES), lambda b, h, l: (b * nl + l, 4 * heads)),
                  wblk(0), wblk(1), wblk(2), cblk(0), cblk(1), cblk(2),
                  pl.BlockSpec(par.shape, lambda b, h, l: (0, 0)),
                  pl.BlockSpec((1, LANES), lambda b, h, l: (0, 0)),
                  st_spec],
        out_specs=[pl.BlockSpec((tile, width), lambda b, h, l: (b * nl + l, h)), st_spec],
        out_shape=[jax.ShapeDtypeStruct((batch * seq, heads * LANES), BF16), jax.ShapeDtypeStruct(s0.shape, F32)],
        scratch_shapes=[pltpu.VMEM((hb, LANES, LANES), F32),
                        pltpu.VMEM((3, 16, width), F32),
                        tbuf, tbuf, tbuf],
        compiler_params=_params(("parallel", "parallel", "arbitrary")),
        name="gdn",
    )(p, p, p, p, p, conv_w, conv_w, conv_w, conv_state, conv_state, conv_state, par, norm_w, s0)


def _pad_cols(w, total):
    return jnp.pad(w, ((0, 0),) * (w.ndim - 1) + ((0, total - w.shape[-1]),))


def _ab_layout(a_cols, wb, lora_w, lora_a, lora_g):
    assert lora_w == LANES and lora_a == LANES and lora_g <= 512
    xg_off = -(-(3 * wb + 2 * LANES) // 512) * 512
    b_width = xg_off + 512
    assert a_cols % 512 == 0
    return xg_off, b_width


def _rwkv_cols(t, wb, xg_off, b_width, lora_g):
    head = t[..., :3 * wb + 2 * LANES]
    tail = t[..., 3 * wb + 2 * LANES:]
    z = lambda n: jnp.zeros(t.shape[:-1] + (n,), t.dtype)
    return jnp.concatenate([head, z(xg_off - head.shape[-1]), tail, z(b_width - xg_off - lora_g)], axis=-1)


def _trunk(x, mod, s_hgrn, s_rwkv, s_shift, s_gdn, s_conv, wts, tiles):
    batch, seq, d = x.shape
    m = batch * seq
    x2 = x.reshape(m, d)
    tm, t_hgrn, t_rwkv, t_gdn = tiles
    outs = {}
    depth = mod.shape[0]
    for layer in range(depth):
        j = layer // 2
        vecs6 = [mod[layer, :, i * d:(i + 1) * d].reshape(batch, 1, d) for i in range(6)]
        sh1, sc1, g1, sh2, sc2, g2 = vecs6
        if layer % 2 == 0:
            w = wts["ab"][j]
            p = _norm_matmul(x2, sc1, sh1, w["w_in"], seq, tm, 512)
            ha, wb = w["ha"], w["wb"]
            oa, st_a = _hgrn(p, w["lb"], w["hgrn_norm_w"], jnp.swapaxes(s_hgrn[j], -1, -2), batch, seq, ha, t_hgrn)
            pairs = wb // LANES
            hb = s_rwkv.shape[2]
            nb = s_rwkv.shape[-1]
            sp = s_rwkv[j].reshape(batch, pairs, 2, nb, nb)
            zero = jnp.zeros_like(sp[:, :, 0])
            s0 = jnp.concatenate([jnp.concatenate([sp[:, :, 0], zero], -1),
                                  jnp.concatenate([zero, sp[:, :, 1]], -1)], -2)
            shift = _rwkv_cols(s_shift[j], wb, w["xg_off"], w["b_width"], w["lora_g"])
            ob, st_b = _rwkv(p, shift, w["vecs"], w["mus"], w["w2"], w["a2"], w["g2"], s0,
                             batch, seq, pairs, w["a_cols"], t_rwkv)
            mix = jnp.concatenate([oa, ob], axis=-1)
            outs.setdefault("hgrn", []).append(jnp.swapaxes(st_a, -1, -2))
            outs.setdefault("rwkv", []).append(
                jnp.stack([st_b[:, :, :nb, :nb], st_b[:, :, nb:, nb:]], axis=2).reshape(batch, hb, nb, nb))
            last = p.reshape(batch, seq, -1)[:, seq - 1:, w["a_cols"]:]
            n_head = 3 * wb + 2 * LANES
            outs.setdefault("shift", []).append(
                jnp.concatenate([last[..., :n_head], last[..., w["xg_off"]:w["xg_off"] + w["lora_g"]]], axis=-1))
            w_out = w["w_out"]
        else:
            w = wts["gdn"][j]
            p = _norm_matmul(x2, sc1, sh1, w["w_in"], seq, tm, 512)
            hc = w["hc"]
            mix, st_c = _gdn(p, w["conv_w"], s_conv[j], w["par"], w["norm_w"], s_gdn[j], batch, seq, hc, t_gdn)
            outs.setdefault("gdn", []).append(st_c)
            cw = w["conv_w"].shape[0]
            raw = p.reshape(batch, seq, -1)[:, :, :3 * hc * LANES]
            prev = jnp.concatenate([s_conv[j], raw[:, max(seq - (cw - 1), 0):]], axis=1)
            outs.setdefault("conv", []).append(prev[:, -(cw - 1):])
            w_out = w["w_out"]
        x2 = _out_residual(mix, w_out, x2, g1, seq, tm, 512)
        x2 = _mlp(x2, sc2, sh2, g2, wts["mlp_w1"][layer], wts["mlp_w2"][layer], seq, min(tm, 512), 512, 256)
    x2 = _final_norm(x2, wts["final_w"], 256)
    st = lambda name: jnp.stack(outs[name])
    return x2.reshape(batch, seq, d), st("hgrn"), st("rwkv"), st("shift"), st("gdn"), st("conv")


def kernel(x_prompt, x_sample, c_prompt, c_sample, state_hgrn, state_rwkv, state_rwkv_shift, state_gdn,
           state_gdn_conv, ada_w, ada_b, mlp_w1, mlp_w2, final_norm_w, ab_w_in, ab_w_out, hgrn_lb_logits,
           hgrn_norm_w, rwkv_mu, rwkv_w0, rwkv_w2, rwkv_a0, rwkv_a2, rwkv_g2, rwkv_k_k, rwkv_k_a, rwkv_r_k,
           rwkv_ln_w, rwkv_ln_b, gdn_w_in, gdn_w_out, gdn_conv_w, gdn_a_log, gdn_dt_bias, gdn_norm_w):
    d = x_prompt.shape[-1]
    nb_p, nb_s = x_prompt.shape[0], x_sample.shape[0]
    n_ab, n_c = ab_w_in.shape[0], gdn_w_in.shape[0]
    ha = state_hgrn.shape[2]
    a_cols = 4 * ha * LANES
    wb = rwkv_w0.shape[1]
    lora_w, lora_a, lora_g = rwkv_w2.shape[1], rwkv_a2.shape[1], rwkv_g2.shape[1]
    xg_off, b_width = _ab_layout(a_cols, wb, lora_w, lora_a, lora_g)
    hc = state_gdn.shape[2]

    rows = nb_p + nb_s
    rows_pad = -(-rows // 8) * 8
    c_all = jnp.pad(jnp.concatenate([c_prompt, c_sample], axis=0), ((0, rows_pad - rows), (0, 0)))
    mod = _adaln(c_all, ada_w, ada_b)

    lbs = jnp.cumsum(jax.nn.softmax(hgrn_lb_logits.astype(F32), axis=0), axis=0)
    wts = {"ab": [], "gdn": [], "final_w": final_norm_w.reshape(1, d),
           "mlp_w1": [mlp_w1[i].astype(BF16) for i in range(mlp_w1.shape[0])],
           "mlp_w2": [mlp_w2[i].astype(BF16) for i in range(mlp_w2.shape[0])]}
    for j in range(n_ab):
        w_in = ab_w_in[j]
        w_in = jnp.concatenate([w_in[:, :a_cols], _rwkv_cols(w_in[:, a_cols:], wb, xg_off, b_width, lora_g)], axis=-1)
        mu = _rwkv_cols(rwkv_mu[j][None], wb, xg_off, b_width, lora_g)[0]
        vec_rows = [mu[:wb], mu[wb:2 * wb], mu[2 * wb:3 * wb], rwkv_w0[j], rwkv_a0[j], rwkv_k_k[j], rwkv_k_a[j],
                    rwkv_r_k[j].reshape(-1), rwkv_ln_w[j], rwkv_ln_b[j]]
        vecs = jnp.pad(jnp.stack(vec_rows), ((0, 16 - len(vec_rows)), (0, 0)))
        mus = jnp.stack([_pad_cols(mu[3 * wb:3 * wb + LANES], 512), _pad_cols(mu[3 * wb + LANES:3 * wb + 2 * LANES], 512),
                         mu[xg_off:xg_off + 512]])
        wts["ab"].append({
            "w_in": w_in.astype(BF16), "w_out": ab_w_out[j].astype(BF16), "lb": lbs[j][None],
            "hgrn_norm_w": hgrn_norm_w[j][None], "vecs": vecs, "mus": jnp.pad(mus, ((0, 5), (0, 0))),
            "w2": rwkv_w2[j].astype(BF16), "a2": rwkv_a2[j].astype(BF16),
            "g2": jnp.pad(rwkv_g2[j], ((0, 512 - lora_g), (0, 0))).astype(BF16),
            "ha": ha, "wb": wb, "a_cols": a_cols, "xg_off": xg_off, "b_width": b_width, "lora_g": lora_g})
    for j in range(n_c):
        cols = gdn_w_in.shape[-1]
        cols_pad = -(-cols // 512) * 512
        par = jnp.zeros((8, LANES), F32)
        par = par.at[0, hc:2 * hc].set(gdn_a_log[j]).at[1, hc:2 * hc].set(gdn_dt_bias[j])
        wts["gdn"].append({
            "w_in": _pad_cols(gdn_w_in[j], cols_pad).astype(BF16), "w_out": gdn_w_out[j].astype(BF16),
            "conv_w": gdn_conv_w[j], "par": par, "norm_w": gdn_norm_w[j][None], "hc": hc})

    zeros = lambda s: jnp.zeros((s.shape[0], nb_p) + s.shape[2:], x_prompt.dtype)
    y_p, hg_p, rw_p, sh_p, gd_p, cv_p = _trunk(
        x_prompt, mod[:, :nb_p], zeros(state_hgrn), zeros(state_rwkv), zeros(state_rwkv_shift),
        zeros(state_gdn), zeros(state_gdn_conv), wts, PROMPT_TILES)
    y_s, hg_s, rw_s, sh_s, gd_s, cv_s = _trunk(
        x_sample, mod[:, nb_p:rows], state_hgrn, state_rwkv, state_rwkv_shift, state_gdn, state_gdn_conv,
        wts, SAMPLE_TILES)
    return (y_p, y_s, hg_p, rw_p, sh_p, gd_p, cv_p, hg_s, rw_s, sh_s, gd_s, cv_s)
```

```python
import functools
import math

import jax
import jax.numpy as jnp
from jax import lax
from jax.experimental import pallas as pl
from jax.experimental.pallas import tpu as pltpu

F32 = jnp.float32
BF16 = jnp.bfloat16
HI = lax.Precision.HIGHEST

NORM_EPS = 1e-6
RWKV_LN_EPS = 64e-5
CHUNK = 64
DIAG = 8
LANES = 128
VMEM_LIMIT = 56 * 1024 * 1024
MLP_VMEM_LIMIT = 60 * 1024 * 1024
PROMPT_TILES = (1024, 512, 256, 256)
SAMPLE_TILES = (512, 64, 64, 64)
HGRN_HEADS_PER_STEP = 8
GDN_SOLVE_HI_LO = True
RWKV_PAIRS_PER_STEP = 8
GDN_HEADS_PER_STEP = 16


def _mm(a, b, ca, cb, exact=False):
    dims = (((ca,), (cb,)), ((), ()))
    if exact:
        return lax.dot_general(a, b, dims, precision=HI, preferred_element_type=F32)
    return lax.dot_general(a.astype(BF16), b.astype(BF16), dims, preferred_element_type=F32)


def _nn(a, b, exact=False):
    return _mm(a, b, 1, 0, exact)


def _nt(a, b, exact=False):
    return _mm(a, b, 1, 1, exact)


def _tn(a, b, exact=False):
    return _mm(a, b, 0, 0, exact)


def _sigmoid(x):
    return 1.0 / (1.0 + jnp.exp(-x))


def _silu(x):
    return x * _sigmoid(x)


def _softplus(x):
    return jnp.maximum(x, 0.0) + jnp.log(1.0 + jnp.exp(-jnp.abs(x)))


def _iota(shape, dim):
    return lax.broadcasted_iota(jnp.int32, shape, dim)


def _split2(x):
    hi = x.astype(BF16)
    return hi, (x - hi.astype(F32)).astype(BF16)


def _split3(x):
    hi = x.astype(BF16)
    r = x - hi.astype(F32)
    lo = r.astype(BF16)
    return hi, lo, (r - lo.astype(F32)).astype(BF16)


def _mm_hi(a, b, a_parts=None, b_parts=None):
    a_hi, a_lo = a_parts if a_parts is not None else _split2(a)
    b_hi, b_lo = b_parts if b_parts is not None else _split2(b)
    return jnp.dot(jnp.concatenate([a_hi, a_lo, a_hi], axis=1), jnp.concatenate([b_hi, b_hi, b_lo], axis=0),
                   preferred_element_type=F32)


def _mask_mm_left(mask3, x):
    return jnp.dot(mask3, jnp.concatenate(_split3(x), axis=0), preferred_element_type=F32)


def _mask_mm_right(x, mask3):
    return jnp.dot(jnp.concatenate(_split3(x), axis=1), mask3, preferred_element_type=F32)


def _nilpotent_solve(m, x, stages, hi_lo=True):
    n = m.shape[0]
    mm = _mm_hi if hi_lo else _nn
    for k in range(stages):
        last = k == stages - 1
        y = mm(m, x if last else jnp.concatenate([m, x], axis=1))
        yield
        if last:
            x = x + y
        else:
            m = y[:, :n]
            x = x + y[:, n:]
    return x


def _lockstep(generators):
    generators = list(generators)
    results = [None] * len(generators)
    live = list(range(len(generators)))
    while live:
        for i in list(live):
            try:
                next(generators[i])
            except StopIteration as stop:
                results[i] = stop.value
                live.remove(i)
    return results


def _params(sem, vmem=VMEM_LIMIT):
    return pltpu.CompilerParams(dimension_semantics=sem, vmem_limit_bytes=vmem)


def _rows_per_tile(rows_per_batch, tile):
    if tile <= rows_per_batch:
        assert rows_per_batch % tile == 0
        return 1
    assert tile % rows_per_batch == 0
    return tile // rows_per_batch


def _adaln_kernel(c_ref, w_ref, b_ref, o_ref):
    c = _silu(c_ref[...]).astype(BF16)
    o_ref[...] = jnp.dot(c, w_ref[...].astype(BF16), preferred_element_type=F32) + b_ref[...]


def _adaln(c, ada_w, ada_b):
    depth, d, n = ada_w.shape
    rows = c.shape[0]
    tn = 512 if n % 512 == 0 else n
    return pl.pallas_call(
        _adaln_kernel,
        grid=(depth, n // tn),
        in_specs=[pl.BlockSpec((rows, d), lambda l, j: (0, 0)),
                  pl.BlockSpec((None, d, tn), lambda l, j: (l, 0, j)),
                  pl.BlockSpec((None, 1, tn), lambda l, j: (l, 0, j))],
        out_specs=pl.BlockSpec((None, rows, tn), lambda l, j: (l, 0, j)),
        out_shape=jax.ShapeDtypeStruct((depth, rows, n), F32),
        compiler_params=_params(("parallel", "parallel")),
        name="adaln",
    )(c, ada_w, ada_b.reshape(depth, 1, n))


def _cast_cols_kernel(w_ref, o_ref, *, hole, shift, n_src):
    j = pl.program_id(0)
    src_col = jnp.where(j < hole[0], j, j - shift) * LANES + _iota((1, LANES), 1)
    keep = (src_col < n_src) & jnp.logical_not((j >= hole[0]) & (j < hole[1]))
    o_ref[...] = jnp.where(keep, w_ref[...], 0.0).astype(o_ref.dtype)


def _cast_cols(w, n_out, hole=(0, 0), shift=0):
    k, n_src = w.shape
    if hole[0] == hole[1]:
        hole = (n_out // LANES, n_out // LANES)
    last = (n_src - 1) // LANES
    return pl.pallas_call(
        functools.partial(_cast_cols_kernel, hole=hole, shift=shift, n_src=n_src),
        grid=(n_out // LANES,),
        in_specs=[pl.BlockSpec((k, LANES), lambda j: (0, jnp.clip(jnp.where(j < hole[0], j, j - shift), 0, last)))],
        out_specs=pl.BlockSpec((k, LANES), lambda j: (0, j)),
        out_shape=jax.ShapeDtypeStruct((k, n_out), BF16),
        compiler_params=_params(("parallel",)),
        name="cast_cols",
    )(w)


def _modulated_norm(x_ref, sc_ref, sh_ref, h_ref, groups):
    rows = x_ref.shape[0] // groups
    step = min(rows, 128)

    for gi in range(groups):
        sc = 1.0 + sc_ref[gi]
        sh = sh_ref[gi]

        def body(i, carry, gi=gi, sc=sc, sh=sh):
            sl = pl.ds(pl.multiple_of(gi * rows + i * step, step), step)
            x = x_ref[sl, :]
            ms = jnp.mean(x * x, axis=-1, keepdims=True)
            h_ref[sl, :] = (x * lax.rsqrt(ms + NORM_EPS) * sc + sh).astype(BF16)
            return carry

        lax.fori_loop(0, rows // step, body, 0)


def _norm_mm_kernel(x_ref, sc_ref, sh_ref, w_ref, o_ref, h_ref, *, groups):
    @pl.when(pl.program_id(1) == 0)
    def _():
        _modulated_norm(x_ref, sc_ref, sh_ref, h_ref, groups)

    o_ref[...] = jnp.dot(h_ref[...], w_ref[...], preferred_element_type=F32).astype(o_ref.dtype)


def _norm_matmul(x, scale, shift, w, rows_per_batch, tm, tn):
    m, d = x.shape
    n = w.shape[1]
    tm = min(tm, m)
    groups = _rows_per_tile(rows_per_batch, tm)
    bidx = (lambda i, j: (i * tm // rows_per_batch, 0, 0)) if groups == 1 else (lambda i, j: (i, 0, 0))
    return pl.pallas_call(
        functools.partial(_norm_mm_kernel, groups=groups),
        grid=(m // tm, n // tn),
        in_specs=[pl.BlockSpec((tm, d), lambda i, j: (i, 0), pipeline_mode=pl.Buffered(1)),
                  pl.BlockSpec((groups, 1, d), bidx),
                  pl.BlockSpec((groups, 1, d), bidx),
                  pl.BlockSpec((d, tn), lambda i, j: (0, j))],
        out_specs=pl.BlockSpec((tm, tn), lambda i, j: (i, j)),
        out_shape=jax.ShapeDtypeStruct((m, n), F32),
        scratch_shapes=[pltpu.VMEM((tm, d), BF16)],
        compiler_params=_params(("parallel", "arbitrary")),
        name="norm_matmul",
    )(x, scale, shift, w)


def _out_res_kernel(a_ref, w_ref, x_ref, g_ref, o_ref, *, groups):
    acc = jnp.dot(a_ref[...], w_ref[...], preferred_element_type=F32)
    rows = acc.shape[0] // groups
    for gi in range(groups):
        sl = slice(gi * rows, (gi + 1) * rows)
        o_ref[sl, :] = x_ref[sl, :] + g_ref[gi] * acc[sl, :]


def _out_residual(a, w, x, gate, rows_per_batch, tm, tn):
    m, k = a.shape
    n = w.shape[1]
    tm = min(tm, m)
    groups = _rows_per_tile(rows_per_batch, tm)
    bidx = (lambda i, j: (i * tm // rows_per_batch, 0, j)) if groups == 1 else (lambda i, j: (i, 0, j))
    return pl.pallas_call(
        functools.partial(_out_res_kernel, groups=groups),
        grid=(m // tm, n // tn),
        in_specs=[pl.BlockSpec((tm, k), lambda i, j: (i, 0)),
                  pl.BlockSpec((k, tn), lambda i, j: (0, j)),
                  pl.BlockSpec((tm, tn), lambda i, j: (i, j)),
                  pl.BlockSpec((groups, 1, tn), bidx)],
        out_specs=pl.BlockSpec((tm, tn), lambda i, j: (i, j)),
        out_shape=jax.ShapeDtypeStruct((m, n), F32),
        compiler_params=_params(("parallel", "parallel")),
        name="out_residual",
    )(a, w, x, gate)


def _mlp_kernel(x_ref, sc_ref, sh_ref, xr_ref, g_ref, w1_ref, w2_ref, o_ref, h_ref, u_ref, *, groups, nf):
    j = pl.program_id(1)

    @pl.when(j == 0)
    def _():
        _modulated_norm(x_ref, sc_ref, sh_ref, h_ref, groups)

    @pl.when(j < nf)
    def _():
        u = jnp.dot(h_ref[...], w1_ref[...], preferred_element_type=F32)
        u_ref[j] = jnp.square(jnp.maximum(u, 0.0)).astype(BF16)

    @pl.when(j >= nf)
    def _():
        tf = u_ref.shape[2]
        acc = jnp.dot(u_ref[0], w2_ref[0:tf, :], preferred_element_type=F32)
        for f in range(1, nf):
            acc += jnp.dot(u_ref[f], w2_ref[f * tf:(f + 1) * tf, :], preferred_element_type=F32)
        rows = acc.shape[0] // groups
        for gi in range(groups):
            sl = slice(gi * rows, (gi + 1) * rows)
            o_ref[sl, :] = xr_ref[sl, :] + g_ref[gi] * acc[sl, :]


def _mlp(x, scale, shift, gate, w1, w2, rows_per_batch, tm, tf, tn):
    m, d = x.shape
    ff = w1.shape[1]
    tm = min(tm, m)
    nf, nn = ff // tf, d // tn
    groups = _rows_per_tile(rows_per_batch, tm)
    batch_of = (lambda i: i * tm // rows_per_batch) if groups == 1 else (lambda i: i)
    vec = pl.BlockSpec((groups, 1, d), lambda i, j: (batch_of(i), 0, 0))
    col = lambda j: jnp.maximum(j - nf, 0)
    return pl.pallas_call(
        functools.partial(_mlp_kernel, groups=groups, nf=nf),
        grid=(m // tm, nf + nn),
        in_specs=[pl.BlockSpec((tm, d), lambda i, j: (i, 0), pipeline_mode=pl.Buffered(1)),
                  vec, vec,
                  pl.BlockSpec((tm, tn), lambda i, j: (i, col(j))),
                  pl.BlockSpec((groups, 1, tn), lambda i, j: (batch_of(i), 0, col(j))),
                  pl.BlockSpec((d, tf), lambda i, j: (0, jnp.minimum(j, nf - 1))),
                  pl.BlockSpec((ff, tn), lambda i, j: (0, col(j)))],
        out_specs=pl.BlockSpec((tm, tn), lambda i, j: (i, col(j))),
        out_shape=jax.ShapeDtypeStruct((m, d), F32),
        scratch_shapes=[pltpu.VMEM((tm, d), BF16), pltpu.VMEM((nf, tm, tf), BF16)],
        compiler_params=_params(("parallel", "arbitrary"), MLP_VMEM_LIMIT),
        name="mlp",
    )(x, scale, shift, x, gate, w1, w2)


def _final_norm_kernel(x_ref, w_ref, o_ref):
    x = x_ref[...]
    ms = jnp.mean(x * x, axis=-1, keepdims=True)
    o_ref[...] = x * lax.rsqrt(ms + NORM_EPS) * w_ref[...]


def _final_norm(x, w, tm):
    m, d = x.shape
    tm = min(tm, m)
    return pl.pallas_call(
        _final_norm_kernel,
        grid=(m // tm,),
        in_specs=[pl.BlockSpec((tm, d), lambda i: (i, 0)), pl.BlockSpec((1, d), lambda i: (0, 0))],
        out_specs=pl.BlockSpec((tm, d), lambda i: (i, 0)),
        out_shape=jax.ShapeDtypeStruct((m, d), F32),
        compiler_params=_params(("parallel",)),
        name="final_norm",
    )(x, w)


def _hgrn_consts(c):
    row = _iota((c, c), 0)
    col = _iota((c, c), 1)
    ltri = (_iota((c, 3 * c), 0) >= _iota((c, 3 * c), 1) % c).astype(BF16)
    levels = []
    gs = 2 * DIAG
    while gs <= c:
        half = gs // 2
        pair = (row // gs == col // gs) & (row % gs >= half) & (col % gs < half)
        levels.append((gs, pair))
        gs *= 2
    rix = _iota((c, 1), 0)
    return ltri, levels, rix


def _hgrn_chunk(q, k, v, g, st, consts):
    ltri, levels, rix = consts
    c = q.shape[0]
    gc = _mask_mm_left(ltri, g)
    yield
    lhs, rhs = [q * jnp.exp(gc)], [st]
    for gs, _ in levels:
        half = gs // 2
        ref = jnp.concatenate(
            [jnp.broadcast_to(gc[m0 * gs + half - 1:m0 * gs + half, :], (gs, gc.shape[1])) for m0 in range(c // gs)],
            axis=0)
        is_q = (rix % gs) >= half
        lhs.append(jnp.where(is_q, q * jnp.exp(gc - ref), 0.0))
        rhs.append(jnp.where(is_q, 0.0, k * jnp.exp(ref - gc)))
        rhs.append(jnp.zeros((LANES - c, k.shape[1]), F32))
    prod = _nt(jnp.concatenate(lhs, axis=0), jnp.concatenate(rhs, axis=0))
    gl = gc[c - 1:c, :]
    upd = _tn(v, k * jnp.exp(gl - gc))
    yield
    att = jnp.zeros((c, c), F32)
    for li, (_, pair) in enumerate(levels):
        att = att + jnp.where(pair, prod[(li + 1) * c:(li + 2) * c, (li + 1) * LANES:(li + 1) * LANES + c], 0.0)
    o = prod[:c, :LANES] + _nn(att, v)
    for dist in range(DIAG):
        if dist == 0:
            kd, gd, vd = k, gc, v
        else:
            kd = pltpu.roll(k, dist, 0)
            gd = pltpu.roll(gc, dist, 0)
            vd = pltpu.roll(v, dist, 0)
        w = jnp.sum(q * kd * jnp.exp(gc - gd), axis=-1, keepdims=True)
        o = o + jnp.where((rix % DIAG) >= dist, w, 0.0) * vd
    yield
    return o, st * jnp.exp(gl) + upd


def _hgrn_kernel(q_ref, f_ref, i_ref, g_ref, lb_ref, nw_ref, s0_ref, o_ref, so_ref, s_ref, *, dk):
    l = pl.program_id(2)

    @pl.when(l == 0)
    def _():
        s_ref[...] = s0_ref[...]

    t, width = q_ref.shape
    c = min(CHUNK, t)
    consts = _hgrn_consts(c)
    lb = lb_ref[...]
    nw = nw_ref[...]

    def body(ci, carry):
        sl = pl.ds(pl.multiple_of(ci * c, c), c)
        f = lb + (1.0 - lb) * _sigmoid(f_ref[sl, :])
        q = _silu(q_ref[sl, :]) * dk ** -0.5
        k = 1.0 - f
        v = i_ref[sl, :]
        g = jnp.log(f)
        gate = _silu(g_ref[sl, :])
        heads = [slice(h * dk, (h + 1) * dk) for h in range(width // dk)]
        results = _lockstep(_hgrn_chunk(q[:, hs], k[:, hs], v[:, hs], g[:, hs], s_ref[h], consts)
                            for h, hs in enumerate(heads))
        for h, (o, st) in enumerate(results):
            s_ref[h] = st
            ms = jnp.mean(o * o, axis=-1, keepdims=True)
            o_ref[sl, heads[h]] = (o * lax.rsqrt(ms + NORM_EPS) * nw * gate[:, heads[h]]).astype(o_ref.dtype)
        return carry

    lax.fori_loop(0, t // c, body, 0)

    @pl.when(l == pl.num_programs(2) - 1)
    def _():
        so_ref[...] = s_ref[...]


def _hgrn(p, lb, norm_w, s0t, batch, seq, heads, tile):
    dk = LANES
    tile = min(tile, seq)
    nl = seq // tile
    hb = HGRN_HEADS_PER_STEP
    assert heads % hb == 0
    groups = heads // hb
    width = hb * dk
    blk = lambda off: pl.BlockSpec((tile, width), lambda b, h, l, off=off: (b * nl + l, off * groups + h))
    st_spec = pl.BlockSpec((None, hb, dk, dk), lambda b, h, l: (b, h, 0, 0))
    return pl.pallas_call(
        functools.partial(_hgrn_kernel, dk=dk),
        grid=(batch, groups, nl),
        in_specs=[blk(0), blk(1), blk(2), blk(3),
                  pl.BlockSpec((1, width), lambda b, h, l: (0, h)),
                  pl.BlockSpec((1, dk), lambda b, h, l: (0, 0)),
                  st_spec],
        out_specs=[pl.BlockSpec((tile, width), lambda b, h, l: (b * nl + l, h)), st_spec],
        out_shape=[jax.ShapeDtypeStruct((batch * seq, heads * dk), BF16),
                   jax.ShapeDtypeStruct(s0t.shape, F32)],
        scratch_shapes=[pltpu.VMEM((hb, dk, dk), F32)],
        compiler_params=_params(("parallel", "parallel", "arbitrary")),
        name="hgrn2",
    )(p, p, p, p, lb, norm_w, s0t)


def _rwkv_consts(c):
    n = 4 * c
    row = _iota((n, n), 0)
    col = _iota((n, n), 1)
    same_head = ((row // c) % 2) == ((col // c) % 2)
    strict = (row % c) > (col % c)
    incl = (row % c) >= (col % c)
    keep = same_head & (strict | ((row >= 2 * c) & incl))
    ltri = (_iota((c, 3 * c), 0) >= _iota((c, 3 * c), 1) % c).astype(BF16)
    head0 = _iota((1, LANES), 1) < (LANES // 2)
    rs = _iota((LANES, LANES), 0)
    cs = _iota((LANES, LANES), 1)
    bdiag = (rs // (LANES // 2)) == (cs // (LANES // 2))
    return keep, ltri, head0, bdiag


def _rwkv_chunk(r, lg, k, v, a, b, s, consts):
    keep, ltri, head0, bdiag = consts
    c = r.shape[0]
    gc = _mask_mm_left(ltri, lg)
    yield
    e_incl = jnp.exp(gc)
    e_inv = jnp.exp(-gc)
    at = a * jnp.exp(gc - lg)
    rt = r * e_incl
    bt = b * e_inv
    kt = k * e_inv
    split = lambda x: [jnp.where(head0, x, 0.0), jnp.where(head0, 0.0, x)]
    lhs = jnp.concatenate(split(at) + split(rt), axis=0)
    rhs = jnp.concatenate([bt, bt, kt, kt, s], axis=0)
    pls = _nt(lhs, rhs)
    yield
    p = jnp.where(keep, pls[:, :4 * c], 0.0)
    ls = pls[:, 4 * c:]
    vst = jnp.concatenate(split(v), axis=0)
    pv = _nn(p[:, 2 * c:], vst)
    yield
    ust = yield from _nilpotent_solve(p[:2 * c, :2 * c], ls[:2 * c] + pv[:2 * c], int(math.log2(c)), hi_lo=False)
    ost = ls[2 * c:] + pv[2 * c:] + _nn(p[2 * c:, :2 * c], ust)
    yield
    o = ost[:c] + ost[c:]
    u = ust[:c] + ust[c:]
    gl = gc[c - 1:c, :]
    e_tail = jnp.exp(gl - gc)
    upd = _tn(jnp.concatenate([u, v], axis=0), jnp.concatenate([b * e_tail, k * e_tail], axis=0))
    yield
    return o, s * jnp.exp(gl) + jnp.where(bdiag, upd, 0.0)


def _rwkv_kernel(xr_ref, xk_ref, xv_ref, xw_ref, xa_ref, xg_ref,
                 hr_ref, hk_ref, hv_ref, hw_ref, ha_ref, hg_ref,
                 vec_ref, mus_ref, w2_ref, a2_ref, g2_ref, s0_ref,
                 o_ref, so_ref,
                 s_ref, buf_ref, bufs_ref, bufg_ref, r_ref, lg_ref, k_ref, v_ref, a_ref, b_ref, y_ref):
    l = pl.program_id(2)
    t, width = xr_ref.shape
    c = min(CHUNK, t)

    @pl.when(l == 0)
    def _():
        s_ref[...] = s0_ref[...]
        for i, h_ref in enumerate((hr_ref, hk_ref, hv_ref)):
            buf_ref[i, 0:1, :] = h_ref[...]
        for i, h_ref in enumerate((hw_ref, ha_ref)):
            bufs_ref[i, 0:1, :] = h_ref[...]
        bufg_ref[0:1, :] = hg_ref[...]

    first_row = _iota((t, 1), 0) == 0

    def shifted(carry, x_ref, mu):
        x = x_ref[...]
        prev = jnp.where(first_row, carry[0:1, :], pltpu.roll(x, 1, 0))
        carry[0:1, :] = x[t - 1:t, :]
        return x + (prev - x) * mu

    vec = vec_ref[...]
    mus = mus_ref[...]
    r = shifted(buf_ref.at[0], xr_ref, vec[0:1])
    k = shifted(buf_ref.at[1], xk_ref, vec[1:2])
    v = shifted(buf_ref.at[2], xv_ref, vec[2:3])
    xw = shifted(bufs_ref.at[0], xw_ref, mus[0:1, :LANES])
    xa = shifted(bufs_ref.at[1], xa_ref, mus[1:2, :LANES])
    xg = shifted(bufg_ref, xg_ref, mus[2:3])

    w0, a0, k_k, k_a, r_k, ln_w, ln_b = (vec[i:i + 1] for i in range(3, 10))
    half = LANES // 2
    ones2 = ((_iota((2 * LANES, LANES), 0) % LANES) // half == _iota((2 * LANES, LANES), 1) // half).astype(BF16)

    def head_sums(x):
        outs = []
        for blk in range(width // LANES):
            outs.append(jnp.dot(jnp.concatenate(_split2(x[:, blk * LANES:(blk + 1) * LANES]), axis=1), ones2,
                                preferred_element_type=F32))
        return jnp.concatenate(outs, axis=1)

    w_log = -_softplus(-(w0 + _nn(jnp.tanh(xw), w2_ref[...]))) - 0.5
    a_lr = _sigmoid(a0 + _nn(xa, a2_ref[...]))
    gate = _nn(_sigmoid(xg), g2_ref[...])
    kk = k * k_k
    kk = kk * lax.rsqrt(head_sums(kk * kk) + 1e-12)
    k = k * (1.0 + (a_lr - 1.0) * k_a)
    r_ref[...] = r
    lg_ref[...] = -jnp.exp(w_log)
    k_ref[...] = k
    v_ref[...] = v
    a_ref[...] = -kk
    b_ref[...] = kk * a_lr

    consts = _rwkv_consts(c)

    def body(ci, carry):
        sl = pl.ds(pl.multiple_of(ci * c, c), c)
        operands = []
        for pi in range(width // LANES):
            cols = slice(pi * LANES, (pi + 1) * LANES)
            operands.append((r_ref[sl, cols], lg_ref[sl, cols], k_ref[sl, cols], v_ref[sl, cols],
                             a_ref[sl, cols], b_ref[sl, cols], s_ref[pi]))
        results = _lockstep(_rwkv_chunk(*ops, consts) for ops in operands)
        for pi, (o, s_new) in enumerate(results):
            s_ref[pi] = s_new
            y_ref[sl, pi * LANES:(pi + 1) * LANES] = o
        return carry

    lax.fori_loop(0, t // c, body, 0)

    o = y_ref[...]
    inv_n = 1.0 / half
    mean = head_sums(o) * inv_n
    dev = o - mean
    var = head_sums(dev * dev) * inv_n
    o = dev * lax.rsqrt(var + RWKV_LN_EPS) * ln_w + ln_b
    o = o + head_sums(r * k * r_k) * v
    o_ref[...] = (o * gate).astype(o_ref.dtype)

    @pl.when(l == pl.num_programs(2) - 1)
    def _():
        so_ref[...] = s_ref[...]


def _rwkv(p, shift, vecs, mus, w2, a2, g2, s0, batch, seq, pairs, col0, tile):
    tile = min(tile, seq)
    nl = seq // tile
    wb = pairs * LANES
    pb = RWKV_PAIRS_PER_STEP
    width = pb * LANES
    groups = pairs // pb
    assert pairs % pb == 0 and col0 % width == 0
    cb0 = col0 // LANES
    gb0 = col0 // width
    xg_off = -(-(3 * wb + 2 * LANES) // 512) * 512
    assert (col0 + xg_off) % 512 == 0
    pblk = lambda off: pl.BlockSpec((tile, width), lambda b, h, l, off=off: (b * nl + l, gb0 + off * groups + h))
    pfix = lambda cb: pl.BlockSpec((tile, LANES), lambda b, h, l, cb=cb: (b * nl + l, cb))
    hblk = lambda off: pl.BlockSpec((None, 1, width), lambda b, h, l, off=off: (b, 0, off * groups + h))
    hfix = lambda cb: pl.BlockSpec((None, 1, LANES), lambda b, h, l, cb=cb: (b, 0, cb))
    st_spec = pl.BlockSpec((None, pb, LANES, LANES), lambda b, h, l: (b, h, 0, 0))
    col = lambda rows: pl.BlockSpec((rows, width), lambda b, h, l: (0, h))
    full = lambda a: pl.BlockSpec(a.shape, lambda b, h, l: (0,) * a.ndim)
    tbuf = pltpu.VMEM((tile, width), F32)
    return pl.pallas_call(
        _rwkv_kernel,
        grid=(batch, groups, nl),
        in_specs=[pblk(0), pblk(1), pblk(2), pfix(cb0 + 3 * pairs), pfix(cb0 + 3 * pairs + 1),
                  pl.BlockSpec((tile, 512), lambda b, h, l: (b * nl + l, (col0 + xg_off) // 512)),
                  hblk(0), hblk(1), hblk(2), hfix(3 * pairs), hfix(3 * pairs + 1),
                  pl.BlockSpec((None, 1, 512), lambda b, h, l: (b, 0, xg_off // 512)),
                  col(vecs.shape[0]), full(mus), col(w2.shape[0]), col(a2.shape[0]), col(g2.shape[0]), st_spec],
        out_specs=[pl.BlockSpec((tile, width), lambda b, h, l: (b * nl + l, h)), st_spec],
        out_shape=[jax.ShapeDtypeStruct((batch * seq, wb), BF16), jax.ShapeDtypeStruct(s0.shape, F32)],
        scratch_shapes=[pltpu.VMEM((pb, LANES, LANES), F32),
                        pltpu.VMEM((3, 8, width), F32),
                        pltpu.VMEM((2, 8, LANES), F32),
                        pltpu.VMEM((8, 512), F32),
                        tbuf, tbuf, tbuf, tbuf, tbuf, tbuf, tbuf],
        compiler_params=_params(("parallel", "parallel", "arbitrary")),
        name="rwkv7",
    )(p, p, p, p, p, p, shift, shift, shift, shift, shift, shift, vecs, mus, w2, a2, g2, s0)


def _gdn_consts(c):
    n = 2 * c
    row = _iota((n, n), 0)
    col = _iota((n, n), 1)
    same = (row // c) == (col // c)
    incl = same & ((row % c) >= (col % c))
    strict = same & ((row % c) > (col % c))
    r3 = _iota((n, 3 * n), 0)
    c3 = _iota((n, 3 * n), 1) % n
    ltri3 = (((r3 // c) == (c3 // c)) & ((r3 % c) >= (c3 % c))).astype(BF16)
    return ltri3, incl, strict, strict.astype(F32)


def _gdn_pair_chunk(q, k, v, g, beta, states, consts):
    ltri3, incl, strict, strict_f = consts
    n = q.shape[0]
    c = n // 2
    cum = _mask_mm_left(ltri3, jnp.concatenate([g * strict_f, jnp.broadcast_to(g, (n, LANES))], axis=1))
    yield
    gc = cum[:, n:]
    dec = jnp.exp(jnp.where(incl, cum[:, :n], -1e30))
    eg = jnp.exp(gc)
    kq = _nt(jnp.concatenate([k, q], axis=0), k)
    head0 = _iota((n, 1), 0) < c
    head0_2 = (_iota((2 * n, 1), 0) % n) < c
    by_head = lambda x, m: jnp.concatenate([jnp.where(m, x, 0.0), jnp.where(m, 0.0, x)], axis=1)
    ks = _nn(by_head(jnp.concatenate([k * eg, q * eg], axis=0), head0_2), jnp.concatenate(states, axis=0))
    yield
    a_mat = jnp.where(strict, beta * kq[:n] * dec, 0.0)
    v_new = yield from _nilpotent_solve(-a_mat, beta * (v - ks[:n]), int(math.log2(c)), hi_lo=GDN_SOLVE_HI_LO)
    o = ks[n:] + _nn(kq[n:] * dec, v_new)
    yield
    gl = jnp.where(head0, gc[c - 1:c, :], gc[n - 1:n, :])
    upd = _tn(k * jnp.exp(gl - gc), by_head(v_new, head0))
    yield
    new_states = [s * jnp.exp(gc[(hi + 1) * c - 1:(hi + 1) * c, :]) + upd[:, hi * LANES:(hi + 1) * LANES]
                  for hi, s in enumerate(states)]
    return o, new_states


def _gdn_kernel(q_ref, k_ref, v_ref, z_ref, ba_ref, cwq_ref, cwk_ref, cwv_ref, cq_ref, ck_ref, cv_ref,
                par_ref, nw_ref, s0_ref, o_ref, so_ref, s_ref, buf_ref, qs_ref, ks_ref, vs_ref,
                *, heads, conv_w):
    hg = pl.program_id(1)
    l = pl.program_id(2)
    t, width = q_ref.shape
    hb = width // LANES
    c = min(CHUNK, t)
    pad = 8
    hist = conv_w - 1
    streams = ((q_ref, cq_ref, cwq_ref, qs_ref), (k_ref, ck_ref, cwk_ref, ks_ref), (v_ref, cv_ref, cwv_ref, vs_ref))

    @pl.when(l == 0)
    def _():
        s_ref[...] = s0_ref[...]
        for i, (_, c_ref, _, _) in enumerate(streams):
            buf_ref[i, pad - hist:pad, :] = c_ref[...]

    for i, (x_ref, _, w_ref, dst_ref) in enumerate(streams):
        x = x_ref[...]
        w = w_ref[...]
        y = x * w[hist:hist + 1]
        for shift in range(1, conv_w):
            y = y + pltpu.roll(x, shift, 0) * w[hist - shift:hist - shift + 1]
        dst_ref[...] = _silu(y)
        buf_ref[i, pad:2 * pad, :] = x[0:pad]
        head = jnp.zeros((pad, width), F32)
        for j in range(conv_w):
            head = head + buf_ref[i, pad - hist + j:2 * pad - hist + j, :] * w[j:j + 1]
        dst_ref[0:pad, :] = _silu(head)
        buf_ref[i, pad - hist:pad, :] = x[t - hist:t]

    par = par_ref[...]
    lane = _iota((1, LANES), 1)
    consts = _gdn_consts(c)
    nw = nw_ref[...]

    def body(ci, carry):
        sl = pl.ds(pl.multiple_of(ci * c, c), c)
        bac = ba_ref[sl, :]
        bac = jnp.where(lane < heads, _sigmoid(bac), -jnp.exp(par[0:1]) * _softplus(bac + par[1:2]))
        column = lambda idx: jnp.sum(jnp.where(lane == idx, bac, 0.0), axis=-1, keepdims=True)
        operands = []
        for pi in range(hb // 2):
            local = (2 * pi, 2 * pi + 1)
            cols = [slice(h * LANES, (h + 1) * LANES) for h in local]
            stack = lambda ref: jnp.concatenate([ref[sl, cs] for cs in cols], axis=0)
            beta = jnp.concatenate([column(hg * hb + h) for h in local], axis=0)
            g = jnp.concatenate([column(heads + hg * hb + h) for h in local], axis=0)
            operands.append((stack(qs_ref), stack(ks_ref), stack(vs_ref), g, beta, [s_ref[h] for h in local],
                             [z_ref[sl, cs] for cs in cols]))
        chains = []
        for q, k, v, g, beta, states, _ in operands:
            q = q * lax.rsqrt(jnp.sum(q * q, axis=-1, keepdims=True) + 1e-12) * LANES ** -0.5
            k = k * lax.rsqrt(jnp.sum(k * k, axis=-1, keepdims=True) + 1e-12)
            chains.append(_gdn_pair_chunk(q, k, v, g, beta, states, consts))
        results = _lockstep(chains)
        for pi, (o, new_states) in enumerate(results):
            z = operands[pi][6]
            for i in range(2):
                h = 2 * pi + i
                s_ref[h] = new_states[i]
                oh = o[i * c:(i + 1) * c]
                ms = jnp.mean(oh * oh, axis=-1, keepdims=True)
                o_ref[sl, h * LANES:(h + 1) * LANES] = (
                    oh * lax.rsqrt(ms + NORM_EPS) * nw * _silu(z[i])).astype(o_ref.dtype)
        return carry

    lax.fori_loop(0, t // c, body, 0)

    @pl.when(l == pl.num_programs(2) - 1)
    def _():
        so_ref[...] = s_ref[...]


def _gdn(p, conv_w, conv_state, par, norm_w, s0, batch, seq, heads, tile):
    tile = min(tile, seq)
    nl = seq // tile
    cw = conv_w.shape[0]
    hb = GDN_HEADS_PER_STEP
    assert heads % hb == 0 and hb % 2 == 0
    groups = heads // hb
    width = hb * LANES
    pblk = lambda off: pl.BlockSpec((tile, width), lambda b, h, l, off=off: (b * nl + l, off * groups + h))
    wblk = lambda off: pl.BlockSpec((cw, width), lambda b, h, l, off=off: (0, off * groups + h))
    cblk = lambda off: pl.BlockSpec((None, cw - 1, width), lambda b, h, l, off=off: (b, 0, off * groups + h))
    st_spec = pl.BlockSpec((None, hb, LANES, LANES), lambda b, h, l: (b, h, 0, 0))
    tbuf = pltpu.VMEM((tile, width), F32)
    return pl.pallas_call(
        functools.partial(_gdn_kernel, heads=heads, conv_w=cw),
        grid=(batch, groups, nl),
        in_specs=[pblk(0), pblk(1), pblk(2), pblk(3),
                  pl.BlockSpec((tile, LANES), lambda b, h, l: (b * nl + l, 4 * heads)),
                  wblk(0), wblk(1), wblk(2), cblk(0), cblk(1), cblk(2),
                  pl.BlockSpec(par.shape, lambda b, h, l: (0, 0)),
                  pl.BlockSpec((1, LANES), lambda b, h, l: (0, 0)),
                  st_spec],
        out_specs=[pl.BlockSpec((tile, width), lambda b, h, l: (b * nl + l, h)), st_spec],
        out_shape=[jax.ShapeDtypeStruct((batch * seq, heads * LANES), BF16), jax.ShapeDtypeStruct(s0.shape, F32)],
        scratch_shapes=[pltpu.VMEM((hb, LANES, LANES), F32),
                        pltpu.VMEM((3, 16, width), F32),
                        tbuf, tbuf, tbuf],
        compiler_params=_params(("parallel", "parallel", "arbitrary")),
        name="gdn",
    )(p, p, p, p, p, conv_w, conv_w, conv_w, conv_state, conv_state, conv_state, par, norm_w, s0)


def _pad_cols(w, total):
    return jnp.pad(w, ((0, 0),) * (w.ndim - 1) + ((0, total - w.shape[-1]),))


def _ab_layout(a_cols, wb, lora_w, lora_a, lora_g):
    assert lora_w == LANES and lora_a == LANES and lora_g <= 512
    xg_off = -(-(3 * wb + 2 * LANES) // 512) * 512
    b_width = xg_off + 512
    assert a_cols % 512 == 0
    return xg_off, b_width


def _rwkv_cols(t, wb, xg_off, b_width, lora_g):
    head = t[..., :3 * wb + 2 * LANES]
    tail = t[..., 3 * wb + 2 * LANES:]
    z = lambda n: jnp.zeros(t.shape[:-1] + (n,), t.dtype)
    return jnp.concatenate([head, z(xg_off - head.shape[-1]), tail, z(b_width - xg_off - lora_g)], axis=-1)


def _trunk(x, mod, s_hgrn, s_rwkv, s_shift, s_gdn, s_conv, wts, tiles):
    batch, seq, d = x.shape
    m = batch * seq
    x2 = x.reshape(m, d)
    tm, t_hgrn, t_rwkv, t_gdn = tiles
    outs = {}
    depth = mod.shape[0]
    for layer in range(depth):
        j = layer // 2
        vecs6 = [mod[layer, :, i * d:(i + 1) * d].reshape(batch, 1, d) for i in range(6)]
        sh1, sc1, g1, sh2, sc2, g2 = vecs6
        if layer % 2 == 0:
            w = wts["ab"][j]
            p = _norm_matmul(x2, sc1, sh1, w["w_in"], seq, tm, 512)
            ha, wb = w["ha"], w["wb"]
            oa, st_a = _hgrn(p, w["lb"], w["hgrn_norm_w"], jnp.swapaxes(s_hgrn[j], -1, -2), batch, seq, ha, t_hgrn)
            pairs = wb // LANES
            hb = s_rwkv.shape[2]
            nb = s_rwkv.shape[-1]
            sp = s_rwkv[j].reshape(batch, pairs, 2, nb, nb)
            zero = jnp.zeros_like(sp[:, :, 0])
            s0 = jnp.concatenate([jnp.concatenate([sp[:, :, 0], zero], -1),
                                  jnp.concatenate([zero, sp[:, :, 1]], -1)], -2)
            shift = _rwkv_cols(s_shift[j], wb, w["xg_off"], w["b_width"], w["lora_g"])
            ob, st_b = _rwkv(p, shift, w["vecs"], w["mus"], w["w2"], w["a2"], w["g2"], s0,
                             batch, seq, pairs, w["a_cols"], t_rwkv)
            mix = jnp.concatenate([oa, ob], axis=-1)
            outs.setdefault("hgrn", []).append(jnp.swapaxes(st_a, -1, -2))
            outs.setdefault("rwkv", []).append(
                jnp.stack([st_b[:, :, :nb, :nb], st_b[:, :, nb:, nb:]], axis=2).reshape(batch, hb, nb, nb))
            last = p.reshape(batch, seq, -1)[:, seq - 1:, w["a_cols"]:]
            n_head = 3 * wb + 2 * LANES
            outs.setdefault("shift", []).append(
                jnp.concatenate([last[..., :n_head], last[..., w["xg_off"]:w["xg_off"] + w["lora_g"]]], axis=-1))
            w_out = w["w_out"]
        else:
            w = wts["gdn"][j]
            p = _norm_matmul(x2, sc1, sh1, w["w_in"], seq, tm, 512)
            hc = w["hc"]
            mix, st_c = _gdn(p, w["conv_w"], s_conv[j], w["par"], w["norm_w"], s_gdn[j], batch, seq, hc, t_gdn)
            outs.setdefault("gdn", []).append(st_c)
            cw = w["conv_w"].shape[0]
            raw = p.reshape(batch, seq, -1)[:, :, :3 * hc * LANES]
            prev = jnp.concatenate([s_conv[j], raw[:, max(seq - (cw - 1), 0):]], axis=1)
            outs.setdefault("conv", []).append(prev[:, -(cw - 1):])
            w_out = w["w_out"]
        x2 = _out_residual(mix, w_out, x2, g1, seq, tm, 512)
        x2 = _mlp(x2, sc2, sh2, g2, wts["mlp_w1"][layer], wts["mlp_w2"][layer], seq, min(tm, 512), 512, 256)
    x2 = _final_norm(x2, wts["final_w"], 256)
    st = lambda name: jnp.stack(outs[name])
    return x2.reshape(batch, seq, d), st("hgrn"), st("rwkv"), st("shift"), st("gdn"), st("conv")


def kernel(x_prompt, x_sample, c_prompt, c_sample, state_hgrn, state_rwkv, state_rwkv_shift, state_gdn,
           state_gdn_conv, ada_w, ada_b, mlp_w1, mlp_w2, final_norm_w, ab_w_in, ab_w_out, hgrn_lb_logits,
           hgrn_norm_w, rwkv_mu, rwkv_w0, rwkv_w2, rwkv_a0, rwkv_a2, rwkv_g2, rwkv_k_k, rwkv_k_a, rwkv_r_k,
           rwkv_ln_w, rwkv_ln_b, gdn_w_in, gdn_w_out, gdn_conv_w, gdn_a_log, gdn_dt_bias, gdn_norm_w):
    d = x_prompt.shape[-1]
    nb_p, nb_s = x_prompt.shape[0], x_sample.shape[0]
    n_ab, n_c = ab_w_in.shape[0], gdn_w_in.shape[0]
    ha = state_hgrn.shape[2]
    a_cols = 4 * ha * LANES
    wb = rwkv_w0.shape[1]
    lora_w, lora_a, lora_g = rwkv_w2.shape[1], rwkv_a2.shape[1], rwkv_g2.shape[1]
    xg_off, b_width = _ab_layout(a_cols, wb, lora_w, lora_a, lora_g)
    hc = state_gdn.shape[2]

    rows = nb_p + nb_s
    rows_pad = -(-rows // 8) * 8
    c_all = jnp.pad(jnp.concatenate([c_prompt, c_sample], axis=0), ((0, rows_pad - rows), (0, 0)))
    mod = _adaln(c_all, ada_w, ada_b)

    lbs = jnp.cumsum(jax.nn.softmax(hgrn_lb_logits.astype(F32), axis=0), axis=0)
    wts = {"ab": [], "gdn": [], "final_w": final_norm_w.reshape(1, d),
           "mlp_w1": [mlp_w1[i].astype(BF16) for i in range(mlp_w1.shape[0])],
           "mlp_w2": [mlp_w2[i].astype(BF16) for i in range(mlp_w2.shape[0])]}
    for j in range(n_ab):
        hole = ((a_cols + 3 * wb + 2 * LANES) // LANES, (a_cols + xg_off) // LANES)
        w_in = _cast_cols(ab_w_in[j], a_cols + b_width, hole, hole[1] - hole[0])
        mu = _rwkv_cols(rwkv_mu[j][None], wb, xg_off, b_width, lora_g)[0]
        vec_rows = [mu[:wb], mu[wb:2 * wb], mu[2 * wb:3 * wb], rwkv_w0[j], rwkv_a0[j], rwkv_k_k[j], rwkv_k_a[j],
                    rwkv_r_k[j].reshape(-1), rwkv_ln_w[j], rwkv_ln_b[j]]
        vecs = jnp.pad(jnp.stack(vec_rows), ((0, 16 - len(vec_rows)), (0, 0)))
        mus = jnp.stack([_pad_cols(mu[3 * wb:3 * wb + LANES], 512), _pad_cols(mu[3 * wb + LANES:3 * wb + 2 * LANES], 512),
                         mu[xg_off:xg_off + 512]])
        wts["ab"].append({
            "w_in": w_in, "w_out": ab_w_out[j].astype(BF16), "lb": lbs[j][None],
            "hgrn_norm_w": hgrn_norm_w[j][None], "vecs": vecs, "mus": jnp.pad(mus, ((0, 5), (0, 0))),
            "w2": rwkv_w2[j].astype(BF16), "a2": rwkv_a2[j].astype(BF16),
            "g2": jnp.pad(rwkv_g2[j], ((0, 512 - lora_g), (0, 0))).astype(BF16),
            "ha": ha, "wb": wb, "a_cols": a_cols, "xg_off": xg_off, "b_width": b_width, "lora_g": lora_g})
    for j in range(n_c):
        cols = gdn_w_in.shape[-1]
        cols_pad = -(-cols // 512) * 512
        par = jnp.zeros((8, LANES), F32)
        par = par.at[0, hc:2 * hc].set(gdn_a_log[j]).at[1, hc:2 * hc].set(gdn_dt_bias[j])
        wts["gdn"].append({
            "w_in": _cast_cols(gdn_w_in[j], cols_pad), "w_out": gdn_w_out[j].astype(BF16),
            "conv_w": gdn_conv_w[j], "par": par, "norm_w": gdn_norm_w[j][None], "hc": hc})

    zeros = lambda s: jnp.zeros((s.shape[0], nb_p) + s.shape[2:], x_prompt.dtype)
    y_p, hg_p, rw_p, sh_p, gd_p, cv_p = _trunk(
        x_prompt, mod[:, :nb_p], zeros(state_hgrn), zeros(state_rwkv), zeros(state_rwkv_shift),
        zeros(state_gdn), zeros(state_gdn_conv), wts, PROMPT_TILES)
    y_s, hg_s, rw_s, sh_s, gd_s, cv_s = _trunk(
        x_sample, mod[:, nb_p:rows], state_hgrn, state_rwkv, state_rwkv_shift, state_gdn, state_gdn_conv,
        wts, SAMPLE_TILES)
    return (y_p, y_s, hg_p, rw_p, sh_p, gd_p, cv_p, hg_s, rw_s, sh_s, gd_s, cv_s)
```

```python
import functools
import math

import jax
import jax.numpy as jnp
from jax import lax
from jax.experimental import pallas as pl
from jax.experimental.pallas import tpu as pltpu

F32 = jnp.float32
BF16 = jnp.bfloat16
HI = lax.Precision.HIGHEST

NORM_EPS = 1e-6
RWKV_LN_EPS = 64e-5
CHUNK = 64
DIAG = 8
LANES = 128
VMEM_LIMIT = 56 * 1024 * 1024
MLP_VMEM_LIMIT = 60 * 1024 * 1024
PROMPT_TILES = (1024, 512, 256, 256)
SAMPLE_TILES = (512, 64, 64, 64)
HGRN_HEADS_PER_STEP = 8
GDN_SOLVE_HI_LO = True
RWKV_PAIRS_PER_STEP = 8
GDN_HEADS_PER_STEP = 16


def _mm(a, b, ca, cb, exact=False):
    dims = (((ca,), (cb,)), ((), ()))
    if exact:
        return lax.dot_general(a, b, dims, precision=HI, preferred_element_type=F32)
    return lax.dot_general(a.astype(BF16), b.astype(BF16), dims, preferred_element_type=F32)


def _nn(a, b, exact=False):
    return _mm(a, b, 1, 0, exact)


def _nt(a, b, exact=False):
    return _mm(a, b, 1, 1, exact)


def _tn(a, b, exact=False):
    return _mm(a, b, 0, 0, exact)


def _sigmoid(x):
    return 1.0 / (1.0 + jnp.exp(-x))


def _silu(x):
    return x * _sigmoid(x)


def _softplus(x):
    return jnp.maximum(x, 0.0) + jnp.log(1.0 + jnp.exp(-jnp.abs(x)))


def _iota(shape, dim):
    return lax.broadcasted_iota(jnp.int32, shape, dim)


def _split2(x):
    hi = x.astype(BF16)
    return hi, (x - hi.astype(F32)).astype(BF16)


def _split3(x):
    hi = x.astype(BF16)
    r = x - hi.astype(F32)
    lo = r.astype(BF16)
    return hi, lo, (r - lo.astype(F32)).astype(BF16)


def _mm_hi(a, b, a_parts=None, b_parts=None):
    a_hi, a_lo = a_parts if a_parts is not None else _split2(a)
    b_hi, b_lo = b_parts if b_parts is not None else _split2(b)
    return jnp.dot(jnp.concatenate([a_hi, a_lo, a_hi], axis=1), jnp.concatenate([b_hi, b_hi, b_lo], axis=0),
                   preferred_element_type=F32)


def _mask_mm_left(mask3, x):
    return jnp.dot(mask3, jnp.concatenate(_split3(x), axis=0), preferred_element_type=F32)


def _mask_mm_right(x, mask3):
    return jnp.dot(jnp.concatenate(_split3(x), axis=1), mask3, preferred_element_type=F32)


def _nilpotent_solve(m, x, stages, hi_lo=True):
    n = m.shape[0]
    mm = _mm_hi if hi_lo else _nn
    for k in range(stages):
        last = k == stages - 1
        y = mm(m, x if last else jnp.concatenate([m, x], axis=1))
        yield
        if last:
            x = x + y
        else:
            m = y[:, :n]
            x = x + y[:, n:]
    return x


def _lockstep(generators):
    generators = list(generators)
    results = [None] * len(generators)
    live = list(range(len(generators)))
    while live:
        for i in list(live):
            try:
                next(generators[i])
            except StopIteration as stop:
                results[i] = stop.value
                live.remove(i)
    return results


def _params(sem, vmem=VMEM_LIMIT):
    return pltpu.CompilerParams(dimension_semantics=sem, vmem_limit_bytes=vmem)


def _rows_per_tile(rows_per_batch, tile):
    if tile <= rows_per_batch:
        assert rows_per_batch % tile == 0
        return 1
    assert tile % rows_per_batch == 0
    return tile // rows_per_batch


def _adaln_kernel(c_ref, w_ref, b_ref, o_ref):
    c = _silu(c_ref[...]).astype(BF16)
    o_ref[...] = jnp.dot(c, w_ref[...].astype(BF16), preferred_element_type=F32) + b_ref[...]


def _adaln(c, ada_w, ada_b):
    depth, d, n = ada_w.shape
    rows = c.shape[0]
    tn = 512 if n % 512 == 0 else n
    return pl.pallas_call(
        _adaln_kernel,
        grid=(depth, n // tn),
        in_specs=[pl.BlockSpec((rows, d), lambda l, j: (0, 0)),
                  pl.BlockSpec((None, d, tn), lambda l, j: (l, 0, j)),
                  pl.BlockSpec((None, 1, tn), lambda l, j: (l, 0, j))],
        out_specs=pl.BlockSpec((None, rows, tn), lambda l, j: (l, 0, j)),
        out_shape=jax.ShapeDtypeStruct((depth, rows, n), F32),
        compiler_params=_params(("parallel", "parallel")),
        name="adaln",
    )(c, ada_w, ada_b.reshape(depth, 1, n))


def _cast_cols_kernel(w_ref, o_ref, *, hole, shift, n_src):
    j = pl.program_id(0)
    src_col = jnp.where(j < hole[0], j, j - shift) * LANES + _iota((1, LANES), 1)
    keep = (src_col < n_src) & jnp.logical_not((j >= hole[0]) & (j < hole[1]))
    o_ref[...] = jnp.where(keep, w_ref[...], 0.0).astype(o_ref.dtype)


def _cast_cols(w, layer, n_out, hole=(0, 0), shift=0):
    _, k, n_src = w.shape
    if hole[0] == hole[1]:
        hole = (n_out // LANES, n_out // LANES)
    last = (n_src - 1) // LANES
    return pl.pallas_call(
        functools.partial(_cast_cols_kernel, hole=hole, shift=shift, n_src=n_src),
        grid=(n_out // LANES,),
        in_specs=[pl.BlockSpec((None, k, LANES),
                               lambda j: (layer, 0, jnp.clip(jnp.where(j < hole[0], j, j - shift), 0, last)))],
        out_specs=pl.BlockSpec((k, LANES), lambda j: (0, j)),
        out_shape=jax.ShapeDtypeStruct((k, n_out), BF16),
        compiler_params=_params(("parallel",)),
        name="cast_cols",
    )(w)


def _modulated_norm(x_ref, sc_ref, sh_ref, h_ref, groups):
    rows = x_ref.shape[0] // groups
    step = min(rows, 128)

    for gi in range(groups):
        sc = 1.0 + sc_ref[gi]
        sh = sh_ref[gi]

        def body(i, carry, gi=gi, sc=sc, sh=sh):
            sl = pl.ds(pl.multiple_of(gi * rows + i * step, step), step)
            x = x_ref[sl, :]
            ms = jnp.mean(x * x, axis=-1, keepdims=True)
            h_ref[sl, :] = (x * lax.rsqrt(ms + NORM_EPS) * sc + sh).astype(BF16)
            return carry

        lax.fori_loop(0, rows // step, body, 0)


def _norm_mm_kernel(x_ref, sc_ref, sh_ref, w_ref, o_ref, h_ref, *, groups):
    @pl.when(pl.program_id(1) == 0)
    def _():
        _modulated_norm(x_ref, sc_ref, sh_ref, h_ref, groups)

    o_ref[...] = jnp.dot(h_ref[...], w_ref[...], preferred_element_type=F32).astype(o_ref.dtype)


def _norm_matmul(x, scale, shift, w, rows_per_batch, tm, tn):
    m, d = x.shape
    n = w.shape[1]
    tm = min(tm, m)
    groups = _rows_per_tile(rows_per_batch, tm)
    bidx = (lambda i, j: (i * tm // rows_per_batch, 0, 0)) if groups == 1 else (lambda i, j: (i, 0, 0))
    return pl.pallas_call(
        functools.partial(_norm_mm_kernel, groups=groups),
        grid=(m // tm, n // tn),
        in_specs=[pl.BlockSpec((tm, d), lambda i, j: (i, 0), pipeline_mode=pl.Buffered(1)),
                  pl.BlockSpec((groups, 1, d), bidx),
                  pl.BlockSpec((groups, 1, d), bidx),
                  pl.BlockSpec((d, tn), lambda i, j: (0, j))],
        out_specs=pl.BlockSpec((tm, tn), lambda i, j: (i, j)),
        out_shape=jax.ShapeDtypeStruct((m, n), F32),
        scratch_shapes=[pltpu.VMEM((tm, d), BF16)],
        compiler_params=_params(("parallel", "arbitrary")),
        name="norm_matmul",
    )(x, scale, shift, w)


def _out_res_kernel(a_ref, w_ref, x_ref, g_ref, o_ref, *, groups):
    acc = jnp.dot(a_ref[...], w_ref[...], preferred_element_type=F32)
    rows = acc.shape[0] // groups
    for gi in range(groups):
        sl = slice(gi * rows, (gi + 1) * rows)
        o_ref[sl, :] = x_ref[sl, :] + g_ref[gi] * acc[sl, :]


def _out_residual(a, w, x, gate, rows_per_batch, tm, tn):
    m, k = a.shape
    n = w.shape[1]
    tm = min(tm, m)
    groups = _rows_per_tile(rows_per_batch, tm)
    bidx = (lambda i, j: (i * tm // rows_per_batch, 0, j)) if groups == 1 else (lambda i, j: (i, 0, j))
    return pl.pallas_call(
        functools.partial(_out_res_kernel, groups=groups),
        grid=(m // tm, n // tn),
        in_specs=[pl.BlockSpec((tm, k), lambda i, j: (i, 0)),
                  pl.BlockSpec((k, tn), lambda i, j: (0, j)),
                  pl.BlockSpec((tm, tn), lambda i, j: (i, j)),
                  pl.BlockSpec((groups, 1, tn), bidx)],
        out_specs=pl.BlockSpec((tm, tn), lambda i, j: (i, j)),
        out_shape=jax.ShapeDtypeStruct((m, n), F32),
        compiler_params=_params(("parallel", "parallel")),
        name="out_residual",
    )(a, w, x, gate)


def _mlp_kernel(x_ref, sc_ref, sh_ref, xr_ref, g_ref, w1_ref, w2_ref, o_ref, h_ref, u_ref, *, groups, nf):
    j = pl.program_id(1)

    @pl.when(j == 0)
    def _():
        _modulated_norm(x_ref, sc_ref, sh_ref, h_ref, groups)

    @pl.when(j < nf)
    def _():
        u = jnp.dot(h_ref[...], w1_ref[...], preferred_element_type=F32)
        u_ref[j] = jnp.square(jnp.maximum(u, 0.0)).astype(BF16)

    @pl.when(j >= nf)
    def _():
        tf = u_ref.shape[2]
        acc = jnp.dot(u_ref[0], w2_ref[0:tf, :], preferred_element_type=F32)
        for f in range(1, nf):
            acc += jnp.dot(u_ref[f], w2_ref[f * tf:(f + 1) * tf, :], preferred_element_type=F32)
        rows = acc.shape[0] // groups
        for gi in range(groups):
            sl = slice(gi * rows, (gi + 1) * rows)
            o_ref[sl, :] = xr_ref[sl, :] + g_ref[gi] * acc[sl, :]


def _mlp(x, scale, shift, gate, w1, w2, layer, rows_per_batch, tm, tf, tn):
    m, d = x.shape
    ff = w1.shape[2]
    tm = min(tm, m)
    nf, nn = ff // tf, d // tn
    groups = _rows_per_tile(rows_per_batch, tm)
    batch_of = (lambda i: i * tm // rows_per_batch) if groups == 1 else (lambda i: i)
    vec = pl.BlockSpec((groups, 1, d), lambda i, j: (batch_of(i), 0, 0))
    col = lambda j: jnp.maximum(j - nf, 0)
    return pl.pallas_call(
        functools.partial(_mlp_kernel, groups=groups, nf=nf),
        grid=(m // tm, nf + nn),
        in_specs=[pl.BlockSpec((tm, d), lambda i, j: (i, 0), pipeline_mode=pl.Buffered(1)),
                  vec, vec,
                  pl.BlockSpec((tm, tn), lambda i, j: (i, col(j))),
                  pl.BlockSpec((groups, 1, tn), lambda i, j: (batch_of(i), 0, col(j))),
                  pl.BlockSpec((None, d, tf), lambda i, j: (layer, 0, jnp.minimum(j, nf - 1))),
                  pl.BlockSpec((None, ff, tn), lambda i, j: (layer, 0, col(j)))],
        out_specs=pl.BlockSpec((tm, tn), lambda i, j: (i, col(j))),
        out_shape=jax.ShapeDtypeStruct((m, d), F32),
        scratch_shapes=[pltpu.VMEM((tm, d), BF16), pltpu.VMEM((nf, tm, tf), BF16)],
        compiler_params=_params(("parallel", "arbitrary"), MLP_VMEM_LIMIT),
        name="mlp",
    )(x, scale, shift, x, gate, w1, w2)


def _final_norm_kernel(x_ref, w_ref, o_ref):
    x = x_ref[...]
    ms = jnp.mean(x * x, axis=-1, keepdims=True)
    o_ref[...] = x * lax.rsqrt(ms + NORM_EPS) * w_ref[...]


def _final_norm(x, w, tm):
    m, d = x.shape
    tm = min(tm, m)
    return pl.pallas_call(
        _final_norm_kernel,
        grid=(m // tm,),
        in_specs=[pl.BlockSpec((tm, d), lambda i: (i, 0)), pl.BlockSpec((1, d), lambda i: (0, 0))],
        out_specs=pl.BlockSpec((tm, d), lambda i: (i, 0)),
        out_shape=jax.ShapeDtypeStruct((m, d), F32),
        compiler_params=_params(("parallel",)),
        name="final_norm",
    )(x, w)


def _hgrn_consts(c):
    row = _iota((c, c), 0)
    col = _iota((c, c), 1)
    ltri = (_iota((c, 3 * c), 0) >= _iota((c, 3 * c), 1) % c).astype(BF16)
    levels = []
    gs = 2 * DIAG
    while gs <= c:
        half = gs // 2
        pair = (row // gs == col // gs) & (row % gs >= half) & (col % gs < half)
        levels.append((gs, pair))
        gs *= 2
    rix = _iota((c, 1), 0)
    return ltri, levels, rix


def _hgrn_chunk(q, k, v, g, st, consts):
    ltri, levels, rix = consts
    c = q.shape[0]
    gc = _mask_mm_left(ltri, g)
    yield
    lhs, rhs = [q * jnp.exp(gc)], [st]
    for gs, _ in levels:
        half = gs // 2
        ref = jnp.concatenate(
            [jnp.broadcast_to(gc[m0 * gs + half - 1:m0 * gs + half, :], (gs, gc.shape[1])) for m0 in range(c // gs)],
            axis=0)
        is_q = (rix % gs) >= half
        lhs.append(jnp.where(is_q, q * jnp.exp(gc - ref), 0.0))
        rhs.append(jnp.where(is_q, 0.0, k * jnp.exp(ref - gc)))
        rhs.append(jnp.zeros((LANES - c, k.shape[1]), F32))
    prod = _nt(jnp.concatenate(lhs, axis=0), jnp.concatenate(rhs, axis=0))
    gl = gc[c - 1:c, :]
    upd = _tn(v, k * jnp.exp(gl - gc))
    yield
    att = jnp.zeros((c, c), F32)
    for li, (_, pair) in enumerate(levels):
        att = att + jnp.where(pair, prod[(li + 1) * c:(li + 2) * c, (li + 1) * LANES:(li + 1) * LANES + c], 0.0)
    o = prod[:c, :LANES] + _nn(att, v)
    for dist in range(DIAG):
        if dist == 0:
            kd, gd, vd = k, gc, v
        else:
            kd = pltpu.roll(k, dist, 0)
            gd = pltpu.roll(gc, dist, 0)
            vd = pltpu.roll(v, dist, 0)
        w = jnp.sum(q * kd * jnp.exp(gc - gd), axis=-1, keepdims=True)
        o = o + jnp.where((rix % DIAG) >= dist, w, 0.0) * vd
    yield
    return o, st * jnp.exp(gl) + upd


def _hgrn_kernel(q_ref, f_ref, i_ref, g_ref, lb_ref, nw_ref, s0_ref, o_ref, so_ref, s_ref, *, dk):
    l = pl.program_id(2)

    @pl.when(l == 0)
    def _():
        s_ref[...] = s0_ref[...]

    t, width = q_ref.shape
    c = min(CHUNK, t)
    consts = _hgrn_consts(c)
    lb = lb_ref[...]
    nw = nw_ref[...]

    def body(ci, carry):
        sl = pl.ds(pl.multiple_of(ci * c, c), c)
        f = lb + (1.0 - lb) * _sigmoid(f_ref[sl, :])
        q = _silu(q_ref[sl, :]) * dk ** -0.5
        k = 1.0 - f
        v = i_ref[sl, :]
        g = jnp.log(f)
        gate = _silu(g_ref[sl, :])
        heads = [slice(h * dk, (h + 1) * dk) for h in range(width // dk)]
        results = _lockstep(_hgrn_chunk(q[:, hs], k[:, hs], v[:, hs], g[:, hs], s_ref[h], consts)
                            for h, hs in enumerate(heads))
        for h, (o, st) in enumerate(results):
            s_ref[h] = st
            ms = jnp.mean(o * o, axis=-1, keepdims=True)
            o_ref[sl, heads[h]] = (o * lax.rsqrt(ms + NORM_EPS) * nw * gate[:, heads[h]]).astype(o_ref.dtype)
        return carry

    lax.fori_loop(0, t // c, body, 0)

    @pl.when(l == pl.num_programs(2) - 1)
    def _():
        so_ref[...] = s_ref[...]


def _hgrn(p, lb, norm_w, s0t, batch, seq, heads, tile):
    dk = LANES
    tile = min(tile, seq)
    nl = seq // tile
    hb = HGRN_HEADS_PER_STEP
    assert heads % hb == 0
    groups = heads // hb
    width = hb * dk
    blk = lambda off: pl.BlockSpec((tile, width), lambda b, h, l, off=off: (b * nl + l, off * groups + h))
    st_spec = pl.BlockSpec((None, hb, dk, dk), lambda b, h, l: (b, h, 0, 0))
    return pl.pallas_call(
        functools.partial(_hgrn_kernel, dk=dk),
        grid=(batch, groups, nl),
        in_specs=[blk(0), blk(1), blk(2), blk(3),
                  pl.BlockSpec((1, width), lambda b, h, l: (0, h)),
                  pl.BlockSpec((1, dk), lambda b, h, l: (0, 0)),
                  st_spec],
        out_specs=[pl.BlockSpec((tile, width), lambda b, h, l: (b * nl + l, h)), st_spec],
        out_shape=[jax.ShapeDtypeStruct((batch * seq, heads * dk), BF16),
                   jax.ShapeDtypeStruct(s0t.shape, F32)],
        scratch_shapes=[pltpu.VMEM((hb, dk, dk), F32)],
        compiler_params=_params(("parallel", "parallel", "arbitrary")),
        name="hgrn2",
    )(p, p, p, p, lb, norm_w, s0t)


def _rwkv_consts(c):
    n = 4 * c
    row = _iota((n, n), 0)
    col = _iota((n, n), 1)
    same_head = ((row // c) % 2) == ((col // c) % 2)
    strict = (row % c) > (col % c)
    incl = (row % c) >= (col % c)
    keep = same_head & (strict | ((row >= 2 * c) & incl))
    ltri = (_iota((c, 3 * c), 0) >= _iota((c, 3 * c), 1) % c).astype(BF16)
    head0 = _iota((1, LANES), 1) < (LANES // 2)
    rs = _iota((LANES, LANES), 0)
    cs = _iota((LANES, LANES), 1)
    bdiag = (rs // (LANES // 2)) == (cs // (LANES // 2))
    return keep, ltri, head0, bdiag


def _rwkv_chunk(r, lg, k, v, a, b, s, consts):
    keep, ltri, head0, bdiag = consts
    c = r.shape[0]
    gc = _mask_mm_left(ltri, lg)
    yield
    e_incl = jnp.exp(gc)
    e_inv = jnp.exp(-gc)
    at = a * jnp.exp(gc - lg)
    rt = r * e_incl
    bt = b * e_inv
    kt = k * e_inv
    split = lambda x: [jnp.where(head0, x, 0.0), jnp.where(head0, 0.0, x)]
    lhs = jnp.concatenate(split(at) + split(rt), axis=0)
    rhs = jnp.concatenate([bt, bt, kt, kt, s], axis=0)
    pls = _nt(lhs, rhs)
    yield
    p = jnp.where(keep, pls[:, :4 * c], 0.0)
    ls = pls[:, 4 * c:]
    vst = jnp.concatenate(split(v), axis=0)
    pv = _nn(p[:, 2 * c:], vst)
    yield
    ust = yield from _nilpotent_solve(p[:2 * c, :2 * c], ls[:2 * c] + pv[:2 * c], int(math.log2(c)), hi_lo=False)
    ost = ls[2 * c:] + pv[2 * c:] + _nn(p[2 * c:, :2 * c], ust)
    yield
    o = ost[:c] + ost[c:]
    u = ust[:c] + ust[c:]
    gl = gc[c - 1:c, :]
    e_tail = jnp.exp(gl - gc)
    upd = _tn(jnp.concatenate([u, v], axis=0), jnp.concatenate([b * e_tail, k * e_tail], axis=0))
    yield
    return o, s * jnp.exp(gl) + jnp.where(bdiag, upd, 0.0)


def _rwkv_kernel(xr_ref, xk_ref, xv_ref, xw_ref, xa_ref, xg_ref,
                 hr_ref, hk_ref, hv_ref, hw_ref, ha_ref, hg_ref,
                 vec_ref, mus_ref, w2_ref, a2_ref, g2_ref, s0_ref,
                 o_ref, so_ref,
                 s_ref, buf_ref, bufs_ref, bufg_ref, r_ref, lg_ref, k_ref, v_ref, a_ref, b_ref, y_ref):
    l = pl.program_id(2)
    t, width = xr_ref.shape
    c = min(CHUNK, t)

    @pl.when(l == 0)
    def _():
        s_ref[...] = s0_ref[...]
        for i, h_ref in enumerate((hr_ref, hk_ref, hv_ref)):
            buf_ref[i, 0:1, :] = h_ref[...]
        for i, h_ref in enumerate((hw_ref, ha_ref)):
            bufs_ref[i, 0:1, :] = h_ref[...]
        bufg_ref[0:1, :] = hg_ref[...]

    first_row = _iota((t, 1), 0) == 0

    def shifted(carry, x_ref, mu):
        x = x_ref[...]
        prev = jnp.where(first_row, carry[0:1, :], pltpu.roll(x, 1, 0))
        carry[0:1, :] = x[t - 1:t, :]
        return x + (prev - x) * mu

    vec = vec_ref[...]
    mus = mus_ref[...]
    r = shifted(buf_ref.at[0], xr_ref, vec[0:1])
    k = shifted(buf_ref.at[1], xk_ref, vec[1:2])
    v = shifted(buf_ref.at[2], xv_ref, vec[2:3])
    xw = shifted(bufs_ref.at[0], xw_ref, mus[0:1, :LANES])
    xa = shifted(bufs_ref.at[1], xa_ref, mus[1:2, :LANES])
    xg = shifted(bufg_ref, xg_ref, mus[2:3])

    w0, a0, k_k, k_a, r_k, ln_w, ln_b = (vec[i:i + 1] for i in range(3, 10))
    half = LANES // 2
    ones2 = ((_iota((2 * LANES, LANES), 0) % LANES) // half == _iota((2 * LANES, LANES), 1) // half).astype(BF16)

    def head_sums(x):
        outs = []
        for blk in range(width // LANES):
            outs.append(jnp.dot(jnp.concatenate(_split2(x[:, blk * LANES:(blk + 1) * LANES]), axis=1), ones2,
                                preferred_element_type=F32))
        return jnp.concatenate(outs, axis=1)

    w_log = -_softplus(-(w0 + _nn(jnp.tanh(xw), w2_ref[...]))) - 0.5
    a_lr = _sigmoid(a0 + _nn(xa, a2_ref[...]))
    gate = _nn(_sigmoid(xg), g2_ref[...])
    kk = k * k_k
    kk = kk * lax.rsqrt(head_sums(kk * kk) + 1e-12)
    k = k * (1.0 + (a_lr - 1.0) * k_a)
    r_ref[...] = r
    lg_ref[...] = -jnp.exp(w_log)
    k_ref[...] = k
    v_ref[...] = v
    a_ref[...] = -kk
    b_ref[...] = kk * a_lr

    consts = _rwkv_consts(c)

    def body(ci, carry):
        sl = pl.ds(pl.multiple_of(ci * c, c), c)
        operands = []
        for pi in range(width // LANES):
            cols = slice(pi * LANES, (pi + 1) * LANES)
            operands.append((r_ref[sl, cols], lg_ref[sl, cols], k_ref[sl, cols], v_ref[sl, cols],
                             a_ref[sl, cols], b_ref[sl, cols], s_ref[pi]))
        results = _lockstep(_rwkv_chunk(*ops, consts) for ops in operands)
        for pi, (o, s_new) in enumerate(results):
            s_ref[pi] = s_new
            y_ref[sl, pi * LANES:(pi + 1) * LANES] = o
        return carry

    lax.fori_loop(0, t // c, body, 0)

    o = y_ref[...]
    inv_n = 1.0 / half
    mean = head_sums(o) * inv_n
    dev = o - mean
    var = head_sums(dev * dev) * inv_n
    o = dev * lax.rsqrt(var + RWKV_LN_EPS) * ln_w + ln_b
    o = o + head_sums(r * k * r_k) * v
    o_ref[...] = (o * gate).astype(o_ref.dtype)

    @pl.when(l == pl.num_programs(2) - 1)
    def _():
        so_ref[...] = s_ref[...]


def _rwkv(p, shift, vecs, mus, w2, a2, g2, s0, batch, seq, pairs, col0, tile):
    tile = min(tile, seq)
    nl = seq // tile
    wb = pairs * LANES
    pb = RWKV_PAIRS_PER_STEP
    width = pb * LANES
    groups = pairs // pb
    assert pairs % pb == 0 and col0 % width == 0
    cb0 = col0 // LANES
    gb0 = col0 // width
    xg_off = -(-(3 * wb + 2 * LANES) // 512) * 512
    assert (col0 + xg_off) % 512 == 0
    pblk = lambda off: pl.BlockSpec((tile, width), lambda b, h, l, off=off: (b * nl + l, gb0 + off * groups + h))
    pfix = lambda cb: pl.BlockSpec((tile, LANES), lambda b, h, l, cb=cb: (b * nl + l, cb))
    hblk = lambda off: pl.BlockSpec((None, 1, width), lambda b, h, l, off=off: (b, 0, off * groups + h))
    hfix = lambda cb: pl.BlockSpec((None, 1, LANES), lambda b, h, l, cb=cb: (b, 0, cb))
    st_spec = pl.BlockSpec((None, pb, LANES, LANES), lambda b, h, l: (b, h, 0, 0))
    col = lambda rows: pl.BlockSpec((rows, width), lambda b, h, l: (0, h))
    full = lambda a: pl.BlockSpec(a.shape, lambda b, h, l: (0,) * a.ndim)
    tbuf = pltpu.VMEM((tile, width), F32)
    return pl.pallas_call(
        _rwkv_kernel,
        grid=(batch, groups, nl),
        in_specs=[pblk(0), pblk(1), pblk(2), pfix(cb0 + 3 * pairs), pfix(cb0 + 3 * pairs + 1),
                  pl.BlockSpec((tile, 512), lambda b, h, l: (b * nl + l, (col0 + xg_off) // 512)),
                  hblk(0), hblk(1), hblk(2), hfix(3 * pairs), hfix(3 * pairs + 1),
                  pl.BlockSpec((None, 1, 512), lambda b, h, l: (b, 0, xg_off // 512)),
                  col(vecs.shape[0]), full(mus), col(w2.shape[0]), col(a2.shape[0]), col(g2.shape[0]), st_spec],
        out_specs=[pl.BlockSpec((tile, width), lambda b, h, l: (b * nl + l, h)), st_spec],
        out_shape=[jax.ShapeDtypeStruct((batch * seq, wb), BF16), jax.ShapeDtypeStruct(s0.shape, F32)],
        scratch_shapes=[pltpu.VMEM((pb, LANES, LANES), F32),
                        pltpu.VMEM((3, 8, width), F32),
                        pltpu.VMEM((2, 8, LANES), F32),
                        pltpu.VMEM((8, 512), F32),
                        tbuf, tbuf, tbuf, tbuf, tbuf, tbuf, tbuf],
        compiler_params=_params(("parallel", "parallel", "arbitrary")),
        name="rwkv7",
    )(p, p, p, p, p, p, shift, shift, shift, shift, shift, shift, vecs, mus, w2, a2, g2, s0)


def _gdn_consts(c):
    n = 2 * c
    row = _iota((n, n), 0)
    col = _iota((n, n), 1)
    same = (row // c) == (col // c)
    incl = same & ((row % c) >= (col % c))
    strict = same & ((row % c) > (col % c))
    r3 = _iota((n, 3 * n), 0)
    c3 = _iota((n, 3 * n), 1) % n
    ltri3 = (((r3 // c) == (c3 // c)) & ((r3 % c) >= (c3 % c))).astype(BF16)
    return ltri3, incl, strict, strict.astype(F32)


def _gdn_pair_chunk(q, k, v, g, beta, states, consts):
    ltri3, incl, strict, strict_f = consts
    n = q.shape[0]
    c = n // 2
    cum = _mask_mm_left(ltri3, jnp.concatenate([g * strict_f, jnp.broadcast_to(g, (n, LANES))], axis=1))
    yield
    gc = cum[:, n:]
    dec = jnp.exp(jnp.where(incl, cum[:, :n], -1e30))
    eg = jnp.exp(gc)
    kq = _nt(jnp.concatenate([k, q], axis=0), k)
    head0 = _iota((n, 1), 0) < c
    head0_2 = (_iota((2 * n, 1), 0) % n) < c
    by_head = lambda x, m: jnp.concatenate([jnp.where(m, x, 0.0), jnp.where(m, 0.0, x)], axis=1)
    ks = _nn(by_head(jnp.concatenate([k * eg, q * eg], axis=0), head0_2), jnp.concatenate(states, axis=0))
    yield
    a_mat = jnp.where(strict, beta * kq[:n] * dec, 0.0)
    v_new = yield from _nilpotent_solve(-a_mat, beta * (v - ks[:n]), int(math.log2(c)), hi_lo=GDN_SOLVE_HI_LO)
    o = ks[n:] + _nn(kq[n:] * dec, v_new)
    yield
    gl = jnp.where(head0, gc[c - 1:c, :], gc[n - 1:n, :])
    upd = _tn(k * jnp.exp(gl - gc), by_head(v_new, head0))
    yield
    new_states = [s * jnp.exp(gc[(hi + 1) * c - 1:(hi + 1) * c, :]) + upd[:, hi * LANES:(hi + 1) * LANES]
                  for hi, s in enumerate(states)]
    return o, new_states


def _gdn_kernel(q_ref, k_ref, v_ref, z_ref, ba_ref, cwq_ref, cwk_ref, cwv_ref, cq_ref, ck_ref, cv_ref,
                par_ref, nw_ref, s0_ref, o_ref, so_ref, s_ref, buf_ref, qs_ref, ks_ref, vs_ref,
                *, heads, conv_w):
    hg = pl.program_id(1)
    l = pl.program_id(2)
    t, width = q_ref.shape
    hb = width // LANES
    c = min(CHUNK, t)
    pad = 8
    hist = conv_w - 1
    streams = ((q_ref, cq_ref, cwq_ref, qs_ref), (k_ref, ck_ref, cwk_ref, ks_ref), (v_ref, cv_ref, cwv_ref, vs_ref))

    @pl.when(l == 0)
    def _():
        s_ref[...] = s0_ref[...]
        for i, (_, c_ref, _, _) in enumerate(streams):
            buf_ref[i, pad - hist:pad, :] = c_ref[...]

    for i, (x_ref, _, w_ref, dst_ref) in enumerate(streams):
        x = x_ref[...]
        w = w_ref[...]
        y = x * w[hist:hist + 1]
        for shift in range(1, conv_w):
            y = y + pltpu.roll(x, shift, 0) * w[hist - shift:hist - shift + 1]
        dst_ref[...] = _silu(y)
        buf_ref[i, pad:2 * pad, :] = x[0:pad]
        head = jnp.zeros((pad, width), F32)
        for j in range(conv_w):
            head = head + buf_ref[i, pad - hist + j:2 * pad - hist + j, :] * w[j:j + 1]
        dst_ref[0:pad, :] = _silu(head)
        buf_ref[i, pad - hist:pad, :] = x[t - hist:t]

    par = par_ref[...]
    lane = _iota((1, LANES), 1)
    consts = _gdn_consts(c)
    nw = nw_ref[...]

    def body(ci, carry):
        sl = pl.ds(pl.multiple_of(ci * c, c), c)
        bac = ba_ref[sl, :]
        bac = jnp.where(lane < heads, _sigmoid(bac), -jnp.exp(par[0:1]) * _softplus(bac + par[1:2]))
        column = lambda idx: jnp.sum(jnp.where(lane == idx, bac, 0.0), axis=-1, keepdims=True)
        operands = []
        for pi in range(hb // 2):
            local = (2 * pi, 2 * pi + 1)
            cols = [slice(h * LANES, (h + 1) * LANES) for h in local]
            stack = lambda ref: jnp.concatenate([ref[sl, cs] for cs in cols], axis=0)
            beta = jnp.concatenate([column(hg * hb + h) for h in local], axis=0)
            g = jnp.concatenate([column(heads + hg * hb + h) for h in local], axis=0)
            operands.append((stack(qs_ref), stack(ks_ref), stack(vs_ref), g, beta, [s_ref[h] for h in local],
                             [z_ref[sl, cs] for cs in cols]))
        chains = []
        for q, k, v, g, beta, states, _ in operands:
            q = q * lax.rsqrt(jnp.sum(q * q, axis=-1, keepdims=True) + 1e-12) * LANES ** -0.5
            k = k * lax.rsqrt(jnp.sum(k * k, axis=-1, keepdims=True) + 1e-12)
            chains.append(_gdn_pair_chunk(q, k, v, g, beta, states, consts))
        results = _lockstep(chains)
        for pi, (o, new_states) in enumerate(results):
            z = operands[pi][6]
            for i in range(2):
                h = 2 * pi + i
                s_ref[h] = new_states[i]
                oh = o[i * c:(i + 1) * c]
                ms = jnp.mean(oh * oh, axis=-1, keepdims=True)
                o_ref[sl, h * LANES:(h + 1) * LANES] = (
                    oh * lax.rsqrt(ms + NORM_EPS) * nw * _silu(z[i])).astype(o_ref.dtype)
        return carry

    lax.fori_loop(0, t // c, body, 0)

    @pl.when(l == pl.num_programs(2) - 1)
    def _():
        so_ref[...] = s_ref[...]


def _gdn(p, conv_w, conv_state, par, norm_w, s0, batch, seq, heads, tile):
    tile = min(tile, seq)
    nl = seq // tile
    cw = conv_w.shape[0]
    hb = GDN_HEADS_PER_STEP
    assert heads % hb == 0 and hb % 2 == 0
    groups = heads // hb
    width = hb * LANES
    pblk = lambda off: pl.BlockSpec((tile, width), lambda b, h, l, off=off: (b * nl + l, off * groups + h))
    wblk = lambda off: pl.BlockSpec((cw, width), lambda b, h, l, off=off: (0, off * groups + h))
    cblk = lambda off: pl.BlockSpec((None, cw - 1, width), lambda b, h, l, off=off: (b, 0, off * groups + h))
    st_spec = pl.BlockSpec((None, hb, LANES, LANES), lambda b, h, l: (b, h, 0, 0))
    tbuf = pltpu.VMEM((tile, width), F32)
    return pl.pallas_call(
        functools.partial(_gdn_kernel, heads=heads, conv_w=cw),
        grid=(batch, groups, nl),
        in_specs=[pblk(0), pblk(1), pblk(2), pblk(3),
                  pl.BlockSpec((tile, LANES), lambda b, h, l: (b * nl + l, 4 * heads)),
                  wblk(0), wblk(1), wblk(2), cblk(0), cblk(1), cblk(2),
                  pl.BlockSpec(par.shape, lambda b, h, l: (0, 0)),
                  pl.BlockSpec((1, LANES), lambda b, h, l: (0, 0)),
                  st_spec],
        out_specs=[pl.BlockSpec((tile, width), lambda b, h, l: (b * nl + l, h)), st_spec],
        out_shape=[jax.ShapeDtypeStruct((batch * seq, heads * LANES), BF16), jax.ShapeDtypeStruct(s0.shape, F32)],
        scratch_shapes=[pltpu.VMEM((hb, LANES, LANES), F32),
                        pltpu.VMEM((3, 16, width), F32),
                        tbuf, tbuf, tbuf],
        compiler_params=_params(("parallel", "parallel", "arbitrary")),
        name="gdn",
    )(p, p, p, p, p, conv_w, conv_w, conv_w, conv_state, conv_state, conv_state, par, norm_w, s0)


def _pad_cols(w, total):
    return jnp.pad(w, ((0, 0),) * (w.ndim - 1) + ((0, total - w.shape[-1]),))


def _ab_layout(a_cols, wb, lora_w, lora_a, lora_g):
    assert lora_w == LANES and lora_a == LANES and lora_g <= 512
    xg_off = -(-(3 * wb + 2 * LANES) // 512) * 512
    b_width = xg_off + 512
    assert a_cols % 512 == 0
    return xg_off, b_width


def _rwkv_cols(t, wb, xg_off, b_width, lora_g):
    head = t[..., :3 * wb + 2 * LANES]
    tail = t[..., 3 * wb + 2 * LANES:]
    z = lambda n: jnp.zeros(t.shape[:-1] + (n,), t.dtype)
    return jnp.concatenate([head, z(xg_off - head.shape[-1]), tail, z(b_width - xg_off - lora_g)], axis=-1)


def _trunk(x, mod, s_hgrn, s_rwkv, s_shift, s_gdn, s_conv, wts, tiles):
    batch, seq, d = x.shape
    m = batch * seq
    x2 = x.reshape(m, d)
    tm, t_hgrn, t_rwkv, t_gdn = tiles
    outs = {}
    depth = mod.shape[0]
    for layer in range(depth):
        j = layer // 2
        vecs6 = [mod[layer, :, i * d:(i + 1) * d].reshape(batch, 1, d) for i in range(6)]
        sh1, sc1, g1, sh2, sc2, g2 = vecs6
        if layer % 2 == 0:
            w = wts["ab"][j]
            p = _norm_matmul(x2, sc1, sh1, w["w_in"], seq, tm, 512)
            ha, wb = w["ha"], w["wb"]
            oa, st_a = _hgrn(p, w["lb"], w["hgrn_norm_w"], jnp.swapaxes(s_hgrn[j], -1, -2), batch, seq, ha, t_hgrn)
            pairs = wb // LANES
            hb = s_rwkv.shape[2]
            nb = s_rwkv.shape[-1]
            sp = s_rwkv[j].reshape(batch, pairs, 2, nb, nb)
            zero = jnp.zeros_like(sp[:, :, 0])
            s0 = jnp.concatenate([jnp.concatenate([sp[:, :, 0], zero], -1),
                                  jnp.concatenate([zero, sp[:, :, 1]], -1)], -2)
            shift = _rwkv_cols(s_shift[j], wb, w["xg_off"], w["b_width"], w["lora_g"])
            ob, st_b = _rwkv(p, shift, w["vecs"], w["mus"], w["w2"], w["a2"], w["g2"], s0,
                             batch, seq, pairs, w["a_cols"], t_rwkv)
            mix = jnp.concatenate([oa, ob], axis=-1)
            outs.setdefault("hgrn", []).append(jnp.swapaxes(st_a, -1, -2))
            outs.setdefault("rwkv", []).append(
                jnp.stack([st_b[:, :, :nb, :nb], st_b[:, :, nb:, nb:]], axis=2).reshape(batch, hb, nb, nb))
            last = p.reshape(batch, seq, -1)[:, seq - 1:, w["a_cols"]:]
            n_head = 3 * wb + 2 * LANES
            outs.setdefault("shift", []).append(
                jnp.concatenate([last[..., :n_head], last[..., w["xg_off"]:w["xg_off"] + w["lora_g"]]], axis=-1))
            w_out = w["w_out"]
        else:
            w = wts["gdn"][j]
            p = _norm_matmul(x2, sc1, sh1, w["w_in"], seq, tm, 512)
            hc = w["hc"]
            mix, st_c = _gdn(p, w["conv_w"], s_conv[j], w["par"], w["norm_w"], s_gdn[j], batch, seq, hc, t_gdn)
            outs.setdefault("gdn", []).append(st_c)
            cw = w["conv_w"].shape[0]
            raw = p.reshape(batch, seq, -1)[:, :, :3 * hc * LANES]
            prev = jnp.concatenate([s_conv[j], raw[:, max(seq - (cw - 1), 0):]], axis=1)
            outs.setdefault("conv", []).append(prev[:, -(cw - 1):])
            w_out = w["w_out"]
        x2 = _out_residual(mix, w_out, x2, g1, seq, tm, 512)
        x2 = _mlp(x2, sc2, sh2, g2, wts["mlp_w1"], wts["mlp_w2"], layer, seq, min(tm, 512), 512, 256)
    x2 = _final_norm(x2, wts["final_w"], 256)
    st = lambda name: jnp.stack(outs[name])
    return x2.reshape(batch, seq, d), st("hgrn"), st("rwkv"), st("shift"), st("gdn"), st("conv")


def kernel(x_prompt, x_sample, c_prompt, c_sample, state_hgrn, state_rwkv, state_rwkv_shift, state_gdn,
           state_gdn_conv, ada_w, ada_b, mlp_w1, mlp_w2, final_norm_w, ab_w_in, ab_w_out, hgrn_lb_logits,
           hgrn_norm_w, rwkv_mu, rwkv_w0, rwkv_w2, rwkv_a0, rwkv_a2, rwkv_g2, rwkv_k_k, rwkv_k_a, rwkv_r_k,
           rwkv_ln_w, rwkv_ln_b, gdn_w_in, gdn_w_out, gdn_conv_w, gdn_a_log, gdn_dt_bias, gdn_norm_w):
    d = x_prompt.shape[-1]
    nb_p, nb_s = x_prompt.shape[0], x_sample.shape[0]
    n_ab, n_c = ab_w_in.shape[0], gdn_w_in.shape[0]
    ha = state_hgrn.shape[2]
    a_cols = 4 * ha * LANES
    wb = rwkv_w0.shape[1]
    lora_w, lora_a, lora_g = rwkv_w2.shape[1], rwkv_a2.shape[1], rwkv_g2.shape[1]
    xg_off, b_width = _ab_layout(a_cols, wb, lora_w, lora_a, lora_g)
    hc = state_gdn.shape[2]

    rows = nb_p + nb_s
    rows_pad = -(-rows // 8) * 8
    c_all = jnp.pad(jnp.concatenate([c_prompt, c_sample], axis=0), ((0, rows_pad - rows), (0, 0)))
    mod = _adaln(c_all, ada_w, ada_b)

    lbs = jnp.cumsum(jax.nn.softmax(hgrn_lb_logits.astype(F32), axis=0), axis=0)
    wts = {"ab": [], "gdn": [], "final_w": final_norm_w.reshape(1, d),
           "mlp_w1": mlp_w1.astype(BF16), "mlp_w2": mlp_w2.astype(BF16)}
    for j in range(n_ab):
        hole = ((a_cols + 3 * wb + 2 * LANES) // LANES, (a_cols + xg_off) // LANES)
        w_in = _cast_cols(ab_w_in, j, a_cols + b_width, hole, hole[1] - hole[0])
        mu = _rwkv_cols(rwkv_mu[j][None], wb, xg_off, b_width, lora_g)[0]
        vec_rows = [mu[:wb], mu[wb:2 * wb], mu[2 * wb:3 * wb], rwkv_w0[j], rwkv_a0[j], rwkv_k_k[j], rwkv_k_a[j],
                    rwkv_r_k[j].reshape(-1), rwkv_ln_w[j], rwkv_ln_b[j]]
        vecs = jnp.pad(jnp.stack(vec_rows), ((0, 16 - len(vec_rows)), (0, 0)))
        mus = jnp.stack([_pad_cols(mu[3 * wb:3 * wb + LANES], 512), _pad_cols(mu[3 * wb + LANES:3 * wb + 2 * LANES], 512),
                         mu[xg_off:xg_off + 512]])
        wts["ab"].append({
            "w_in": w_in, "w_out": ab_w_out[j].astype(BF16), "lb": lbs[j][None],
            "hgrn_norm_w": hgrn_norm_w[j][None], "vecs": vecs, "mus": jnp.pad(mus, ((0, 5), (0, 0))),
            "w2": rwkv_w2[j].astype(BF16), "a2": rwkv_a2[j].astype(BF16),
            "g2": jnp.pad(rwkv_g2[j], ((0, 512 - lora_g), (0, 0))).astype(BF16),
            "ha": ha, "wb": wb, "a_cols": a_cols, "xg_off": xg_off, "b_width": b_width, "lora_g": lora_g})
    for j in range(n_c):
        cols = gdn_w_in.shape[-1]
        cols_pad = -(-cols // 512) * 512
        par = jnp.zeros((8, LANES), F32)
        par = par.at[0, hc:2 * hc].set(gdn_a_log[j]).at[1, hc:2 * hc].set(gdn_dt_bias[j])
        wts["gdn"].append({
            "w_in": _cast_cols(gdn_w_in, j, cols_pad), "w_out": gdn_w_out[j].astype(BF16),
            "conv_w": gdn_conv_w[j], "par": par, "norm_w": gdn_norm_w[j][None], "hc": hc})

    zeros = lambda s: jnp.zeros((s.shape[0], nb_p) + s.shape[2:], x_prompt.dtype)
    y_p, hg_p, rw_p, sh_p, gd_p, cv_p = _trunk(
        x_prompt, mod[:, :nb_p], zeros(state_hgrn), zeros(state_rwkv), zeros(state_rwkv_shift),
        zeros(state_gdn), zeros(state_gdn_conv), wts, PROMPT_TILES)
    y_s, hg_s, rw_s, sh_s, gd_s, cv_s = _trunk(
        x_sample, mod[:, nb_p:rows], state_hgrn, state_rwkv, state_rwkv_shift, state_gdn, state_gdn_conv,
        wts, SAMPLE_TILES)
    return (y_p, y_s, hg_p, rw_p, sh_p, gd_p, cv_p, hg_s, rw_s, sh_s, gd_s, cv_s)
```

```python
import functools
import math

import jax
import jax.numpy as jnp
from jax import lax
from jax.experimental import pallas as pl
from jax.experimental.pallas import tpu as pltpu

F32 = jnp.float32
BF16 = jnp.bfloat16
HI = lax.Precision.HIGHEST

NORM_EPS = 1e-6
RWKV_LN_EPS = 64e-5
CHUNK = 64
DIAG = 8
LANES = 128
VMEM_LIMIT = 56 * 1024 * 1024
MLP_VMEM_LIMIT = 60 * 1024 * 1024
PROMPT_TILES = (1024, 512, 256, 256)
SAMPLE_TILES = (512, 64, 64, 64)
HGRN_HEADS_PER_STEP = 8
GDN_SOLVE_HI_LO = True
RWKV_PAIRS_PER_STEP = 8
GDN_HEADS_PER_STEP = 16


def _mm(a, b, ca, cb, exact=False):
    dims = (((ca,), (cb,)), ((), ()))
    if exact:
        return lax.dot_general(a, b, dims, precision=HI, preferred_element_type=F32)
    return lax.dot_general(a.astype(BF16), b.astype(BF16), dims, preferred_element_type=F32)


def _nn(a, b, exact=False):
    return _mm(a, b, 1, 0, exact)


def _nt(a, b, exact=False):
    return _mm(a, b, 1, 1, exact)


def _tn(a, b, exact=False):
    return _mm(a, b, 0, 0, exact)


def _sigmoid(x):
    return 1.0 / (1.0 + jnp.exp(-x))


def _silu(x):
    return x * _sigmoid(x)


def _softplus(x):
    return jnp.maximum(x, 0.0) + jnp.log(1.0 + jnp.exp(-jnp.abs(x)))


def _iota(shape, dim):
    return lax.broadcasted_iota(jnp.int32, shape, dim)


def _split2(x):
    hi = x.astype(BF16)
    return hi, (x - hi.astype(F32)).astype(BF16)


def _split3(x):
    hi = x.astype(BF16)
    r = x - hi.astype(F32)
    lo = r.astype(BF16)
    return hi, lo, (r - lo.astype(F32)).astype(BF16)


def _mm_hi(a, b, a_parts=None, b_parts=None):
    a_hi, a_lo = a_parts if a_parts is not None else _split2(a)
    b_hi, b_lo = b_parts if b_parts is not None else _split2(b)
    return jnp.dot(jnp.concatenate([a_hi, a_lo, a_hi], axis=1), jnp.concatenate([b_hi, b_hi, b_lo], axis=0),
                   preferred_element_type=F32)


def _mask_mm_left(mask3, x):
    return jnp.dot(mask3, jnp.concatenate(_split3(x), axis=0), preferred_element_type=F32)


def _mask_mm_right(x, mask3):
    return jnp.dot(jnp.concatenate(_split3(x), axis=1), mask3, preferred_element_type=F32)


def _nilpotent_solve(m, x, stages, hi_lo=True):
    n = m.shape[0]
    mm = _mm_hi if hi_lo else _nn
    for k in range(stages):
        last = k == stages - 1
        y = mm(m, x if last else jnp.concatenate([m, x], axis=1))
        yield
        if last:
            x = x + y
        else:
            m = y[:, :n]
            x = x + y[:, n:]
    return x


def _lockstep(generators):
    generators = list(generators)
    results = [None] * len(generators)
    live = list(range(len(generators)))
    while live:
        for i in list(live):
            try:
                next(generators[i])
            except StopIteration as stop:
                results[i] = stop.value
                live.remove(i)
    return results


def _params(sem, vmem=VMEM_LIMIT):
    return pltpu.CompilerParams(dimension_semantics=sem, vmem_limit_bytes=vmem)


def _rows_per_tile(rows_per_batch, tile):
    if tile <= rows_per_batch:
        assert rows_per_batch % tile == 0
        return 1
    assert tile % rows_per_batch == 0
    return tile // rows_per_batch


def _adaln_kernel(c_ref, w_ref, b_ref, o_ref):
    c = _silu(c_ref[...]).astype(BF16)
    o_ref[...] = jnp.dot(c, w_ref[...].astype(BF16), preferred_element_type=F32) + b_ref[...]


def _adaln(c, ada_w, ada_b):
    depth, d, n = ada_w.shape
    rows = c.shape[0]
    tn = 512 if n % 512 == 0 else n
    return pl.pallas_call(
        _adaln_kernel,
        grid=(depth, n // tn),
        in_specs=[pl.BlockSpec((rows, d), lambda l, j: (0, 0)),
                  pl.BlockSpec((None, d, tn), lambda l, j: (l, 0, j)),
                  pl.BlockSpec((None, 1, tn), lambda l, j: (l, 0, j))],
        out_specs=pl.BlockSpec((None, rows, tn), lambda l, j: (l, 0, j)),
        out_shape=jax.ShapeDtypeStruct((depth, rows, n), F32),
        compiler_params=_params(("parallel", "parallel")),
        name="adaln",
    )(c, ada_w, ada_b.reshape(depth, 1, n))


def _cast_cols_kernel(w_ref, o_ref, *, hole, shift, n_src):
    j = pl.program_id(0)
    src_col = jnp.where(j < hole[0], j, j - shift) * LANES + _iota((LANES, 1), 0)
    keep = (src_col < n_src) & jnp.logical_not((j >= hole[0]) & (j < hole[1]))
    o_ref[...] = jnp.where(keep, w_ref[...], 0.0).T.astype(o_ref.dtype)


def _cast_cols(w, layer, n_out, hole=(0, 0), shift=0):
    _, k, n_src = w.shape
    if hole[0] == hole[1]:
        hole = (n_out // LANES, n_out // LANES)
    last = (n_src - 1) // LANES
    return pl.pallas_call(
        functools.partial(_cast_cols_kernel, hole=hole, shift=shift, n_src=n_src),
        grid=(n_out // LANES,),
        in_specs=[pl.BlockSpec((None, LANES, k),
                               lambda j: (layer, jnp.clip(jnp.where(j < hole[0], j, j - shift), 0, last), 0))],
        out_specs=pl.BlockSpec((k, LANES), lambda j: (0, j)),
        out_shape=jax.ShapeDtypeStruct((k, n_out), BF16),
        compiler_params=_params(("parallel",)),
        name="cast_cols",
    )(jnp.swapaxes(w, 1, 2))


def _modulated_norm(x_ref, sc_ref, sh_ref, h_ref, groups):
    rows = x_ref.shape[0] // groups
    step = min(rows, 128)

    for gi in range(groups):
        sc = 1.0 + sc_ref[gi]
        sh = sh_ref[gi]

        def body(i, carry, gi=gi, sc=sc, sh=sh):
            sl = pl.ds(pl.multiple_of(gi * rows + i * step, step), step)
            x = x_ref[sl, :]
            ms = jnp.mean(x * x, axis=-1, keepdims=True)
            h_ref[sl, :] = (x * lax.rsqrt(ms + NORM_EPS) * sc + sh).astype(BF16)
            return carry

        lax.fori_loop(0, rows // step, body, 0)


def _norm_mm_kernel(x_ref, sc_ref, sh_ref, w_ref, o_ref, h_ref, *, groups):
    @pl.when(pl.program_id(1) == 0)
    def _():
        _modulated_norm(x_ref, sc_ref, sh_ref, h_ref, groups)

    o_ref[...] = jnp.dot(h_ref[...], w_ref[...], preferred_element_type=F32).astype(o_ref.dtype)


def _norm_matmul(x, scale, shift, w, rows_per_batch, tm, tn):
    m, d = x.shape
    n = w.shape[1]
    tm = min(tm, m)
    groups = _rows_per_tile(rows_per_batch, tm)
    bidx = (lambda i, j: (i * tm // rows_per_batch, 0, 0)) if groups == 1 else (lambda i, j: (i, 0, 0))
    return pl.pallas_call(
        functools.partial(_norm_mm_kernel, groups=groups),
        grid=(m // tm, n // tn),
        in_specs=[pl.BlockSpec((tm, d), lambda i, j: (i, 0), pipeline_mode=pl.Buffered(1)),
                  pl.BlockSpec((groups, 1, d), bidx),
                  pl.BlockSpec((groups, 1, d), bidx),
                  pl.BlockSpec((d, tn), lambda i, j: (0, j))],
        out_specs=pl.BlockSpec((tm, tn), lambda i, j: (i, j)),
        out_shape=jax.ShapeDtypeStruct((m, n), F32),
        scratch_shapes=[pltpu.VMEM((tm, d), BF16)],
        compiler_params=_params(("parallel", "arbitrary")),
        name="norm_matmul",
    )(x, scale, shift, w)


def _out_res_kernel(a_ref, w_ref, x_ref, g_ref, o_ref, *, groups):
    acc = jnp.dot(a_ref[...], w_ref[...], preferred_element_type=F32)
    rows = acc.shape[0] // groups
    for gi in range(groups):
        sl = slice(gi * rows, (gi + 1) * rows)
        o_ref[sl, :] = x_ref[sl, :] + g_ref[gi] * acc[sl, :]


def _out_residual(a, w, x, gate, rows_per_batch, tm, tn):
    m, k = a.shape
    n = w.shape[1]
    tm = min(tm, m)
    groups = _rows_per_tile(rows_per_batch, tm)
    bidx = (lambda i, j: (i * tm // rows_per_batch, 0, j)) if groups == 1 else (lambda i, j: (i, 0, j))
    return pl.pallas_call(
        functools.partial(_out_res_kernel, groups=groups),
        grid=(m // tm, n // tn),
        in_specs=[pl.BlockSpec((tm, k), lambda i, j: (i, 0)),
                  pl.BlockSpec((k, tn), lambda i, j: (0, j)),
                  pl.BlockSpec((tm, tn), lambda i, j: (i, j)),
                  pl.BlockSpec((groups, 1, tn), bidx)],
        out_specs=pl.BlockSpec((tm, tn), lambda i, j: (i, j)),
        out_shape=jax.ShapeDtypeStruct((m, n), F32),
        compiler_params=_params(("parallel", "parallel")),
        name="out_residual",
    )(a, w, x, gate)


def _mlp_kernel(x_ref, sc_ref, sh_ref, xr_ref, g_ref, w1_ref, w2_ref, o_ref, h_ref, u_ref, *, groups, nf):
    j = pl.program_id(1)

    @pl.when(j == 0)
    def _():
        _modulated_norm(x_ref, sc_ref, sh_ref, h_ref, groups)

    @pl.when(j < nf)
    def _():
        u = jnp.dot(h_ref[...], w1_ref[...], preferred_element_type=F32)
        u_ref[j] = jnp.square(jnp.maximum(u, 0.0)).astype(BF16)

    @pl.when(j >= nf)
    def _():
        tf = u_ref.shape[2]
        acc = jnp.dot(u_ref[0], w2_ref[0:tf, :], preferred_element_type=F32)
        for f in range(1, nf):
            acc += jnp.dot(u_ref[f], w2_ref[f * tf:(f + 1) * tf, :], preferred_element_type=F32)
        rows = acc.shape[0] // groups
        for gi in range(groups):
            sl = slice(gi * rows, (gi + 1) * rows)
            o_ref[sl, :] = xr_ref[sl, :] + g_ref[gi] * acc[sl, :]


def _mlp(x, scale, shift, gate, w1, w2, layer, rows_per_batch, tm, tf, tn):
    m, d = x.shape
    ff = w1.shape[2]
    tm = min(tm, m)
    nf, nn = ff // tf, d // tn
    groups = _rows_per_tile(rows_per_batch, tm)
    batch_of = (lambda i: i * tm // rows_per_batch) if groups == 1 else (lambda i: i)
    vec = pl.BlockSpec((groups, 1, d), lambda i, j: (batch_of(i), 0, 0))
    col = lambda j: jnp.maximum(j - nf, 0)
    return pl.pallas_call(
        functools.partial(_mlp_kernel, groups=groups, nf=nf),
        grid=(m // tm, nf + nn),
        in_specs=[pl.BlockSpec((tm, d), lambda i, j: (i, 0), pipeline_mode=pl.Buffered(1)),
                  vec, vec,
                  pl.BlockSpec((tm, tn), lambda i, j: (i, col(j))),
                  pl.BlockSpec((groups, 1, tn), lambda i, j: (batch_of(i), 0, col(j))),
                  pl.BlockSpec((None, d, tf), lambda i, j: (layer, 0, jnp.minimum(j, nf - 1))),
                  pl.BlockSpec((None, ff, tn), lambda i, j: (layer, 0, col(j)))],
        out_specs=pl.BlockSpec((tm, tn), lambda i, j: (i, col(j))),
        out_shape=jax.ShapeDtypeStruct((m, d), F32),
        scratch_shapes=[pltpu.VMEM((tm, d), BF16), pltpu.VMEM((nf, tm, tf), BF16)],
        compiler_params=_params(("parallel", "arbitrary"), MLP_VMEM_LIMIT),
        name="mlp",
    )(x, scale, shift, x, gate, w1, w2)


def _final_norm_kernel(x_ref, w_ref, o_ref):
    x = x_ref[...]
    ms = jnp.mean(x * x, axis=-1, keepdims=True)
    o_ref[...] = x * lax.rsqrt(ms + NORM_EPS) * w_ref[...]


def _final_norm(x, w, tm):
    m, d = x.shape
    tm = min(tm, m)
    return pl.pallas_call(
        _final_norm_kernel,
        grid=(m // tm,),
        in_specs=[pl.BlockSpec((tm, d), lambda i: (i, 0)), pl.BlockSpec((1, d), lambda i: (0, 0))],
        out_specs=pl.BlockSpec((tm, d), lambda i: (i, 0)),
        out_shape=jax.ShapeDtypeStruct((m, d), F32),
        compiler_params=_params(("parallel",)),
        name="final_norm",
    )(x, w)


def _hgrn_consts(c):
    row = _iota((c, c), 0)
    col = _iota((c, c), 1)
    ltri = (_iota((c, 3 * c), 0) >= _iota((c, 3 * c), 1) % c).astype(BF16)
    levels = []
    gs = 2 * DIAG
    while gs <= c:
        half = gs // 2
        pair = (row // gs == col // gs) & (row % gs >= half) & (col % gs < half)
        levels.append((gs, pair))
        gs *= 2
    rix = _iota((c, 1), 0)
    return ltri, levels, rix


def _hgrn_chunk(q, k, v, g, st, consts):
    ltri, levels, rix = consts
    c = q.shape[0]
    gc = _mask_mm_left(ltri, g)
    yield
    lhs, rhs = [q * jnp.exp(gc)], [st]
    for gs, _ in levels:
        half = gs // 2
        ref = jnp.concatenate(
            [jnp.broadcast_to(gc[m0 * gs + half - 1:m0 * gs + half, :], (gs, gc.shape[1])) for m0 in range(c // gs)],
            axis=0)
        is_q = (rix % gs) >= half
        lhs.append(jnp.where(is_q, q * jnp.exp(gc - ref), 0.0))
        rhs.append(jnp.where(is_q, 0.0, k * jnp.exp(ref - gc)))
        rhs.append(jnp.zeros((LANES - c, k.shape[1]), F32))
    prod = _nt(jnp.concatenate(lhs, axis=0), jnp.concatenate(rhs, axis=0))
    gl = gc[c - 1:c, :]
    upd = _tn(v, k * jnp.exp(gl - gc))
    yield
    att = jnp.zeros((c, c), F32)
    for li, (_, pair) in enumerate(levels):
        att = att + jnp.where(pair, prod[(li + 1) * c:(li + 2) * c, (li + 1) * LANES:(li + 1) * LANES + c], 0.0)
    o = prod[:c, :LANES] + _nn(att, v)
    for dist in range(DIAG):
        if dist == 0:
            kd, gd, vd = k, gc, v
        else:
            kd = pltpu.roll(k, dist, 0)
            gd = pltpu.roll(gc, dist, 0)
            vd = pltpu.roll(v, dist, 0)
        w = jnp.sum(q * kd * jnp.exp(gc - gd), axis=-1, keepdims=True)
        o = o + jnp.where((rix % DIAG) >= dist, w, 0.0) * vd
    yield
    return o, st * jnp.exp(gl) + upd


def _hgrn_kernel(q_ref, f_ref, i_ref, g_ref, lb_ref, nw_ref, s0_ref, o_ref, so_ref, s_ref, *, dk):
    l = pl.program_id(2)

    @pl.when(l == 0)
    def _():
        s_ref[...] = s0_ref[...]

    t, width = q_ref.shape
    c = min(CHUNK, t)
    consts = _hgrn_consts(c)
    lb = lb_ref[...]
    nw = nw_ref[...]

    def body(ci, carry):
        sl = pl.ds(pl.multiple_of(ci * c, c), c)
        f = lb + (1.0 - lb) * _sigmoid(f_ref[sl, :])
        q = _silu(q_ref[sl, :]) * dk ** -0.5
        k = 1.0 - f
        v = i_ref[sl, :]
        g = jnp.log(f)
        gate = _silu(g_ref[sl, :])
        heads = [slice(h * dk, (h + 1) * dk) for h in range(width // dk)]
        results = _lockstep(_hgrn_chunk(q[:, hs], k[:, hs], v[:, hs], g[:, hs], s_ref[h], consts)
                            for h, hs in enumerate(heads))
        for h, (o, st) in enumerate(results):
            s_ref[h] = st
            ms = jnp.mean(o * o, axis=-1, keepdims=True)
            o_ref[sl, heads[h]] = (o * lax.rsqrt(ms + NORM_EPS) * nw * gate[:, heads[h]]).astype(o_ref.dtype)
        return carry

    lax.fori_loop(0, t // c, body, 0)

    @pl.when(l == pl.num_programs(2) - 1)
    def _():
        so_ref[...] = s_ref[...]


def _hgrn(p, lb, norm_w, s0t, batch, seq, heads, tile):
    dk = LANES
    tile = min(tile, seq)
    nl = seq // tile
    hb = HGRN_HEADS_PER_STEP
    assert heads % hb == 0
    groups = heads // hb
    width = hb * dk
    blk = lambda off: pl.BlockSpec((tile, width), lambda b, h, l, off=off: (b * nl + l, off * groups + h))
    st_spec = pl.BlockSpec((None, hb, dk, dk), lambda b, h, l: (b, h, 0, 0))
    return pl.pallas_call(
        functools.partial(_hgrn_kernel, dk=dk),
        grid=(batch, groups, nl),
        in_specs=[blk(0), blk(1), blk(2), blk(3),
                  pl.BlockSpec((1, width), lambda b, h, l: (0, h)),
                  pl.BlockSpec((1, dk), lambda b, h, l: (0, 0)),
                  st_spec],
        out_specs=[pl.BlockSpec((tile, width), lambda b, h, l: (b * nl + l, h)), st_spec],
        out_shape=[jax.ShapeDtypeStruct((batch * seq, heads * dk), BF16),
                   jax.ShapeDtypeStruct(s0t.shape, F32)],
        scratch_shapes=[pltpu.VMEM((hb, dk, dk), F32)],
        compiler_params=_params(("parallel", "parallel", "arbitrary")),
        name="hgrn2",
    )(p, p, p, p, lb, norm_w, s0t)


def _rwkv_consts(c):
    n = 4 * c
    row = _iota((n, n), 0)
    col = _iota((n, n), 1)
    same_head = ((row // c) % 2) == ((col // c) % 2)
    strict = (row % c) > (col % c)
    incl = (row % c) >= (col % c)
    keep = same_head & (strict | ((row >= 2 * c) & incl))
    ltri = (_iota((c, 3 * c), 0) >= _iota((c, 3 * c), 1) % c).astype(BF16)
    head0 = _iota((1, LANES), 1) < (LANES // 2)
    rs = _iota((LANES, LANES), 0)
    cs = _iota((LANES, LANES), 1)
    bdiag = (rs // (LANES // 2)) == (cs // (LANES // 2))
    return keep, ltri, head0, bdiag


def _rwkv_chunk(r, lg, k, v, a, b, s, consts):
    keep, ltri, head0, bdiag = consts
    c = r.shape[0]
    gc = _mask_mm_left(ltri, lg)
    yield
    e_incl = jnp.exp(gc)
    e_inv = jnp.exp(-gc)
    at = a * jnp.exp(gc - lg)
    rt = r * e_incl
    bt = b * e_inv
    kt = k * e_inv
    split = lambda x: [jnp.where(head0, x, 0.0), jnp.where(head0, 0.0, x)]
    lhs = jnp.concatenate(split(at) + split(rt), axis=0)
    rhs = jnp.concatenate([bt, bt, kt, kt, s], axis=0)
    pls = _nt(lhs, rhs)
    yield
    p = jnp.where(keep, pls[:, :4 * c], 0.0)
    ls = pls[:, 4 * c:]
    vst = jnp.concatenate(split(v), axis=0)
    pv = _nn(p[:, 2 * c:], vst)
    yield
    ust = yield from _nilpotent_solve(p[:2 * c, :2 * c], ls[:2 * c] + pv[:2 * c], int(math.log2(c)), hi_lo=False)
    ost = ls[2 * c:] + pv[2 * c:] + _nn(p[2 * c:, :2 * c], ust)
    yield
    o = ost[:c] + ost[c:]
    u = ust[:c] + ust[c:]
    gl = gc[c - 1:c, :]
    e_tail = jnp.exp(gl - gc)
    upd = _tn(jnp.concatenate([u, v], axis=0), jnp.concatenate([b * e_tail, k * e_tail], axis=0))
    yield
    return o, s * jnp.exp(gl) + jnp.where(bdiag, upd, 0.0)


def _rwkv_kernel(xr_ref, xk_ref, xv_ref, xw_ref, xa_ref, xg_ref,
                 hr_ref, hk_ref, hv_ref, hw_ref, ha_ref, hg_ref,
                 vec_ref, mus_ref, w2_ref, a2_ref, g2_ref, s0_ref,
                 o_ref, so_ref,
                 s_ref, buf_ref, bufs_ref, bufg_ref, r_ref, lg_ref, k_ref, v_ref, a_ref, b_ref, y_ref):
    l = pl.program_id(2)
    t, width = xr_ref.shape
    c = min(CHUNK, t)

    @pl.when(l == 0)
    def _():
        s_ref[...] = s0_ref[...]
        for i, h_ref in enumerate((hr_ref, hk_ref, hv_ref)):
            buf_ref[i, 0:1, :] = h_ref[...]
        for i, h_ref in enumerate((hw_ref, ha_ref)):
            bufs_ref[i, 0:1, :] = h_ref[...]
        bufg_ref[0:1, :] = hg_ref[...]

    first_row = _iota((t, 1), 0) == 0

    def shifted(carry, x_ref, mu):
        x = x_ref[...]
        prev = jnp.where(first_row, carry[0:1, :], pltpu.roll(x, 1, 0))
        carry[0:1, :] = x[t - 1:t, :]
        return x + (prev - x) * mu

    vec = vec_ref[...]
    mus = mus_ref[...]
    r = shifted(buf_ref.at[0], xr_ref, vec[0:1])
    k = shifted(buf_ref.at[1], xk_ref, vec[1:2])
    v = shifted(buf_ref.at[2], xv_ref, vec[2:3])
    xw = shifted(bufs_ref.at[0], xw_ref, mus[0:1, :LANES])
    xa = shifted(bufs_ref.at[1], xa_ref, mus[1:2, :LANES])
    xg = shifted(bufg_ref, xg_ref, mus[2:3])

    w0, a0, k_k, k_a, r_k, ln_w, ln_b = (vec[i:i + 1] for i in range(3, 10))
    half = LANES // 2
    ones2 = ((_iota((2 * LANES, LANES), 0) % LANES) // half == _iota((2 * LANES, LANES), 1) // half).astype(BF16)

    def head_sums(x):
        outs = []
        for blk in range(width // LANES):
            outs.append(jnp.dot(jnp.concatenate(_split2(x[:, blk * LANES:(blk + 1) * LANES]), axis=1), ones2,
                                preferred_element_type=F32))
        return jnp.concatenate(outs, axis=1)

    w_log = -_softplus(-(w0 + _nn(jnp.tanh(xw), w2_ref[...]))) - 0.5
    a_lr = _sigmoid(a0 + _nn(xa, a2_ref[...]))
    gate = _nn(_sigmoid(xg), g2_ref[...])
    kk = k * k_k
    kk = kk * lax.rsqrt(head_sums(kk * kk) + 1e-12)
    k = k * (1.0 + (a_lr - 1.0) * k_a)
    r_ref[...] = r
    lg_ref[...] = -jnp.exp(w_log)
    k_ref[...] = k
    v_ref[...] = v
    a_ref[...] = -kk
    b_ref[...] = kk * a_lr

    consts = _rwkv_consts(c)

    def body(ci, carry):
        sl = pl.ds(pl.multiple_of(ci * c, c), c)
        operands = []
        for pi in range(width // LANES):
            cols = slice(pi * LANES, (pi + 1) * LANES)
            operands.append((r_ref[sl, cols], lg_ref[sl, cols], k_ref[sl, cols], v_ref[sl, cols],
                             a_ref[sl, cols], b_ref[sl, cols], s_ref[pi]))
        results = _lockstep(_rwkv_chunk(*ops, consts) for ops in operands)
        for pi, (o, s_new) in enumerate(results):
            s_ref[pi] = s_new
            y_ref[sl, pi * LANES:(pi + 1) * LANES] = o
        return carry

    lax.fori_loop(0, t // c, body, 0)

    o = y_ref[...]
    inv_n = 1.0 / half
    mean = head_sums(o) * inv_n
    dev = o - mean
    var = head_sums(dev * dev) * inv_n
    o = dev * lax.rsqrt(var + RWKV_LN_EPS) * ln_w + ln_b
    o = o + head_sums(r * k * r_k) * v
    o_ref[...] = (o * gate).astype(o_ref.dtype)

    @pl.when(l == pl.num_programs(2) - 1)
    def _():
        so_ref[...] = s_ref[...]


def _rwkv(p, shift, vecs, mus, w2, a2, g2, s0, batch, seq, pairs, col0, tile):
    tile = min(tile, seq)
    nl = seq // tile
    wb = pairs * LANES
    pb = RWKV_PAIRS_PER_STEP
    width = pb * LANES
    groups = pairs // pb
    assert pairs % pb == 0 and col0 % width == 0
    cb0 = col0 // LANES
    gb0 = col0 // width
    xg_off = -(-(3 * wb + 2 * LANES) // 512) * 512
    assert (col0 + xg_off) % 512 == 0
    pblk = lambda off: pl.BlockSpec((tile, width), lambda b, h, l, off=off: (b * nl + l, gb0 + off * groups + h))
    pfix = lambda cb: pl.BlockSpec((tile, LANES), lambda b, h, l, cb=cb: (b * nl + l, cb))
    hblk = lambda off: pl.BlockSpec((None, 1, width), lambda b, h, l, off=off: (b, 0, off * groups + h))
    hfix = lambda cb: pl.BlockSpec((None, 1, LANES), lambda b, h, l, cb=cb: (b, 0, cb))
    st_spec = pl.BlockSpec((None, pb, LANES, LANES), lambda b, h, l: (b, h, 0, 0))
    col = lambda rows: pl.BlockSpec((rows, width), lambda b, h, l: (0, h))
    full = lambda a: pl.BlockSpec(a.shape, lambda b, h, l: (0,) * a.ndim)
    tbuf = pltpu.VMEM((tile, width), F32)
    return pl.pallas_call(
        _rwkv_kernel,
        grid=(batch, groups, nl),
        in_specs=[pblk(0), pblk(1), pblk(2), pfix(cb0 + 3 * pairs), pfix(cb0 + 3 * pairs + 1),
                  pl.BlockSpec((tile, 512), lambda b, h, l: (b * nl + l, (col0 + xg_off) // 512)),
                  hblk(0), hblk(1), hblk(2), hfix(3 * pairs), hfix(3 * pairs + 1),
                  pl.BlockSpec((None, 1, 512), lambda b, h, l: (b, 0, xg_off // 512)),
                  col(vecs.shape[0]), full(mus), col(w2.shape[0]), col(a2.shape[0]), col(g2.shape[0]), st_spec],
        out_specs=[pl.BlockSpec((tile, width), lambda b, h, l: (b * nl + l, h)), st_spec],
        out_shape=[jax.ShapeDtypeStruct((batch * seq, wb), BF16), jax.ShapeDtypeStruct(s0.shape, F32)],
        scratch_shapes=[pltpu.VMEM((pb, LANES, LANES), F32),
                        pltpu.VMEM((3, 8, width), F32),
                        pltpu.VMEM((2, 8, LANES), F32),
                        pltpu.VMEM((8, 512), F32),
                        tbuf, tbuf, tbuf, tbuf, tbuf, tbuf, tbuf],
        compiler_params=_params(("parallel", "parallel", "arbitrary")),
        name="rwkv7",
    )(p, p, p, p, p, p, shift, shift, shift, shift, shift, shift, vecs, mus, w2, a2, g2, s0)


def _gdn_consts(c):
    n = 2 * c
    row = _iota((n, n), 0)
    col = _iota((n, n), 1)
    same = (row // c) == (col // c)
    incl = same & ((row % c) >= (col % c))
    strict = same & ((row % c) > (col % c))
    r3 = _iota((n, 3 * n), 0)
    c3 = _iota((n, 3 * n), 1) % n
    ltri3 = (((r3 // c) == (c3 // c)) & ((r3 % c) >= (c3 % c))).astype(BF16)
    return ltri3, incl, strict, strict.astype(F32)


def _gdn_pair_chunk(q, k, v, g, beta, states, consts):
    ltri3, incl, strict, strict_f = consts
    n = q.shape[0]
    c = n // 2
    cum = _mask_mm_left(ltri3, jnp.concatenate([g * strict_f, jnp.broadcast_to(g, (n, LANES))], axis=1))
    yield
    gc = cum[:, n:]
    dec = jnp.exp(jnp.where(incl, cum[:, :n], -1e30))
    eg = jnp.exp(gc)
    kq = _nt(jnp.concatenate([k, q], axis=0), k)
    head0 = _iota((n, 1), 0) < c
    head0_2 = (_iota((2 * n, 1), 0) % n) < c
    by_head = lambda x, m: jnp.concatenate([jnp.where(m, x, 0.0), jnp.where(m, 0.0, x)], axis=1)
    ks = _nn(by_head(jnp.concatenate([k * eg, q * eg], axis=0), head0_2), jnp.concatenate(states, axis=0))
    yield
    a_mat = jnp.where(strict, beta * kq[:n] * dec, 0.0)
    v_new = yield from _nilpotent_solve(-a_mat, beta * (v - ks[:n]), int(math.log2(c)), hi_lo=GDN_SOLVE_HI_LO)
    o = ks[n:] + _nn(kq[n:] * dec, v_new)
    yield
    gl = jnp.where(head0, gc[c - 1:c, :], gc[n - 1:n, :])
    upd = _tn(k * jnp.exp(gl - gc), by_head(v_new, head0))
    yield
    new_states = [s * jnp.exp(gc[(hi + 1) * c - 1:(hi + 1) * c, :]) + upd[:, hi * LANES:(hi + 1) * LANES]
                  for hi, s in enumerate(states)]
    return o, new_states


def _gdn_kernel(q_ref, k_ref, v_ref, z_ref, ba_ref, cwq_ref, cwk_ref, cwv_ref, cq_ref, ck_ref, cv_ref,
                par_ref, nw_ref, s0_ref, o_ref, so_ref, s_ref, buf_ref, qs_ref, ks_ref, vs_ref,
                *, heads, conv_w):
    hg = pl.program_id(1)
    l = pl.program_id(2)
    t, width = q_ref.shape
    hb = width // LANES
    c = min(CHUNK, t)
    pad = 8
    hist = conv_w - 1
    streams = ((q_ref, cq_ref, cwq_ref, qs_ref), (k_ref, ck_ref, cwk_ref, ks_ref), (v_ref, cv_ref, cwv_ref, vs_ref))

    @pl.when(l == 0)
    def _():
        s_ref[...] = s0_ref[...]
        for i, (_, c_ref, _, _) in enumerate(streams):
            buf_ref[i, pad - hist:pad, :] = c_ref[...]

    for i, (x_ref, _, w_ref, dst_ref) in enumerate(streams):
        x = x_ref[...]
        w = w_ref[...]
        y = x * w[hist:hist + 1]
        for shift in range(1, conv_w):
            y = y + pltpu.roll(x, shift, 0) * w[hist - shift:hist - shift + 1]
        dst_ref[...] = _silu(y)
        buf_ref[i, pad:2 * pad, :] = x[0:pad]
        head = jnp.zeros((pad, width), F32)
        for j in range(conv_w):
            head = head + buf_ref[i, pad - hist + j:2 * pad - hist + j, :] * w[j:j + 1]
        dst_ref[0:pad, :] = _silu(head)
        buf_ref[i, pad - hist:pad, :] = x[t - hist:t]

    par = par_ref[...]
    lane = _iota((1, LANES), 1)
    consts = _gdn_consts(c)
    nw = nw_ref[...]

    def body(ci, carry):
        sl = pl.ds(pl.multiple_of(ci * c, c), c)
        bac = ba_ref[sl, :]
        bac = jnp.where(lane < heads, _sigmoid(bac), -jnp.exp(par[0:1]) * _softplus(bac + par[1:2]))
        column = lambda idx: jnp.sum(jnp.where(lane == idx, bac, 0.0), axis=-1, keepdims=True)
        operands = []
        for pi in range(hb // 2):
            local = (2 * pi, 2 * pi + 1)
            cols = [slice(h * LANES, (h + 1) * LANES) for h in local]
            stack = lambda ref: jnp.concatenate([ref[sl, cs] for cs in cols], axis=0)
            beta = jnp.concatenate([column(hg * hb + h) for h in local], axis=0)
            g = jnp.concatenate([column(heads + hg * hb + h) for h in local], axis=0)
            operands.append((stack(qs_ref), stack(ks_ref), stack(vs_ref), g, beta, [s_ref[h] for h in local],
                             [z_ref[sl, cs] for cs in cols]))
        chains = []
        for q, k, v, g, beta, states, _ in operands:
            q = q * lax.rsqrt(jnp.sum(q * q, axis=-1, keepdims=True) + 1e-12) * LANES ** -0.5
            k = k * lax.rsqrt(jnp.sum(k * k, axis=-1, keepdims=True) + 1e-12)
            chains.append(_gdn_pair_chunk(q, k, v, g, beta, states, consts))
        results = _lockstep(chains)
        for pi, (o, new_states) in enumerate(results):
            z = operands[pi][6]
            for i in range(2):
                h = 2 * pi + i
                s_ref[h] = new_states[i]
                oh = o[i * c:(i + 1) * c]
                ms = jnp.mean(oh * oh, axis=-1, keepdims=True)
                o_ref[sl, h * LANES:(h + 1) * LANES] = (
                    oh * lax.rsqrt(ms + NORM_EPS) * nw * _silu(z[i])).astype(o_ref.dtype)
        return carry

    lax.fori_loop(0, t // c, body, 0)

    @pl.when(l == pl.num_programs(2) - 1)
    def _():
        so_ref[...] = s_ref[...]


def _gdn(p, conv_w, conv_state, par, norm_w, s0, batch, seq, heads, tile):
    tile = min(tile, seq)
    nl = seq // tile
    cw = conv_w.shape[0]
    hb = GDN_HEADS_PER_STEP
    assert heads % hb == 0 and hb % 2 == 0
    groups = heads // hb
    width = hb * LANES
    pblk = lambda off: pl.BlockSpec((tile, width), lambda b, h, l, off=off: (b * nl + l, off * groups + h))
    wblk = lambda off: pl.BlockSpec((cw, width), lambda b, h, l, off=off: (0, off * groups + h))
    cblk = lambda off: pl.BlockSpec((None, cw - 1, width), lambda b, h, l, off=off: (b, 0, off * groups + h))
    st_spec = pl.BlockSpec((None, hb, LANES, LANES), lambda b, h, l: (b, h, 0, 0))
    tbuf = pltpu.VMEM((tile, width), F32)
    return pl.pallas_call(
        functools.partial(_gdn_kernel, heads=heads, conv_w=cw),
        grid=(batch, groups, nl),
        in_specs=[pblk(0), pblk(1), pblk(2), pblk(3),
                  pl.BlockSpec((tile, LANES), lambda b, h, l: (b * nl + l, 4 * heads)),
                  wblk(0), wblk(1), wblk(2), cblk(0), cblk(1), cblk(2),
                  pl.BlockSpec(par.shape, lambda b, h, l: (0, 0)),
                  pl.BlockSpec((1, LANES), lambda b, h, l: (0, 0)),
                  st_spec],
        out_specs=[pl.BlockSpec((tile, width), lambda b, h, l: (b * nl + l, h)), st_spec],
        out_shape=[jax.ShapeDtypeStruct((batch * seq, heads * LANES), BF16), jax.ShapeDtypeStruct(s0.shape, F32)],
        scratch_shapes=[pltpu.VMEM((hb, LANES, LANES), F32),
                        pltpu.VMEM((3, 16, width), F32),
                        tbuf, tbuf, tbuf],
        compiler_params=_params(("parallel", "parallel", "arbitrary")),
        name="gdn",
    )(p, p, p, p, p, conv_w, conv_w, conv_w, conv_state, conv_state, conv_state, par, norm_w, s0)


def _pad_cols(w, total):
    return jnp.pad(w, ((0, 0),) * (w.ndim - 1) + ((0, total - w.shape[-1]),))


def _ab_layout(a_cols, wb, lora_w, lora_a, lora_g):
    assert lora_w == LANES and lora_a == LANES and lora_g <= 512
    xg_off = -(-(3 * wb + 2 * LANES) // 512) * 512
    b_width = xg_off + 512
    assert a_cols % 512 == 0
    return xg_off, b_width


def _rwkv_cols(t, wb, xg_off, b_width, lora_g):
    head = t[..., :3 * wb + 2 * LANES]
    tail = t[..., 3 * wb + 2 * LANES:]
    z = lambda n: jnp.zeros(t.shape[:-1] + (n,), t.dtype)
    return jnp.concatenate([head, z(xg_off - head.shape[-1]), tail, z(b_width - xg_off - lora_g)], axis=-1)


def _trunk(x, mod, s_hgrn, s_rwkv, s_shift, s_gdn, s_conv, wts, tiles):
    batch, seq, d = x.shape
    m = batch * seq
    x2 = x.reshape(m, d)
    tm, t_hgrn, t_rwkv, t_gdn = tiles
    outs = {}
    depth = mod.shape[0]
    for layer in range(depth):
        j = layer // 2
        vecs6 = [mod[layer, :, i * d:(i + 1) * d].reshape(batch, 1, d) for i in range(6)]
        sh1, sc1, g1, sh2, sc2, g2 = vecs6
        if layer % 2 == 0:
            w = wts["ab"][j]
            p = _norm_matmul(x2, sc1, sh1, w["w_in"], seq, tm, 512)
            ha, wb = w["ha"], w["wb"]
            oa, st_a = _hgrn(p, w["lb"], w["hgrn_norm_w"], jnp.swapaxes(s_hgrn[j], -1, -2), batch, seq, ha, t_hgrn)
            pairs = wb // LANES
            hb = s_rwkv.shape[2]
            nb = s_rwkv.shape[-1]
            sp = s_rwkv[j].reshape(batch, pairs, 2, nb, nb)
            zero = jnp.zeros_like(sp[:, :, 0])
            s0 = jnp.concatenate([jnp.concatenate([sp[:, :, 0], zero], -1),
                                  jnp.concatenate([zero, sp[:, :, 1]], -1)], -2)
            shift = _rwkv_cols(s_shift[j], wb, w["xg_off"], w["b_width"], w["lora_g"])
            ob, st_b = _rwkv(p, shift, w["vecs"], w["mus"], w["w2"], w["a2"], w["g2"], s0,
                             batch, seq, pairs, w["a_cols"], t_rwkv)
            mix = jnp.concatenate([oa, ob], axis=-1)
            outs.setdefault("hgrn", []).append(jnp.swapaxes(st_a, -1, -2))
            outs.setdefault("rwkv", []).append(
                jnp.stack([st_b[:, :, :nb, :nb], st_b[:, :, nb:, nb:]], axis=2).reshape(batch, hb, nb, nb))
            last = p.reshape(batch, seq, -1)[:, seq - 1:, w["a_cols"]:]
            n_head = 3 * wb + 2 * LANES
            outs.setdefault("shift", []).append(
                jnp.concatenate([last[..., :n_head], last[..., w["xg_off"]:w["xg_off"] + w["lora_g"]]], axis=-1))
            w_out = w["w_out"]
        else:
            w = wts["gdn"][j]
            p = _norm_matmul(x2, sc1, sh1, w["w_in"], seq, tm, 512)
            hc = w["hc"]
            mix, st_c = _gdn(p, w["conv_w"], s_conv[j], w["par"], w["norm_w"], s_gdn[j], batch, seq, hc, t_gdn)
            outs.setdefault("gdn", []).append(st_c)
            cw = w["conv_w"].shape[0]
            raw = p.reshape(batch, seq, -1)[:, :, :3 * hc * LANES]
            prev = jnp.concatenate([s_conv[j], raw[:, max(seq - (cw - 1), 0):]], axis=1)
            outs.setdefault("conv", []).append(prev[:, -(cw - 1):])
            w_out = w["w_out"]
        x2 = _out_residual(mix, w_out, x2, g1, seq, tm, 512)
        x2 = _mlp(x2, sc2, sh2, g2, wts["mlp_w1"], wts["mlp_w2"], layer, seq, min(tm, 512), 512, 256)
    x2 = _final_norm(x2, wts["final_w"], 256)
    st = lambda name: jnp.stack(outs[name])
    return x2.reshape(batch, seq, d), st("hgrn"), st("rwkv"), st("shift"), st("gdn"), st("conv")


def kernel(x_prompt, x_sample, c_prompt, c_sample, state_hgrn, state_rwkv, state_rwkv_shift, state_gdn,
           state_gdn_conv, ada_w, ada_b, mlp_w1, mlp_w2, final_norm_w, ab_w_in, ab_w_out, hgrn_lb_logits,
           hgrn_norm_w, rwkv_mu, rwkv_w0, rwkv_w2, rwkv_a0, rwkv_a2, rwkv_g2, rwkv_k_k, rwkv_k_a, rwkv_r_k,
           rwkv_ln_w, rwkv_ln_b, gdn_w_in, gdn_w_out, gdn_conv_w, gdn_a_log, gdn_dt_bias, gdn_norm_w):
    d = x_prompt.shape[-1]
    nb_p, nb_s = x_prompt.shape[0], x_sample.shape[0]
    n_ab, n_c = ab_w_in.shape[0], gdn_w_in.shape[0]
    ha = state_hgrn.shape[2]
    a_cols = 4 * ha * LANES
    wb = rwkv_w0.shape[1]
    lora_w, lora_a, lora_g = rwkv_w2.shape[1], rwkv_a2.shape[1], rwkv_g2.shape[1]
    xg_off, b_width = _ab_layout(a_cols, wb, lora_w, lora_a, lora_g)
    hc = state_gdn.shape[2]

    rows = nb_p + nb_s
    rows_pad = -(-rows // 8) * 8
    c_all = jnp.pad(jnp.concatenate([c_prompt, c_sample], axis=0), ((0, rows_pad - rows), (0, 0)))
    mod = _adaln(c_all, ada_w, ada_b)

    lbs = jnp.cumsum(jax.nn.softmax(hgrn_lb_logits.astype(F32), axis=0), axis=0)
    wts = {"ab": [], "gdn": [], "final_w": final_norm_w.reshape(1, d),
           "mlp_w1": mlp_w1.astype(BF16), "mlp_w2": mlp_w2.astype(BF16)}
    for j in range(n_ab):
        hole = ((a_cols + 3 * wb + 2 * LANES) // LANES, (a_cols + xg_off) // LANES)
        w_in = _cast_cols(ab_w_in, j, a_cols + b_width, hole, hole[1] - hole[0])
        mu = _rwkv_cols(rwkv_mu[j][None], wb, xg_off, b_width, lora_g)[0]
        vec_rows = [mu[:wb], mu[wb:2 * wb], mu[2 * wb:3 * wb], rwkv_w0[j], rwkv_a0[j], rwkv_k_k[j], rwkv_k_a[j],
                    rwkv_r_k[j].reshape(-1), rwkv_ln_w[j], rwkv_ln_b[j]]
        vecs = jnp.pad(jnp.stack(vec_rows), ((0, 16 - len(vec_rows)), (0, 0)))
        mus = jnp.stack([_pad_cols(mu[3 * wb:3 * wb + LANES], 512), _pad_cols(mu[3 * wb + LANES:3 * wb + 2 * LANES], 512),
                         mu[xg_off:xg_off + 512]])
        wts["ab"].append({
            "w_in": w_in, "w_out": ab_w_out[j].astype(BF16), "lb": lbs[j][None],
            "hgrn_norm_w": hgrn_norm_w[j][None], "vecs": vecs, "mus": jnp.pad(mus, ((0, 5), (0, 0))),
            "w2": rwkv_w2[j].astype(BF16), "a2": rwkv_a2[j].astype(BF16),
            "g2": jnp.pad(rwkv_g2[j], ((0, 512 - lora_g), (0, 0))).astype(BF16),
            "ha": ha, "wb": wb, "a_cols": a_cols, "xg_off": xg_off, "b_width": b_width, "lora_g": lora_g})
    for j in range(n_c):
        cols = gdn_w_in.shape[-1]
        cols_pad = -(-cols // 512) * 512
        par = jnp.zeros((8, LANES), F32)
        par = par.at[0, hc:2 * hc].set(gdn_a_log[j]).at[1, hc:2 * hc].set(gdn_dt_bias[j])
        wts["gdn"].append({
            "w_in": _cast_cols(gdn_w_in, j, cols_pad), "w_out": gdn_w_out[j].astype(BF16),
            "conv_w": gdn_conv_w[j], "par": par, "norm_w": gdn_norm_w[j][None], "hc": hc})

    zeros = lambda s: jnp.zeros((s.shape[0], nb_p) + s.shape[2:], x_prompt.dtype)
    y_p, hg_p, rw_p, sh_p, gd_p, cv_p = _trunk(
        x_prompt, mod[:, :nb_p], zeros(state_hgrn), zeros(state_rwkv), zeros(state_rwkv_shift),
        zeros(state_gdn), zeros(state_gdn_conv), wts, PROMPT_TILES)
    y_s, hg_s, rw_s, sh_s, gd_s, cv_s = _trunk(
        x_sample, mod[:, nb_p:rows], state_hgrn, state_rwkv, state_rwkv_shift, state_gdn, state_gdn_conv,
        wts, SAMPLE_TILES)
    return (y_p, y_s, hg_p, rw_p, sh_p, gd_p, cv_p, hg_s, rw_s, sh_s, gd_s, cv_s)
```

```python
import functools
import math

import jax
import jax.numpy as jnp
from jax import lax
from jax.experimental import pallas as pl
from jax.experimental.pallas import tpu as pltpu

F32 = jnp.float32
BF16 = jnp.bfloat16

NORM_EPS = 1e-6
RWKV_LN_EPS = 64e-5
CHUNK = 64
DIAG = 8
LANES = 128
VMEM_LIMIT = 56 * 1024 * 1024
MLP_VMEM_LIMIT = 60 * 1024 * 1024
PROMPT_TILES = (1024, 512, 256, 256)
SAMPLE_TILES = (512, 64, 64, 64)
HGRN_HEADS_PER_STEP = 8
RWKV_PAIRS_PER_STEP = 8
GDN_HEADS_PER_STEP = 16


def _mm(a, b, ca, cb):
    return lax.dot_general(a.astype(BF16), b.astype(BF16), (((ca,), (cb,)), ((), ())), preferred_element_type=F32)


def _nn(a, b):
    return _mm(a, b, 1, 0)


def _nt(a, b):
    return _mm(a, b, 1, 1)


def _tn(a, b):
    return _mm(a, b, 0, 0)


def _sigmoid(x):
    return 1.0 / (1.0 + jnp.exp(-x))


def _silu(x):
    return x * _sigmoid(x)


def _softplus(x):
    return jnp.maximum(x, 0.0) + jnp.log(1.0 + jnp.exp(-jnp.abs(x)))


def _iota(shape, dim):
    return lax.broadcasted_iota(jnp.int32, shape, dim)


def _split2(x):
    hi = x.astype(BF16)
    return hi, (x - hi.astype(F32)).astype(BF16)


def _split3(x):
    hi = x.astype(BF16)
    r = x - hi.astype(F32)
    lo = r.astype(BF16)
    return hi, lo, (r - lo.astype(F32)).astype(BF16)


def _mm_hi(a, b, a_parts=None, b_parts=None):
    a_hi, a_lo = a_parts if a_parts is not None else _split2(a)
    b_hi, b_lo = b_parts if b_parts is not None else _split2(b)
    return jnp.dot(jnp.concatenate([a_hi, a_lo, a_hi], axis=1), jnp.concatenate([b_hi, b_hi, b_lo], axis=0),
                   preferred_element_type=F32)


def _mask_mm_left(mask3, x):
    return jnp.dot(mask3, jnp.concatenate(_split3(x), axis=0), preferred_element_type=F32)


def _nilpotent_solve(m, x, stages, hi_lo=True):
    n = m.shape[0]
    mm = _mm_hi if hi_lo else _nn
    for k in range(stages):
        last = k == stages - 1
        y = mm(m, x if last else jnp.concatenate([m, x], axis=1))
        yield
        if last:
            x = x + y
        else:
            m = y[:, :n]
            x = x + y[:, n:]
    return x


def _lockstep(generators):
    generators = list(generators)
    results = [None] * len(generators)
    live = list(range(len(generators)))
    while live:
        for i in list(live):
            try:
                next(generators[i])
            except StopIteration as stop:
                results[i] = stop.value
                live.remove(i)
    return results


def _params(sem, vmem=VMEM_LIMIT):
    return pltpu.CompilerParams(dimension_semantics=sem, vmem_limit_bytes=vmem)


def _rows_per_tile(rows_per_batch, tile):
    if tile <= rows_per_batch:
        assert rows_per_batch % tile == 0
        return 1
    assert tile % rows_per_batch == 0
    return tile // rows_per_batch


def _adaln_kernel(c_ref, w_ref, b_ref, o_ref):
    c = _silu(c_ref[...]).astype(BF16)
    o_ref[...] = jnp.dot(c, w_ref[...].astype(BF16), preferred_element_type=F32) + b_ref[...]


def _adaln(c, ada_w, ada_b):
    depth, d, n = ada_w.shape
    rows = c.shape[0]
    tn = 512 if n % 512 == 0 else n
    return pl.pallas_call(
        _adaln_kernel,
        grid=(depth, n // tn),
        in_specs=[pl.BlockSpec((rows, d), lambda l, j: (0, 0)),
                  pl.BlockSpec((None, d, tn), lambda l, j: (l, 0, j)),
                  pl.BlockSpec((None, 1, tn), lambda l, j: (l, 0, j))],
        out_specs=pl.BlockSpec((None, rows, tn), lambda l, j: (l, 0, j)),
        out_shape=jax.ShapeDtypeStruct((depth, rows, n), F32),
        compiler_params=_params(("parallel", "parallel")),
        name="adaln",
    )(c, ada_w, ada_b.reshape(depth, 1, n))


def _cast_cols_kernel(w_ref, o_ref, *, hole, shift, n_src):
    j = pl.program_id(0)
    src_col = jnp.where(j < hole[0], j, j - shift) * LANES + _iota((LANES, 1), 0)
    keep = (src_col < n_src) & jnp.logical_not((j >= hole[0]) & (j < hole[1]))
    o_ref[...] = jnp.where(keep, w_ref[...], 0.0).T.astype(o_ref.dtype)


def _cast_cols(w, layer, n_out, hole=(0, 0), shift=0):
    _, k, n_src = w.shape
    if hole[0] == hole[1]:
        hole = (n_out // LANES, n_out // LANES)
    last = (n_src - 1) // LANES
    return pl.pallas_call(
        functools.partial(_cast_cols_kernel, hole=hole, shift=shift, n_src=n_src),
        grid=(n_out // LANES,),
        in_specs=[pl.BlockSpec((None, LANES, k),
                               lambda j: (layer, jnp.clip(jnp.where(j < hole[0], j, j - shift), 0, last), 0))],
        out_specs=pl.BlockSpec((k, LANES), lambda j: (0, j)),
        out_shape=jax.ShapeDtypeStruct((k, n_out), BF16),
        compiler_params=_params(("parallel",)),
        name="cast_cols",
    )(jnp.swapaxes(w, 1, 2))


def _modulated_norm(x_ref, sc_ref, sh_ref, h_ref, groups):
    rows = x_ref.shape[0] // groups
    step = min(rows, 128)

    for gi in range(groups):
        sc = 1.0 + sc_ref[gi]
        sh = sh_ref[gi]

        def body(i, carry, gi=gi, sc=sc, sh=sh):
            sl = pl.ds(pl.multiple_of(gi * rows + i * step, step), step)
            x = x_ref[sl, :]
            ms = jnp.mean(x * x, axis=-1, keepdims=True)
            h_ref[sl, :] = (x * lax.rsqrt(ms + NORM_EPS) * sc + sh).astype(BF16)
            return carry

        lax.fori_loop(0, rows // step, body, 0)


def _norm_mm_kernel(x_ref, sc_ref, sh_ref, w_ref, o_ref, h_ref, *, groups):
    @pl.when(pl.program_id(1) == 0)
    def _():
        _modulated_norm(x_ref, sc_ref, sh_ref, h_ref, groups)

    o_ref[...] = jnp.dot(h_ref[...], w_ref[...], preferred_element_type=F32).astype(o_ref.dtype)


def _norm_matmul(x, scale, shift, w, rows_per_batch, tm, tn):
    m, d = x.shape
    n = w.shape[1]
    tm = min(tm, m)
    groups = _rows_per_tile(rows_per_batch, tm)
    bidx = (lambda i, j: (i * tm // rows_per_batch, 0, 0)) if groups == 1 else (lambda i, j: (i, 0, 0))
    return pl.pallas_call(
        functools.partial(_norm_mm_kernel, groups=groups),
        grid=(m // tm, n // tn),
        in_specs=[pl.BlockSpec((tm, d), lambda i, j: (i, 0), pipeline_mode=pl.Buffered(1)),
                  pl.BlockSpec((groups, 1, d), bidx),
                  pl.BlockSpec((groups, 1, d), bidx),
                  pl.BlockSpec((d, tn), lambda i, j: (0, j))],
        out_specs=pl.BlockSpec((tm, tn), lambda i, j: (i, j)),
        out_shape=jax.ShapeDtypeStruct((m, n), F32),
        scratch_shapes=[pltpu.VMEM((tm, d), BF16)],
        compiler_params=_params(("parallel", "arbitrary")),
        name="norm_matmul",
    )(x, scale, shift, w)


def _out_res_kernel(a_ref, w_ref, x_ref, g_ref, o_ref, *, groups):
    acc = jnp.dot(a_ref[...], w_ref[...], preferred_element_type=F32)
    rows = acc.shape[0] // groups
    for gi in range(groups):
        sl = slice(gi * rows, (gi + 1) * rows)
        o_ref[sl, :] = x_ref[sl, :] + g_ref[gi] * acc[sl, :]


def _out_residual(a, w, x, gate, rows_per_batch, tm, tn):
    m, k = a.shape
    n = w.shape[1]
    tm = min(tm, m)
    groups = _rows_per_tile(rows_per_batch, tm)
    bidx = (lambda i, j: (i * tm // rows_per_batch, 0, j)) if groups == 1 else (lambda i, j: (i, 0, j))
    return pl.pallas_call(
        functools.partial(_out_res_kernel, groups=groups),
        grid=(m // tm, n // tn),
        in_specs=[pl.BlockSpec((tm, k), lambda i, j: (i, 0)),
                  pl.BlockSpec((k, tn), lambda i, j: (0, j)),
                  pl.BlockSpec((tm, tn), lambda i, j: (i, j)),
                  pl.BlockSpec((groups, 1, tn), bidx)],
        out_specs=pl.BlockSpec((tm, tn), lambda i, j: (i, j)),
        out_shape=jax.ShapeDtypeStruct((m, n), F32),
        compiler_params=_params(("parallel", "parallel")),
        name="out_residual",
    )(a, w, x, gate)


def _mlp_kernel(x_ref, sc_ref, sh_ref, xr_ref, g_ref, w1_ref, w2_ref, o_ref, h_ref, u_ref, *, groups, nf):
    j = pl.program_id(1)

    @pl.when(j == 0)
    def _():
        _modulated_norm(x_ref, sc_ref, sh_ref, h_ref, groups)

    @pl.when(j < nf)
    def _():
        u = jnp.dot(h_ref[...], w1_ref[...], preferred_element_type=F32)
        u_ref[j] = jnp.square(jnp.maximum(u, 0.0)).astype(BF16)

    @pl.when(j >= nf)
    def _():
        tf = u_ref.shape[2]
        acc = jnp.dot(u_ref[0], w2_ref[0:tf, :], preferred_element_type=F32)
        for f in range(1, nf):
            acc += jnp.dot(u_ref[f], w2_ref[f * tf:(f + 1) * tf, :], preferred_element_type=F32)
        rows = acc.shape[0] // groups
        for gi in range(groups):
            sl = slice(gi * rows, (gi + 1) * rows)
            o_ref[sl, :] = xr_ref[sl, :] + g_ref[gi] * acc[sl, :]


def _mlp(x, scale, shift, gate, w1, w2, layer, rows_per_batch, tm, tf, tn):
    m, d = x.shape
    ff = w1.shape[2]
    tm = min(tm, m)
    nf, nn = ff // tf, d // tn
    groups = _rows_per_tile(rows_per_batch, tm)
    batch_of = (lambda i: i * tm // rows_per_batch) if groups == 1 else (lambda i: i)
    vec = pl.BlockSpec((groups, 1, d), lambda i, j: (batch_of(i), 0, 0))
    col = lambda j: jnp.maximum(j - nf, 0)
    return pl.pallas_call(
        functools.partial(_mlp_kernel, groups=groups, nf=nf),
        grid=(m // tm, nf + nn),
        in_specs=[pl.BlockSpec((tm, d), lambda i, j: (i, 0), pipeline_mode=pl.Buffered(1)),
                  vec, vec,
                  pl.BlockSpec((tm, tn), lambda i, j: (i, col(j))),
                  pl.BlockSpec((groups, 1, tn), lambda i, j: (batch_of(i), 0, col(j))),
                  pl.BlockSpec((None, d, tf), lambda i, j: (layer, 0, jnp.minimum(j, nf - 1))),
                  pl.BlockSpec((None, ff, tn), lambda i, j: (layer, 0, col(j)))],
        out_specs=pl.BlockSpec((tm, tn), lambda i, j: (i, col(j))),
        out_shape=jax.ShapeDtypeStruct((m, d), F32),
        scratch_shapes=[pltpu.VMEM((tm, d), BF16), pltpu.VMEM((nf, tm, tf), BF16)],
        compiler_params=_params(("parallel", "arbitrary"), MLP_VMEM_LIMIT),
        name="mlp",
    )(x, scale, shift, x, gate, w1, w2)


def _final_norm_kernel(x_ref, w_ref, o_ref):
    x = x_ref[...]
    ms = jnp.mean(x * x, axis=-1, keepdims=True)
    o_ref[...] = x * lax.rsqrt(ms + NORM_EPS) * w_ref[...]


def _final_norm(x, w, tm):
    m, d = x.shape
    tm = min(tm, m)
    return pl.pallas_call(
        _final_norm_kernel,
        grid=(m // tm,),
        in_specs=[pl.BlockSpec((tm, d), lambda i: (i, 0)), pl.BlockSpec((1, d), lambda i: (0, 0))],
        out_specs=pl.BlockSpec((tm, d), lambda i: (i, 0)),
        out_shape=jax.ShapeDtypeStruct((m, d), F32),
        compiler_params=_params(("parallel",)),
        name="final_norm",
    )(x, w)


def _hgrn_consts(c):
    row = _iota((c, c), 0)
    col = _iota((c, c), 1)
    ltri = (_iota((c, 3 * c), 0) >= _iota((c, 3 * c), 1) % c).astype(BF16)
    levels = []
    gs = 2 * DIAG
    while gs <= c:
        half = gs // 2
        pair = (row // gs == col // gs) & (row % gs >= half) & (col % gs < half)
        levels.append((gs, pair))
        gs *= 2
    rix = _iota((c, 1), 0)
    return ltri, levels, rix


def _hgrn_chunk(q, k, v, g, st, consts):
    ltri, levels, rix = consts
    c = q.shape[0]
    gc = _mask_mm_left(ltri, g)
    yield
    lhs, rhs = [q * jnp.exp(gc)], [st]
    for gs, _ in levels:
        half = gs // 2
        ref = jnp.concatenate(
            [jnp.broadcast_to(gc[m0 * gs + half - 1:m0 * gs + half, :], (gs, gc.shape[1])) for m0 in range(c // gs)],
            axis=0)
        is_q = (rix % gs) >= half
        lhs.append(jnp.where(is_q, q * jnp.exp(gc - ref), 0.0))
        rhs.append(jnp.where(is_q, 0.0, k * jnp.exp(ref - gc)))
        rhs.append(jnp.zeros((LANES - c, k.shape[1]), F32))
    prod = _nt(jnp.concatenate(lhs, axis=0), jnp.concatenate(rhs, axis=0))
    gl = gc[c - 1:c, :]
    upd = _tn(v, k * jnp.exp(gl - gc))
    yield
    att = jnp.zeros((c, c), F32)
    for li, (_, pair) in enumerate(levels):
        att = att + jnp.where(pair, prod[(li + 1) * c:(li + 2) * c, (li + 1) * LANES:(li + 1) * LANES + c], 0.0)
    o = prod[:c, :LANES] + _nn(att, v)
    in_block = lambda x, dist: pltpu.roll(x.reshape(c // DIAG, DIAG, x.shape[1]), dist, 1).reshape(x.shape)
    for dist in range(DIAG):
        if dist == 0:
            kd, gd, vd = k, gc, v
        else:
            kd, gd, vd = in_block(k, dist), in_block(gc, dist), in_block(v, dist)
        w = jnp.sum(q * kd * jnp.exp(gc - gd), axis=-1, keepdims=True)
        o = o + jnp.where((rix % DIAG) >= dist, w, 0.0) * vd
    yield
    return o, st * jnp.exp(gl) + upd


def _hgrn_kernel(q_ref, f_ref, i_ref, g_ref, lb_ref, nw_ref, s0_ref, o_ref, so_ref, s_ref, *, dk):
    l = pl.program_id(2)

    @pl.when(l == 0)
    def _():
        s_ref[...] = s0_ref[...]

    t, width = q_ref.shape
    c = min(CHUNK, t)
    consts = _hgrn_consts(c)
    lb = lb_ref[...]
    nw = nw_ref[...]

    def body(ci, carry):
        sl = pl.ds(pl.multiple_of(ci * c, c), c)
        f = lb + (1.0 - lb) * _sigmoid(f_ref[sl, :])
        q = _silu(q_ref[sl, :]) * dk ** -0.5
        k = 1.0 - f
        v = i_ref[sl, :]
        g = jnp.log(f)
        gate = _silu(g_ref[sl, :])
        heads = [slice(h * dk, (h + 1) * dk) for h in range(width // dk)]
        results = _lockstep(_hgrn_chunk(q[:, hs], k[:, hs], v[:, hs], g[:, hs], s_ref[h], consts)
                            for h, hs in enumerate(heads))
        for h, (o, st) in enumerate(results):
            s_ref[h] = st
            ms = jnp.mean(o * o, axis=-1, keepdims=True)
            o_ref[sl, heads[h]] = (o * lax.rsqrt(ms + NORM_EPS) * nw * gate[:, heads[h]]).astype(o_ref.dtype)
        return carry

    lax.fori_loop(0, t // c, body, 0)

    @pl.when(l == pl.num_programs(2) - 1)
    def _():
        so_ref[...] = s_ref[...]


def _hgrn(p, lb, norm_w, s0t, batch, seq, heads, tile):
    dk = LANES
    tile = min(tile, seq)
    nl = seq // tile
    hb = HGRN_HEADS_PER_STEP
    assert heads % hb == 0
    groups = heads // hb
    width = hb * dk
    blk = lambda off: pl.BlockSpec((tile, width), lambda b, h, l, off=off: (b * nl + l, off * groups + h))
    st_spec = pl.BlockSpec((None, hb, dk, dk), lambda b, h, l: (b, h, 0, 0))
    return pl.pallas_call(
        functools.partial(_hgrn_kernel, dk=dk),
        grid=(batch, groups, nl),
        in_specs=[blk(0), blk(1), blk(2), blk(3),
                  pl.BlockSpec((1, width), lambda b, h, l: (0, h)),
                  pl.BlockSpec((1, dk), lambda b, h, l: (0, 0)),
                  st_spec],
        out_specs=[pl.BlockSpec((tile, width), lambda b, h, l: (b * nl + l, h)), st_spec],
        out_shape=[jax.ShapeDtypeStruct((batch * seq, heads * dk), BF16),
                   jax.ShapeDtypeStruct(s0t.shape, F32)],
        scratch_shapes=[pltpu.VMEM((hb, dk, dk), F32)],
        compiler_params=_params(("parallel", "parallel", "arbitrary")),
        name="hgrn2",
    )(p, p, p, p, lb, norm_w, s0t)


def _rwkv_consts(c):
    n = 4 * c
    row = _iota((n, n), 0)
    col = _iota((n, n), 1)
    same_head = ((row // c) % 2) == ((col // c) % 2)
    strict = (row % c) > (col % c)
    incl = (row % c) >= (col % c)
    keep = same_head & (strict | ((row >= 2 * c) & incl))
    ltri = (_iota((c, 3 * c), 0) >= _iota((c, 3 * c), 1) % c).astype(BF16)
    head0 = _iota((1, LANES), 1) < (LANES // 2)
    rs = _iota((LANES, LANES), 0)
    cs = _iota((LANES, LANES), 1)
    bdiag = (rs // (LANES // 2)) == (cs // (LANES // 2))
    return keep, ltri, head0, bdiag


def _rwkv_chunk(r, lg, k, v, a, b, s, consts):
    keep, ltri, head0, bdiag = consts
    c = r.shape[0]
    gc = _mask_mm_left(ltri, lg)
    yield
    e_incl = jnp.exp(gc)
    e_inv = jnp.exp(-gc)
    at = a * jnp.exp(gc - lg)
    rt = r * e_incl
    bt = b * e_inv
    kt = k * e_inv
    split = lambda x: [jnp.where(head0, x, 0.0), jnp.where(head0, 0.0, x)]
    lhs = jnp.concatenate(split(at) + split(rt), axis=0)
    rhs = jnp.concatenate([bt, bt, kt, kt, s], axis=0)
    pls = _nt(lhs, rhs)
    yield
    p = jnp.where(keep, pls[:, :4 * c], 0.0)
    ls = pls[:, 4 * c:]
    vst = jnp.concatenate(split(v), axis=0)
    pv = _nn(p[:, 2 * c:], vst)
    yield
    ust = yield from _nilpotent_solve(p[:2 * c, :2 * c], ls[:2 * c] + pv[:2 * c], int(math.log2(c)), hi_lo=False)
    ost = ls[2 * c:] + pv[2 * c:] + _nn(p[2 * c:, :2 * c], ust)
    yield
    o = ost[:c] + ost[c:]
    u = ust[:c] + ust[c:]
    gl = gc[c - 1:c, :]
    e_tail = jnp.exp(gl - gc)
    upd = _tn(jnp.concatenate([u, v], axis=0), jnp.concatenate([b * e_tail, k * e_tail], axis=0))
    yield
    return o, s * jnp.exp(gl) + jnp.where(bdiag, upd, 0.0)


def _rwkv_kernel(xr_ref, xk_ref, xv_ref, xw_ref, xa_ref, xg_ref,
                 hr_ref, hk_ref, hv_ref, hw_ref, ha_ref, hg_ref,
                 vec_ref, mus_ref, w2_ref, a2_ref, g2_ref, s0_ref,
                 o_ref, so_ref,
                 s_ref, buf_ref, bufs_ref, bufg_ref, r_ref, lg_ref, k_ref, v_ref, a_ref, b_ref, y_ref):
    l = pl.program_id(2)
    t, width = xr_ref.shape
    c = min(CHUNK, t)

    @pl.when(l == 0)
    def _():
        s_ref[...] = s0_ref[...]
        for i, h_ref in enumerate((hr_ref, hk_ref, hv_ref)):
            buf_ref[i, 0:1, :] = h_ref[...]
        for i, h_ref in enumerate((hw_ref, ha_ref)):
            bufs_ref[i, 0:1, :] = h_ref[...]
        bufg_ref[0:1, :] = hg_ref[...]

    first_row = _iota((t, 1), 0) == 0

    def shifted(carry, x_ref, mu):
        x = x_ref[...]
        prev = jnp.where(first_row, carry[0:1, :], pltpu.roll(x, 1, 0))
        carry[0:1, :] = x[t - 1:t, :]
        return x + (prev - x) * mu

    vec = vec_ref[...]
    mus = mus_ref[...]
    r = shifted(buf_ref.at[0], xr_ref, vec[0:1])
    k = shifted(buf_ref.at[1], xk_ref, vec[1:2])
    v = shifted(buf_ref.at[2], xv_ref, vec[2:3])
    xw = shifted(bufs_ref.at[0], xw_ref, mus[0:1, :LANES])
    xa = shifted(bufs_ref.at[1], xa_ref, mus[1:2, :LANES])
    xg = shifted(bufg_ref, xg_ref, mus[2:3])

    w0, a0, k_k, k_a, r_k, ln_w, ln_b = (vec[i:i + 1] for i in range(3, 10))
    half = LANES // 2
    ones2 = ((_iota((2 * LANES, LANES), 0) % LANES) // half == _iota((2 * LANES, LANES), 1) // half).astype(BF16)

    def head_sums(x):
        outs = []
        for blk in range(width // LANES):
            outs.append(jnp.dot(jnp.concatenate(_split2(x[:, blk * LANES:(blk + 1) * LANES]), axis=1), ones2,
                                preferred_element_type=F32))
        return jnp.concatenate(outs, axis=1)

    w_log = -_softplus(-(w0 + _nn(jnp.tanh(xw), w2_ref[...]))) - 0.5
    a_lr = _sigmoid(a0 + _nn(xa, a2_ref[...]))
    gate = _nn(_sigmoid(xg), g2_ref[...])
    kk = k * k_k
    kk = kk * lax.rsqrt(head_sums(kk * kk) + 1e-12)
    k = k * (1.0 + (a_lr - 1.0) * k_a)
    r_ref[...] = r
    lg_ref[...] = -jnp.exp(w_log)
    k_ref[...] = k
    v_ref[...] = v
    a_ref[...] = -kk
    b_ref[...] = kk * a_lr

    consts = _rwkv_consts(c)

    def body(ci, carry):
        sl = pl.ds(pl.multiple_of(ci * c, c), c)
        operands = []
        for pi in range(width // LANES):
            cols = slice(pi * LANES, (pi + 1) * LANES)
            operands.append((r_ref[sl, cols], lg_ref[sl, cols], k_ref[sl, cols], v_ref[sl, cols],
                             a_ref[sl, cols], b_ref[sl, cols], s_ref[pi]))
        results = _lockstep(_rwkv_chunk(*ops, consts) for ops in operands)
        for pi, (o, s_new) in enumerate(results):
            s_ref[pi] = s_new
            y_ref[sl, pi * LANES:(pi + 1) * LANES] = o
        return carry

    lax.fori_loop(0, t // c, body, 0)

    o = y_ref[...]
    inv_n = 1.0 / half
    mean = head_sums(o) * inv_n
    dev = o - mean
    var = head_sums(dev * dev) * inv_n
    o = dev * lax.rsqrt(var + RWKV_LN_EPS) * ln_w + ln_b
    o = o + head_sums(r * k * r_k) * v
    o_ref[...] = (o * gate).astype(o_ref.dtype)

    @pl.when(l == pl.num_programs(2) - 1)
    def _():
        so_ref[...] = s_ref[...]


def _rwkv(p, shift, vecs, mus, w2, a2, g2, s0, batch, seq, pairs, col0, tile):
    tile = min(tile, seq)
    nl = seq // tile
    wb = pairs * LANES
    pb = RWKV_PAIRS_PER_STEP
    width = pb * LANES
    groups = pairs // pb
    assert pairs % pb == 0 and col0 % width == 0
    cb0 = col0 // LANES
    gb0 = col0 // width
    xg_off = -(-(3 * wb + 2 * LANES) // 512) * 512
    assert (col0 + xg_off) % 512 == 0
    pblk = lambda off: pl.BlockSpec((tile, width), lambda b, h, l, off=off: (b * nl + l, gb0 + off * groups + h))
    pfix = lambda cb: pl.BlockSpec((tile, LANES), lambda b, h, l, cb=cb: (b * nl + l, cb))
    hblk = lambda off: pl.BlockSpec((None, 1, width), lambda b, h, l, off=off: (b, 0, off * groups + h))
    hfix = lambda cb: pl.BlockSpec((None, 1, LANES), lambda b, h, l, cb=cb: (b, 0, cb))
    st_spec = pl.BlockSpec((None, pb, LANES, LANES), lambda b, h, l: (b, h, 0, 0))
    col = lambda rows: pl.BlockSpec((rows, width), lambda b, h, l: (0, h))
    full = lambda a: pl.BlockSpec(a.shape, lambda b, h, l: (0,) * a.ndim)
    tbuf = pltpu.VMEM((tile, width), F32)
    return pl.pallas_call(
        _rwkv_kernel,
        grid=(batch, groups, nl),
        in_specs=[pblk(0), pblk(1), pblk(2), pfix(cb0 + 3 * pairs), pfix(cb0 + 3 * pairs + 1),
                  pl.BlockSpec((tile, 512), lambda b, h, l: (b * nl + l, (col0 + xg_off) // 512)),
                  hblk(0), hblk(1), hblk(2), hfix(3 * pairs), hfix(3 * pairs + 1),
                  pl.BlockSpec((None, 1, 512), lambda b, h, l: (b, 0, xg_off // 512)),
                  col(vecs.shape[0]), full(mus), col(w2.shape[0]), col(a2.shape[0]), col(g2.shape[0]), st_spec],
        out_specs=[pl.BlockSpec((tile, width), lambda b, h, l: (b * nl + l, h)), st_spec],
        out_shape=[jax.ShapeDtypeStruct((batch * seq, wb), BF16), jax.ShapeDtypeStruct(s0.shape, F32)],
        scratch_shapes=[pltpu.VMEM((pb, LANES, LANES), F32),
                        pltpu.VMEM((3, 8, width), F32),
                        pltpu.VMEM((2, 8, LANES), F32),
                        pltpu.VMEM((8, 512), F32),
                        tbuf, tbuf, tbuf, tbuf, tbuf, tbuf, tbuf],
        compiler_params=_params(("parallel", "parallel", "arbitrary")),
        name="rwkv7",
    )(p, p, p, p, p, p, shift, shift, shift, shift, shift, shift, vecs, mus, w2, a2, g2, s0)


def _gdn_consts(c):
    n = 2 * c
    row = _iota((n, n), 0)
    col = _iota((n, n), 1)
    same = (row // c) == (col // c)
    incl = same & ((row % c) >= (col % c))
    strict = same & ((row % c) > (col % c))
    r3 = _iota((n, 3 * n), 0)
    c3 = _iota((n, 3 * n), 1) % n
    ltri3 = (((r3 // c) == (c3 // c)) & ((r3 % c) >= (c3 % c))).astype(BF16)
    return ltri3, incl, strict, strict.astype(F32)


def _gdn_pair_chunk(q, k, v, g, beta, states, consts):
    ltri3, incl, strict, strict_f = consts
    n = q.shape[0]
    c = n // 2
    cum = _mask_mm_left(ltri3, jnp.concatenate([g * strict_f, jnp.broadcast_to(g, (n, LANES))], axis=1))
    yield
    gc = cum[:, n:]
    dec = jnp.exp(jnp.where(incl, cum[:, :n], -1e30))
    eg = jnp.exp(gc)
    kq = _nt(jnp.concatenate([k, q], axis=0), k)
    head0 = _iota((n, 1), 0) < c
    head0_2 = (_iota((2 * n, 1), 0) % n) < c
    by_head = lambda x, m: jnp.concatenate([jnp.where(m, x, 0.0), jnp.where(m, 0.0, x)], axis=1)
    ks = _nn(by_head(jnp.concatenate([k * eg, q * eg], axis=0), head0_2), jnp.concatenate(states, axis=0))
    yield
    a_mat = jnp.where(strict, beta * kq[:n] * dec, 0.0)
    v_new = yield from _nilpotent_solve(-a_mat, beta * (v - ks[:n]), int(math.log2(c)))
    o = ks[n:] + _nn(kq[n:] * dec, v_new)
    yield
    gl = jnp.where(head0, gc[c - 1:c, :], gc[n - 1:n, :])
    upd = _tn(k * jnp.exp(gl - gc), by_head(v_new, head0))
    yield
    new_states = [s * jnp.exp(gc[(hi + 1) * c - 1:(hi + 1) * c, :]) + upd[:, hi * LANES:(hi + 1) * LANES]
                  for hi, s in enumerate(states)]
    return o, new_states


def _gdn_kernel(q_ref, k_ref, v_ref, z_ref, ba_ref, cwq_ref, cwk_ref, cwv_ref, cq_ref, ck_ref, cv_ref,
                par_ref, nw_ref, s0_ref, o_ref, so_ref, s_ref, buf_ref, qs_ref, ks_ref, vs_ref,
                *, heads, conv_w):
    hg = pl.program_id(1)
    l = pl.program_id(2)
    t, width = q_ref.shape
    hb = width // LANES
    c = min(CHUNK, t)
    pad = 8
    hist = conv_w - 1
    streams = ((q_ref, cq_ref, cwq_ref, qs_ref), (k_ref, ck_ref, cwk_ref, ks_ref), (v_ref, cv_ref, cwv_ref, vs_ref))

    @pl.when(l == 0)
    def _():
        s_ref[...] = s0_ref[...]
        for i, (_, c_ref, _, _) in enumerate(streams):
            buf_ref[i, pad - hist:pad, :] = c_ref[...]

    for i, (x_ref, _, w_ref, dst_ref) in enumerate(streams):
        x = x_ref[...]
        w = w_ref[...]
        y = x * w[hist:hist + 1]
        for shift in range(1, conv_w):
            y = y + pltpu.roll(x, shift, 0) * w[hist - shift:hist - shift + 1]
        dst_ref[...] = _silu(y)
        buf_ref[i, pad:2 * pad, :] = x[0:pad]
        head = jnp.zeros((pad, width), F32)
        for j in range(conv_w):
            head = head + buf_ref[i, pad - hist + j:2 * pad - hist + j, :] * w[j:j + 1]
        dst_ref[0:pad, :] = _silu(head)
        buf_ref[i, pad - hist:pad, :] = x[t - hist:t]

    par = par_ref[...]
    lane = _iota((1, LANES), 1)
    consts = _gdn_consts(c)
    nw = nw_ref[...]

    def body(ci, carry):
        sl = pl.ds(pl.multiple_of(ci * c, c), c)
        bac = ba_ref[sl, :]
        bac = jnp.where(lane < heads, _sigmoid(bac), -jnp.exp(par[0:1]) * _softplus(bac + par[1:2]))
        column = lambda idx: jnp.sum(jnp.where(lane == idx, bac, 0.0), axis=-1, keepdims=True)
        operands = []
        for pi in range(hb // 2):
            local = (2 * pi, 2 * pi + 1)
            cols = [slice(h * LANES, (h + 1) * LANES) for h in local]
            stack = lambda ref: jnp.concatenate([ref[sl, cs] for cs in cols], axis=0)
            beta = jnp.concatenate([column(hg * hb + h) for h in local], axis=0)
            g = jnp.concatenate([column(heads + hg * hb + h) for h in local], axis=0)
            operands.append((stack(qs_ref), stack(ks_ref), stack(vs_ref), g, beta, [s_ref[h] for h in local],
                             [z_ref[sl, cs] for cs in cols]))
        chains = []
        for q, k, v, g, beta, states, _ in operands:
            q = q * lax.rsqrt(jnp.sum(q * q, axis=-1, keepdims=True) + 1e-12) * LANES ** -0.5
            k = k * lax.rsqrt(jnp.sum(k * k, axis=-1, keepdims=True) + 1e-12)
            chains.append(_gdn_pair_chunk(q, k, v, g, beta, states, consts))
        results = _lockstep(chains)
        for pi, (o, new_states) in enumerate(results):
            z = operands[pi][6]
            for i in range(2):
                h = 2 * pi + i
                s_ref[h] = new_states[i]
                oh = o[i * c:(i + 1) * c]
                ms = jnp.mean(oh * oh, axis=-1, keepdims=True)
                o_ref[sl, h * LANES:(h + 1) * LANES] = (
                    oh * lax.rsqrt(ms + NORM_EPS) * nw * _silu(z[i])).astype(o_ref.dtype)
        return carry

    lax.fori_loop(0, t // c, body, 0)

    @pl.when(l == pl.num_programs(2) - 1)
    def _():
        so_ref[...] = s_ref[...]


def _gdn(p, conv_w, conv_state, par, norm_w, s0, batch, seq, heads, tile):
    tile = min(tile, seq)
    nl = seq // tile
    cw = conv_w.shape[0]
    hb = GDN_HEADS_PER_STEP
    assert heads % hb == 0 and hb % 2 == 0
    groups = heads // hb
    width = hb * LANES
    pblk = lambda off: pl.BlockSpec((tile, width), lambda b, h, l, off=off: (b * nl + l, off * groups + h))
    wblk = lambda off: pl.BlockSpec((cw, width), lambda b, h, l, off=off: (0, off * groups + h))
    cblk = lambda off: pl.BlockSpec((None, cw - 1, width), lambda b, h, l, off=off: (b, 0, off * groups + h))
    st_spec = pl.BlockSpec((None, hb, LANES, LANES), lambda b, h, l: (b, h, 0, 0))
    tbuf = pltpu.VMEM((tile, width), F32)
    return pl.pallas_call(
        functools.partial(_gdn_kernel, heads=heads, conv_w=cw),
        grid=(batch, groups, nl),
        in_specs=[pblk(0), pblk(1), pblk(2), pblk(3),
                  pl.BlockSpec((tile, LANES), lambda b, h, l: (b * nl + l, 4 * heads)),
                  wblk(0), wblk(1), wblk(2), cblk(0), cblk(1), cblk(2),
                  pl.BlockSpec(par.shape, lambda b, h, l: (0, 0)),
                  pl.BlockSpec((1, LANES), lambda b, h, l: (0, 0)),
                  st_spec],
        out_specs=[pl.BlockSpec((tile, width), lambda b, h, l: (b * nl + l, h)), st_spec],
        out_shape=[jax.ShapeDtypeStruct((batch * seq, heads * LANES), BF16), jax.ShapeDtypeStruct(s0.shape, F32)],
        scratch_shapes=[pltpu.VMEM((hb, LANES, LANES), F32),
                        pltpu.VMEM((3, 16, width), F32),
                        tbuf, tbuf, tbuf],
        compiler_params=_params(("parallel", "parallel", "arbitrary")),
        name="gdn",
    )(p, p, p, p, p, conv_w, conv_w, conv_w, conv_state, conv_state, conv_state, par, norm_w, s0)


def _pad_cols(w, total):
    return jnp.pad(w, ((0, 0),) * (w.ndim - 1) + ((0, total - w.shape[-1]),))


def _ab_layout(a_cols, wb, lora_w, lora_a, lora_g):
    assert lora_w == LANES and lora_a == LANES and lora_g <= 512
    xg_off = -(-(3 * wb + 2 * LANES) // 512) * 512
    b_width = xg_off + 512
    assert a_cols % 512 == 0
    return xg_off, b_width


def _rwkv_cols(t, wb, xg_off, b_width, lora_g):
    head = t[..., :3 * wb + 2 * LANES]
    tail = t[..., 3 * wb + 2 * LANES:]
    z = lambda n: jnp.zeros(t.shape[:-1] + (n,), t.dtype)
    return jnp.concatenate([head, z(xg_off - head.shape[-1]), tail, z(b_width - xg_off - lora_g)], axis=-1)


def _trunk(x, mod, s_hgrn, s_rwkv, s_shift, s_gdn, s_conv, wts, tiles):
    batch, seq, d = x.shape
    m = batch * seq
    x2 = x.reshape(m, d)
    tm, t_hgrn, t_rwkv, t_gdn = tiles
    outs = {}
    depth = mod.shape[0]
    for layer in range(depth):
        j = layer // 2
        vecs6 = [mod[layer, :, i * d:(i + 1) * d].reshape(batch, 1, d) for i in range(6)]
        sh1, sc1, g1, sh2, sc2, g2 = vecs6
        if layer % 2 == 0:
            w = wts["ab"][j]
            p = _norm_matmul(x2, sc1, sh1, w["w_in"], seq, tm, 512)
            ha, wb = w["ha"], w["wb"]
            oa, st_a = _hgrn(p, w["lb"], w["hgrn_norm_w"], jnp.swapaxes(s_hgrn[j], -1, -2), batch, seq, ha, t_hgrn)
            pairs = wb // LANES
            hb = s_rwkv.shape[2]
            nb = s_rwkv.shape[-1]
            sp = s_rwkv[j].reshape(batch, pairs, 2, nb, nb)
            zero = jnp.zeros_like(sp[:, :, 0])
            s0 = jnp.concatenate([jnp.concatenate([sp[:, :, 0], zero], -1),
                                  jnp.concatenate([zero, sp[:, :, 1]], -1)], -2)
            shift = _rwkv_cols(s_shift[j], wb, w["xg_off"], w["b_width"], w["lora_g"])
            ob, st_b = _rwkv(p, shift, w["vecs"], w["mus"], w["w2"], w["a2"], w["g2"], s0,
                             batch, seq, pairs, w["a_cols"], t_rwkv)
            mix = jnp.concatenate([oa, ob], axis=-1)
            outs.setdefault("hgrn", []).append(jnp.swapaxes(st_a, -1, -2))
            outs.setdefault("rwkv", []).append(
                jnp.stack([st_b[:, :, :nb, :nb], st_b[:, :, nb:, nb:]], axis=2).reshape(batch, hb, nb, nb))
            last = p.reshape(batch, seq, -1)[:, seq - 1:, w["a_cols"]:]
            n_head = 3 * wb + 2 * LANES
            outs.setdefault("shift", []).append(
                jnp.concatenate([last[..., :n_head], last[..., w["xg_off"]:w["xg_off"] + w["lora_g"]]], axis=-1))
            w_out = w["w_out"]
        else:
            w = wts["gdn"][j]
            p = _norm_matmul(x2, sc1, sh1, w["w_in"], seq, tm, 512)
            hc = w["hc"]
            mix, st_c = _gdn(p, w["conv_w"], s_conv[j], w["par"], w["norm_w"], s_gdn[j], batch, seq, hc, t_gdn)
            outs.setdefault("gdn", []).append(st_c)
            cw = w["conv_w"].shape[0]
            raw = p.reshape(batch, seq, -1)[:, :, :3 * hc * LANES]
            prev = jnp.concatenate([s_conv[j], raw[:, max(seq - (cw - 1), 0):]], axis=1)
            outs.setdefault("conv", []).append(prev[:, -(cw - 1):])
            w_out = w["w_out"]
        x2 = _out_residual(mix, w_out, x2, g1, seq, tm, 512)
        x2 = _mlp(x2, sc2, sh2, g2, wts["mlp_w1"], wts["mlp_w2"], layer, seq, min(tm, 512), 512, 256)
    x2 = _final_norm(x2, wts["final_w"], 256)
    st = lambda name: jnp.stack(outs[name])
    return x2.reshape(batch, seq, d), st("hgrn"), st("rwkv"), st("shift"), st("gdn"), st("conv")


def kernel(x_prompt, x_sample, c_prompt, c_sample, state_hgrn, state_rwkv, state_rwkv_shift, state_gdn,
           state_gdn_conv, ada_w, ada_b, mlp_w1, mlp_w2, final_norm_w, ab_w_in, ab_w_out, hgrn_lb_logits,
           hgrn_norm_w, rwkv_mu, rwkv_w0, rwkv_w2, rwkv_a0, rwkv_a2, rwkv_g2, rwkv_k_k, rwkv_k_a, rwkv_r_k,
           rwkv_ln_w, rwkv_ln_b, gdn_w_in, gdn_w_out, gdn_conv_w, gdn_a_log, gdn_dt_bias, gdn_norm_w):
    d = x_prompt.shape[-1]
    nb_p, nb_s = x_prompt.shape[0], x_sample.shape[0]
    n_ab, n_c = ab_w_in.shape[0], gdn_w_in.shape[0]
    ha = state_hgrn.shape[2]
    a_cols = 4 * ha * LANES
    wb = rwkv_w0.shape[1]
    lora_w, lora_a, lora_g = rwkv_w2.shape[1], rwkv_a2.shape[1], rwkv_g2.shape[1]
    xg_off, b_width = _ab_layout(a_cols, wb, lora_w, lora_a, lora_g)
    hc = state_gdn.shape[2]

    rows = nb_p + nb_s
    rows_pad = -(-rows // 8) * 8
    c_all = jnp.pad(jnp.concatenate([c_prompt, c_sample], axis=0), ((0, rows_pad - rows), (0, 0)))
    mod = _adaln(c_all, ada_w, ada_b)

    lbs = jnp.cumsum(jax.nn.softmax(hgrn_lb_logits.astype(F32), axis=0), axis=0)
    wts = {"ab": [], "gdn": [], "final_w": final_norm_w.reshape(1, d),
           "mlp_w1": mlp_w1.astype(BF16), "mlp_w2": mlp_w2.astype(BF16)}
    for j in range(n_ab):
        hole = ((a_cols + 3 * wb + 2 * LANES) // LANES, (a_cols + xg_off) // LANES)
        w_in = _cast_cols(ab_w_in, j, a_cols + b_width, hole, hole[1] - hole[0])
        mu = _rwkv_cols(rwkv_mu[j][None], wb, xg_off, b_width, lora_g)[0]
        vec_rows = [mu[:wb], mu[wb:2 * wb], mu[2 * wb:3 * wb], rwkv_w0[j], rwkv_a0[j], rwkv_k_k[j], rwkv_k_a[j],
                    rwkv_r_k[j].reshape(-1), rwkv_ln_w[j], rwkv_ln_b[j]]
        vecs = jnp.pad(jnp.stack(vec_rows), ((0, 16 - len(vec_rows)), (0, 0)))
        mus = jnp.stack([_pad_cols(mu[3 * wb:3 * wb + LANES], 512), _pad_cols(mu[3 * wb + LANES:3 * wb + 2 * LANES], 512),
                         mu[xg_off:xg_off + 512]])
        wts["ab"].append({
            "w_in": w_in, "w_out": ab_w_out[j].astype(BF16), "lb": lbs[j][None],
            "hgrn_norm_w": hgrn_norm_w[j][None], "vecs": vecs, "mus": jnp.pad(mus, ((0, 5), (0, 0))),
            "w2": rwkv_w2[j].astype(BF16), "a2": rwkv_a2[j].astype(BF16),
            "g2": jnp.pad(rwkv_g2[j], ((0, 512 - lora_g), (0, 0))).astype(BF16),
            "ha": ha, "wb": wb, "a_cols": a_cols, "xg_off": xg_off, "b_width": b_width, "lora_g": lora_g})
    for j in range(n_c):
        cols = gdn_w_in.shape[-1]
        cols_pad = -(-cols // 512) * 512
        par = jnp.zeros((8, LANES), F32)
        par = par.at[0, hc:2 * hc].set(gdn_a_log[j]).at[1, hc:2 * hc].set(gdn_dt_bias[j])
        wts["gdn"].append({
            "w_in": _cast_cols(gdn_w_in, j, cols_pad), "w_out": gdn_w_out[j].astype(BF16),
            "conv_w": gdn_conv_w[j], "par": par, "norm_w": gdn_norm_w[j][None], "hc": hc})

    zeros = lambda s: jnp.zeros((s.shape[0], nb_p) + s.shape[2:], x_prompt.dtype)
    y_p, hg_p, rw_p, sh_p, gd_p, cv_p = _trunk(
        x_prompt, mod[:, :nb_p], zeros(state_hgrn), zeros(state_rwkv), zeros(state_rwkv_shift),
        zeros(state_gdn), zeros(state_gdn_conv), wts, PROMPT_TILES)
    y_s, hg_s, rw_s, sh_s, gd_s, cv_s = _trunk(
        x_sample, mod[:, nb_p:rows], state_hgrn, state_rwkv, state_rwkv_shift, state_gdn, state_gdn_conv,
        wts, SAMPLE_TILES)
    return (y_p, y_s, hg_p, rw_p, sh_p, gd_p, cv_p, hg_s, rw_s, sh_s, gd_s, cv_s)
```

```python
import functools
import math

import jax
import jax.numpy as jnp
from jax import lax
from jax.experimental import pallas as pl
from jax.experimental.pallas import tpu as pltpu

F32 = jnp.float32
BF16 = jnp.bfloat16

NORM_EPS = 1e-6
RWKV_LN_EPS = 64e-5
CHUNK = 64
DIAG = 8
LANES = 128
VMEM_LIMIT = 56 * 1024 * 1024
MLP_VMEM_LIMIT = 60 * 1024 * 1024
PROMPT_TILES = (1024, 512, 256, 256)
SAMPLE_TILES = (512, 64, 64, 64)
HGRN_HEADS_PER_STEP = 8
RWKV_PAIRS_PER_STEP = 16
GDN_HEADS_PER_STEP = 16


def _mm(a, b, ca, cb):
    return lax.dot_general(a.astype(BF16), b.astype(BF16), (((ca,), (cb,)), ((), ())), preferred_element_type=F32)


def _nn(a, b):
    return _mm(a, b, 1, 0)


def _nt(a, b):
    return _mm(a, b, 1, 1)


def _tn(a, b):
    return _mm(a, b, 0, 0)


def _sigmoid(x):
    return 1.0 / (1.0 + jnp.exp(-x))


def _silu(x):
    return x * _sigmoid(x)


def _softplus(x):
    return jnp.maximum(x, 0.0) + jnp.log(1.0 + jnp.exp(-jnp.abs(x)))


def _iota(shape, dim):
    return lax.broadcasted_iota(jnp.int32, shape, dim)


def _split2(x):
    hi = x.astype(BF16)
    return hi, (x - hi.astype(F32)).astype(BF16)


def _split3(x):
    hi = x.astype(BF16)
    r = x - hi.astype(F32)
    lo = r.astype(BF16)
    return hi, lo, (r - lo.astype(F32)).astype(BF16)


def _mm_hi(a, b, a_parts=None, b_parts=None):
    a_hi, a_lo = a_parts if a_parts is not None else _split2(a)
    b_hi, b_lo = b_parts if b_parts is not None else _split2(b)
    return jnp.dot(jnp.concatenate([a_hi, a_lo, a_hi], axis=1), jnp.concatenate([b_hi, b_hi, b_lo], axis=0),
                   preferred_element_type=F32)


def _mask_mm_left(mask3, x):
    return jnp.dot(mask3, jnp.concatenate(_split3(x), axis=0), preferred_element_type=F32)


def _nilpotent_solve(m, x, stages, hi_lo=True):
    n = m.shape[0]
    mm = _mm_hi if hi_lo else _nn
    for k in range(stages):
        last = k == stages - 1
        y = mm(m, x if last else jnp.concatenate([m, x], axis=1))
        yield
        if last:
            x = x + y
        else:
            m = y[:, :n]
            x = x + y[:, n:]
    return x


def _lockstep(generators):
    generators = list(generators)
    results = [None] * len(generators)
    live = list(range(len(generators)))
    while live:
        for i in list(live):
            try:
                next(generators[i])
            except StopIteration as stop:
                results[i] = stop.value
                live.remove(i)
    return results


def _params(sem, vmem=VMEM_LIMIT):
    return pltpu.CompilerParams(dimension_semantics=sem, vmem_limit_bytes=vmem)


def _rows_per_tile(rows_per_batch, tile):
    if tile <= rows_per_batch:
        assert rows_per_batch % tile == 0
        return 1
    assert tile % rows_per_batch == 0
    return tile // rows_per_batch


def _adaln_kernel(c_ref, w_ref, b_ref, o_ref):
    c = _silu(c_ref[...]).astype(BF16)
    o_ref[...] = jnp.dot(c, w_ref[...].astype(BF16), preferred_element_type=F32) + b_ref[...]


def _adaln(c, ada_w, ada_b):
    depth, d, n = ada_w.shape
    rows = c.shape[0]
    tn = 512 if n % 512 == 0 else n
    return pl.pallas_call(
        _adaln_kernel,
        grid=(depth, n // tn),
        in_specs=[pl.BlockSpec((rows, d), lambda l, j: (0, 0)),
                  pl.BlockSpec((None, d, tn), lambda l, j: (l, 0, j)),
                  pl.BlockSpec((None, 1, tn), lambda l, j: (l, 0, j))],
        out_specs=pl.BlockSpec((None, rows, tn), lambda l, j: (l, 0, j)),
        out_shape=jax.ShapeDtypeStruct((depth, rows, n), F32),
        compiler_params=_params(("parallel", "parallel")),
        name="adaln",
    )(c, ada_w, ada_b.reshape(depth, 1, n))


def _cast_cols_kernel(w_ref, o_ref, *, hole, shift, n_src):
    j = pl.program_id(0)
    src_col = jnp.where(j < hole[0], j, j - shift) * LANES + _iota((LANES, 1), 0)
    keep = (src_col < n_src) & jnp.logical_not((j >= hole[0]) & (j < hole[1]))
    o_ref[...] = jnp.where(keep, w_ref[...], 0.0).T.astype(o_ref.dtype)


def _cast_cols(w, layer, n_out, hole=(0, 0), shift=0):
    _, k, n_src = w.shape
    if hole[0] == hole[1]:
        hole = (n_out // LANES, n_out // LANES)
    last = (n_src - 1) // LANES
    return pl.pallas_call(
        functools.partial(_cast_cols_kernel, hole=hole, shift=shift, n_src=n_src),
        grid=(n_out // LANES,),
        in_specs=[pl.BlockSpec((None, LANES, k),
                               lambda j: (layer, jnp.clip(jnp.where(j < hole[0], j, j - shift), 0, last), 0))],
        out_specs=pl.BlockSpec((k, LANES), lambda j: (0, j)),
        out_shape=jax.ShapeDtypeStruct((k, n_out), BF16),
        compiler_params=_params(("parallel",)),
        name="cast_cols",
    )(jnp.swapaxes(w, 1, 2))


def _modulated_norm(x_ref, sc_ref, sh_ref, h_ref, groups):
    rows = x_ref.shape[0] // groups
    step = min(rows, 128)

    for gi in range(groups):
        sc = 1.0 + sc_ref[gi]
        sh = sh_ref[gi]

        def body(i, carry, gi=gi, sc=sc, sh=sh):
            sl = pl.ds(pl.multiple_of(gi * rows + i * step, step), step)
            x = x_ref[sl, :]
            ms = jnp.mean(x * x, axis=-1, keepdims=True)
            h_ref[sl, :] = (x * lax.rsqrt(ms + NORM_EPS) * sc + sh).astype(BF16)
            return carry

        lax.fori_loop(0, rows // step, body, 0)


def _norm_mm_kernel(x_ref, sc_ref, sh_ref, w_ref, o_ref, h_ref, *, groups):
    @pl.when(pl.program_id(1) == 0)
    def _():
        _modulated_norm(x_ref, sc_ref, sh_ref, h_ref, groups)

    o_ref[...] = jnp.dot(h_ref[...], w_ref[...], preferred_element_type=F32).astype(o_ref.dtype)


def _norm_matmul(x, scale, shift, w, rows_per_batch, tm, tn):
    m, d = x.shape
    n = w.shape[1]
    tm = min(tm, m)
    groups = _rows_per_tile(rows_per_batch, tm)
    bidx = (lambda i, j: (i * tm // rows_per_batch, 0, 0)) if groups == 1 else (lambda i, j: (i, 0, 0))
    return pl.pallas_call(
        functools.partial(_norm_mm_kernel, groups=groups),
        grid=(m // tm, n // tn),
        in_specs=[pl.BlockSpec((tm, d), lambda i, j: (i, 0), pipeline_mode=pl.Buffered(1)),
                  pl.BlockSpec((groups, 1, d), bidx),
                  pl.BlockSpec((groups, 1, d), bidx),
                  pl.BlockSpec((d, tn), lambda i, j: (0, j))],
        out_specs=pl.BlockSpec((tm, tn), lambda i, j: (i, j)),
        out_shape=jax.ShapeDtypeStruct((m, n), F32),
        scratch_shapes=[pltpu.VMEM((tm, d), BF16)],
        compiler_params=_params(("parallel", "arbitrary")),
        name="norm_matmul",
    )(x, scale, shift, w)


def _out_res_kernel(a_ref, w_ref, x_ref, g_ref, o_ref, *, groups):
    acc = jnp.dot(a_ref[...], w_ref[...], preferred_element_type=F32)
    rows = acc.shape[0] // groups
    for gi in range(groups):
        sl = slice(gi * rows, (gi + 1) * rows)
        o_ref[sl, :] = x_ref[sl, :] + g_ref[gi] * acc[sl, :]


def _out_residual(a, w, x, gate, rows_per_batch, tm, tn):
    m, k = a.shape
    n = w.shape[1]
    tm = min(tm, m)
    groups = _rows_per_tile(rows_per_batch, tm)
    bidx = (lambda i, j: (i * tm // rows_per_batch, 0, j)) if groups == 1 else (lambda i, j: (i, 0, j))
    return pl.pallas_call(
        functools.partial(_out_res_kernel, groups=groups),
        grid=(m // tm, n // tn),
        in_specs=[pl.BlockSpec((tm, k), lambda i, j: (i, 0)),
                  pl.BlockSpec((k, tn), lambda i, j: (0, j)),
                  pl.BlockSpec((tm, tn), lambda i, j: (i, j)),
                  pl.BlockSpec((groups, 1, tn), bidx)],
        out_specs=pl.BlockSpec((tm, tn), lambda i, j: (i, j)),
        out_shape=jax.ShapeDtypeStruct((m, n), F32),
        compiler_params=_params(("parallel", "parallel")),
        name="out_residual",
    )(a, w, x, gate)


def _mlp_kernel(x_ref, sc_ref, sh_ref, xr_ref, g_ref, w1_ref, w2_ref, o_ref, h_ref, u_ref, *, groups, nf):
    j = pl.program_id(1)

    @pl.when(j == 0)
    def _():
        _modulated_norm(x_ref, sc_ref, sh_ref, h_ref, groups)

    @pl.when(j < nf)
    def _():
        u = jnp.dot(h_ref[...], w1_ref[...], preferred_element_type=F32)
        u_ref[j] = jnp.square(jnp.maximum(u, 0.0)).astype(BF16)

    @pl.when(j >= nf)
    def _():
        tf = u_ref.shape[2]
        acc = jnp.dot(u_ref[0], w2_ref[0:tf, :], preferred_element_type=F32)
        for f in range(1, nf):
            acc += jnp.dot(u_ref[f], w2_ref[f * tf:(f + 1) * tf, :], preferred_element_type=F32)
        rows = acc.shape[0] // groups
        for gi in range(groups):
            sl = slice(gi * rows, (gi + 1) * rows)
            o_ref[sl, :] = xr_ref[sl, :] + g_ref[gi] * acc[sl, :]


def _mlp(x, scale, shift, gate, w1, w2, layer, rows_per_batch, tm, tf, tn):
    m, d = x.shape
    ff = w1.shape[2]
    tm = min(tm, m)
    nf, nn = ff // tf, d // tn
    groups = _rows_per_tile(rows_per_batch, tm)
    batch_of = (lambda i: i * tm // rows_per_batch) if groups == 1 else (lambda i: i)
    vec = pl.BlockSpec((groups, 1, d), lambda i, j: (batch_of(i), 0, 0))
    col = lambda j: jnp.maximum(j - nf, 0)
    return pl.pallas_call(
        functools.partial(_mlp_kernel, groups=groups, nf=nf),
        grid=(m // tm, nf + nn),
        in_specs=[pl.BlockSpec((tm, d), lambda i, j: (i, 0), pipeline_mode=pl.Buffered(1)),
                  vec, vec,
                  pl.BlockSpec((tm, tn), lambda i, j: (i, col(j))),
                  pl.BlockSpec((groups, 1, tn), lambda i, j: (batch_of(i), 0, col(j))),
                  pl.BlockSpec((None, d, tf), lambda i, j: (layer, 0, jnp.minimum(j, nf - 1))),
                  pl.BlockSpec((None, ff, tn), lambda i, j: (layer, 0, col(j)))],
        out_specs=pl.BlockSpec((tm, tn), lambda i, j: (i, col(j))),
        out_shape=jax.ShapeDtypeStruct((m, d), F32),
        scratch_shapes=[pltpu.VMEM((tm, d), BF16), pltpu.VMEM((nf, tm, tf), BF16)],
        compiler_params=_params(("parallel", "arbitrary"), MLP_VMEM_LIMIT),
        name="mlp",
    )(x, scale, shift, x, gate, w1, w2)


def _final_norm_kernel(x_ref, w_ref, o_ref):
    x = x_ref[...]
    ms = jnp.mean(x * x, axis=-1, keepdims=True)
    o_ref[...] = x * lax.rsqrt(ms + NORM_EPS) * w_ref[...]


def _final_norm(x, w, tm):
    m, d = x.shape
    tm = min(tm, m)
    return pl.pallas_call(
        _final_norm_kernel,
        grid=(m // tm,),
        in_specs=[pl.BlockSpec((tm, d), lambda i: (i, 0)), pl.BlockSpec((1, d), lambda i: (0, 0))],
        out_specs=pl.BlockSpec((tm, d), lambda i: (i, 0)),
        out_shape=jax.ShapeDtypeStruct((m, d), F32),
        compiler_params=_params(("parallel",)),
        name="final_norm",
    )(x, w)


def _hgrn_consts(c):
    row = _iota((c, c), 0)
    col = _iota((c, c), 1)
    ltri = (_iota((c, 3 * c), 0) >= _iota((c, 3 * c), 1) % c).astype(BF16)
    levels = []
    gs = 2 * DIAG
    while gs <= c:
        half = gs // 2
        pair = (row // gs == col // gs) & (row % gs >= half) & (col % gs < half)
        levels.append((gs, pair))
        gs *= 2
    rix = _iota((c, 1), 0)
    return ltri, levels, rix


def _hgrn_chunk(q, k, v, g, st, consts):
    ltri, levels, rix = consts
    c = q.shape[0]
    gc = _mask_mm_left(ltri, g)
    yield
    lhs, rhs = [q * jnp.exp(gc)], [st]
    for gs, _ in levels:
        half = gs // 2
        ref = jnp.concatenate(
            [jnp.broadcast_to(gc[m0 * gs + half - 1:m0 * gs + half, :], (gs, gc.shape[1])) for m0 in range(c // gs)],
            axis=0)
        is_q = (rix % gs) >= half
        lhs.append(jnp.where(is_q, q * jnp.exp(gc - ref), 0.0))
        rhs.append(jnp.where(is_q, 0.0, k * jnp.exp(ref - gc)))
        rhs.append(jnp.zeros((LANES - c, k.shape[1]), F32))
    prod = _nt(jnp.concatenate(lhs, axis=0), jnp.concatenate(rhs, axis=0))
    gl = gc[c - 1:c, :]
    upd = _tn(v, k * jnp.exp(gl - gc))
    yield
    att = jnp.zeros((c, c), F32)
    for li, (_, pair) in enumerate(levels):
        att = att + jnp.where(pair, prod[(li + 1) * c:(li + 2) * c, (li + 1) * LANES:(li + 1) * LANES + c], 0.0)
    o = prod[:c, :LANES] + _nn(att, v)
    in_block = lambda x, dist: pltpu.roll(x.reshape(c // DIAG, DIAG, x.shape[1]), dist, 1).reshape(x.shape)
    for dist in range(DIAG):
        if dist == 0:
            kd, gd, vd = k, gc, v
        else:
            kd, gd, vd = in_block(k, dist), in_block(gc, dist), in_block(v, dist)
        w = jnp.sum(q * kd * jnp.exp(gc - gd), axis=-1, keepdims=True)
        o = o + jnp.where((rix % DIAG) >= dist, w, 0.0) * vd
    yield
    return o, st * jnp.exp(gl) + upd


def _hgrn_kernel(q_ref, f_ref, i_ref, g_ref, lb_ref, nw_ref, s0_ref, o_ref, so_ref, s_ref, *, dk):
    l = pl.program_id(2)

    @pl.when(l == 0)
    def _():
        s_ref[...] = s0_ref[...]

    t, width = q_ref.shape
    c = min(CHUNK, t)
    consts = _hgrn_consts(c)
    lb = lb_ref[...]
    nw = nw_ref[...]

    def body(ci, carry):
        sl = pl.ds(pl.multiple_of(ci * c, c), c)
        f = lb + (1.0 - lb) * _sigmoid(f_ref[sl, :])
        q = _silu(q_ref[sl, :]) * dk ** -0.5
        k = 1.0 - f
        v = i_ref[sl, :]
        g = jnp.log(f)
        gate = _silu(g_ref[sl, :])
        heads = [slice(h * dk, (h + 1) * dk) for h in range(width // dk)]
        results = _lockstep(_hgrn_chunk(q[:, hs], k[:, hs], v[:, hs], g[:, hs], s_ref[h], consts)
                            for h, hs in enumerate(heads))
        for h, (o, st) in enumerate(results):
            s_ref[h] = st
            ms = jnp.mean(o * o, axis=-1, keepdims=True)
            o_ref[sl, heads[h]] = (o * lax.rsqrt(ms + NORM_EPS) * nw * gate[:, heads[h]]).astype(o_ref.dtype)
        return carry

    lax.fori_loop(0, t // c, body, 0)

    @pl.when(l == pl.num_programs(2) - 1)
    def _():
        so_ref[...] = s_ref[...]


def _hgrn(p, lb, norm_w, s0t, batch, seq, heads, tile):
    dk = LANES
    tile = min(tile, seq)
    nl = seq // tile
    hb = HGRN_HEADS_PER_STEP
    assert heads % hb == 0
    groups = heads // hb
    width = hb * dk
    blk = lambda off: pl.BlockSpec((tile, width), lambda b, h, l, off=off: (b * nl + l, off * groups + h))
    st_spec = pl.BlockSpec((None, hb, dk, dk), lambda b, h, l: (b, h, 0, 0))
    return pl.pallas_call(
        functools.partial(_hgrn_kernel, dk=dk),
        grid=(batch, groups, nl),
        in_specs=[blk(0), blk(1), blk(2), blk(3),
                  pl.BlockSpec((1, width), lambda b, h, l: (0, h)),
                  pl.BlockSpec((1, dk), lambda b, h, l: (0, 0)),
                  st_spec],
        out_specs=[pl.BlockSpec((tile, width), lambda b, h, l: (b * nl + l, h)), st_spec],
        out_shape=[jax.ShapeDtypeStruct((batch * seq, heads * dk), BF16),
                   jax.ShapeDtypeStruct(s0t.shape, F32)],
        scratch_shapes=[pltpu.VMEM((hb, dk, dk), F32)],
        compiler_params=_params(("parallel", "parallel", "arbitrary")),
        name="hgrn2",
    )(p, p, p, p, lb, norm_w, s0t)


def _rwkv_consts(c):
    n = 4 * c
    row = _iota((n, n), 0)
    col = _iota((n, n), 1)
    same_head = ((row // c) % 2) == ((col // c) % 2)
    strict = (row % c) > (col % c)
    incl = (row % c) >= (col % c)
    keep = same_head & (strict | ((row >= 2 * c) & incl))
    ltri = (_iota((c, 3 * c), 0) >= _iota((c, 3 * c), 1) % c).astype(BF16)
    head0 = _iota((1, LANES), 1) < (LANES // 2)
    rs = _iota((LANES, LANES), 0)
    cs = _iota((LANES, LANES), 1)
    bdiag = (rs // (LANES // 2)) == (cs // (LANES // 2))
    return keep, ltri, head0, bdiag


def _rwkv_chunk(r, lg, k, v, a, b, s, consts):
    keep, ltri, head0, bdiag = consts
    c = r.shape[0]
    gc = _mask_mm_left(ltri, lg)
    yield
    e_incl = jnp.exp(gc)
    e_inv = jnp.exp(-gc)
    at = a * jnp.exp(gc - lg)
    rt = r * e_incl
    bt = b * e_inv
    kt = k * e_inv
    split = lambda x: [jnp.where(head0, x, 0.0), jnp.where(head0, 0.0, x)]
    lhs = jnp.concatenate(split(at) + split(rt), axis=0)
    rhs = jnp.concatenate([bt, bt, kt, kt, s], axis=0)
    pls = _nt(lhs, rhs)
    yield
    p = jnp.where(keep, pls[:, :4 * c], 0.0)
    ls = pls[:, 4 * c:]
    vst = jnp.concatenate(split(v), axis=0)
    pv = _nn(p[:, 2 * c:], vst)
    yield
    ust = yield from _nilpotent_solve(p[:2 * c, :2 * c], ls[:2 * c] + pv[:2 * c], int(math.log2(c)), hi_lo=False)
    ost = ls[2 * c:] + pv[2 * c:] + _nn(p[2 * c:, :2 * c], ust)
    yield
    o = ost[:c] + ost[c:]
    u = ust[:c] + ust[c:]
    gl = gc[c - 1:c, :]
    e_tail = jnp.exp(gl - gc)
    upd = _tn(jnp.concatenate([u, v], axis=0), jnp.concatenate([b * e_tail, k * e_tail], axis=0))
    yield
    return o, s * jnp.exp(gl) + jnp.where(bdiag, upd, 0.0)


def _rwkv_kernel(xr_ref, xk_ref, xv_ref, xw_ref, xa_ref, xg_ref,
                 hr_ref, hk_ref, hv_ref, hw_ref, ha_ref, hg_ref,
                 vec_ref, mus_ref, w2_ref, a2_ref, g2_ref, s0_ref,
                 o_ref, so_ref,
                 s_ref, buf_ref, bufs_ref, bufg_ref, r_ref, lg_ref, k_ref, v_ref, a_ref, b_ref, y_ref):
    l = pl.program_id(2)
    t, width = xr_ref.shape
    c = min(CHUNK, t)

    @pl.when(l == 0)
    def _():
        s_ref[...] = s0_ref[...]
        for i, h_ref in enumerate((hr_ref, hk_ref, hv_ref)):
            buf_ref[i, 0:1, :] = h_ref[...]
        for i, h_ref in enumerate((hw_ref, ha_ref)):
            bufs_ref[i, 0:1, :] = h_ref[...]
        bufg_ref[0:1, :] = hg_ref[...]

    first_row = _iota((t, 1), 0) == 0

    def shifted(carry, x_ref, mu):
        x = x_ref[...]
        prev = jnp.where(first_row, carry[0:1, :], pltpu.roll(x, 1, 0))
        carry[0:1, :] = x[t - 1:t, :]
        return x + (prev - x) * mu

    vec = vec_ref[...]
    mus = mus_ref[...]
    r = shifted(buf_ref.at[0], xr_ref, vec[0:1])
    k = shifted(buf_ref.at[1], xk_ref, vec[1:2])
    v = shifted(buf_ref.at[2], xv_ref, vec[2:3])
    xw = shifted(bufs_ref.at[0], xw_ref, mus[0:1, :LANES])
    xa = shifted(bufs_ref.at[1], xa_ref, mus[1:2, :LANES])
    xg = shifted(bufg_ref, xg_ref, mus[2:3])

    w0, a0, k_k, k_a, r_k, ln_w, ln_b = (vec[i:i + 1] for i in range(3, 10))
    half = LANES // 2
    ones2 = ((_iota((2 * LANES, LANES), 0) % LANES) // half == _iota((2 * LANES, LANES), 1) // half).astype(BF16)

    def head_sums(x):
        outs = []
        for blk in range(width // LANES):
            outs.append(jnp.dot(jnp.concatenate(_split2(x[:, blk * LANES:(blk + 1) * LANES]), axis=1), ones2,
                                preferred_element_type=F32))
        return jnp.concatenate(outs, axis=1)

    w_log = -_softplus(-(w0 + _nn(jnp.tanh(xw), w2_ref[...]))) - 0.5
    a_lr = _sigmoid(a0 + _nn(xa, a2_ref[...]))
    gate = _nn(_sigmoid(xg), g2_ref[...])
    kk = k * k_k
    kk = kk * lax.rsqrt(head_sums(kk * kk) + 1e-12)
    k = k * (1.0 + (a_lr - 1.0) * k_a)
    r_ref[...] = r
    lg_ref[...] = -jnp.exp(w_log)
    k_ref[...] = k
    v_ref[...] = v
    a_ref[...] = -kk
    b_ref[...] = kk * a_lr

    consts = _rwkv_consts(c)

    def body(ci, carry):
        sl = pl.ds(pl.multiple_of(ci * c, c), c)
        operands = []
        for pi in range(width // LANES):
            cols = slice(pi * LANES, (pi + 1) * LANES)
            operands.append((r_ref[sl, cols], lg_ref[sl, cols], k_ref[sl, cols], v_ref[sl, cols],
                             a_ref[sl, cols], b_ref[sl, cols], s_ref[pi]))
        results = _lockstep(_rwkv_chunk(*ops, consts) for ops in operands)
        for pi, (o, s_new) in enumerate(results):
            s_ref[pi] = s_new
            y_ref[sl, pi * LANES:(pi + 1) * LANES] = o
        return carry

    lax.fori_loop(0, t // c, body, 0)

    o = y_ref[...]
    inv_n = 1.0 / half
    mean = head_sums(o) * inv_n
    dev = o - mean
    var = head_sums(dev * dev) * inv_n
    o = dev * lax.rsqrt(var + RWKV_LN_EPS) * ln_w + ln_b
    o = o + head_sums(r * k * r_k) * v
    o_ref[...] = (o * gate).astype(o_ref.dtype)

    @pl.when(l == pl.num_programs(2) - 1)
    def _():
        so_ref[...] = s_ref[...]


def _rwkv(p, shift, vecs, mus, w2, a2, g2, s0, batch, seq, pairs, col0, tile):
    tile = min(tile, seq)
    nl = seq // tile
    wb = pairs * LANES
    pb = RWKV_PAIRS_PER_STEP
    width = pb * LANES
    groups = pairs // pb
    assert pairs % pb == 0 and col0 % width == 0
    cb0 = col0 // LANES
    gb0 = col0 // width
    xg_off = -(-(3 * wb + 2 * LANES) // 512) * 512
    assert (col0 + xg_off) % 512 == 0
    pblk = lambda off: pl.BlockSpec((tile, width), lambda b, h, l, off=off: (b * nl + l, gb0 + off * groups + h))
    pfix = lambda cb: pl.BlockSpec((tile, LANES), lambda b, h, l, cb=cb: (b * nl + l, cb))
    hblk = lambda off: pl.BlockSpec((None, 1, width), lambda b, h, l, off=off: (b, 0, off * groups + h))
    hfix = lambda cb: pl.BlockSpec((None, 1, LANES), lambda b, h, l, cb=cb: (b, 0, cb))
    st_spec = pl.BlockSpec((None, pb, LANES, LANES), lambda b, h, l: (b, h, 0, 0))
    col = lambda rows: pl.BlockSpec((rows, width), lambda b, h, l: (0, h))
    full = lambda a: pl.BlockSpec(a.shape, lambda b, h, l: (0,) * a.ndim)
    tbuf = pltpu.VMEM((tile, width), F32)
    return pl.pallas_call(
        _rwkv_kernel,
        grid=(batch, groups, nl),
        in_specs=[pblk(0), pblk(1), pblk(2), pfix(cb0 + 3 * pairs), pfix(cb0 + 3 * pairs + 1),
                  pl.BlockSpec((tile, 512), lambda b, h, l: (b * nl + l, (col0 + xg_off) // 512)),
                  hblk(0), hblk(1), hblk(2), hfix(3 * pairs), hfix(3 * pairs + 1),
                  pl.BlockSpec((None, 1, 512), lambda b, h, l: (b, 0, xg_off // 512)),
                  col(vecs.shape[0]), full(mus), col(w2.shape[0]), col(a2.shape[0]), col(g2.shape[0]), st_spec],
        out_specs=[pl.BlockSpec((tile, width), lambda b, h, l: (b * nl + l, h)), st_spec],
        out_shape=[jax.ShapeDtypeStruct((batch * seq, wb), BF16), jax.ShapeDtypeStruct(s0.shape, F32)],
        scratch_shapes=[pltpu.VMEM((pb, LANES, LANES), F32),
                        pltpu.VMEM((3, 8, width), F32),
                        pltpu.VMEM((2, 8, LANES), F32),
                        pltpu.VMEM((8, 512), F32),
                        tbuf, tbuf, tbuf, tbuf, tbuf, tbuf, tbuf],
        compiler_params=_params(("parallel", "parallel", "arbitrary")),
        name="rwkv7",
    )(p, p, p, p, p, p, shift, shift, shift, shift, shift, shift, vecs, mus, w2, a2, g2, s0)


def _gdn_consts(c):
    n = 2 * c
    row = _iota((n, n), 0)
    col = _iota((n, n), 1)
    same = (row // c) == (col // c)
    incl = same & ((row % c) >= (col % c))
    strict = same & ((row % c) > (col % c))
    r3 = _iota((n, 3 * n), 0)
    c3 = _iota((n, 3 * n), 1) % n
    ltri3 = (((r3 // c) == (c3 // c)) & ((r3 % c) >= (c3 % c))).astype(BF16)
    return ltri3, incl, strict, strict.astype(F32)


def _gdn_pair_chunk(q, k, v, g, beta, states, consts):
    ltri3, incl, strict, strict_f = consts
    n = q.shape[0]
    c = n // 2
    cum = _mask_mm_left(ltri3, jnp.concatenate([g * strict_f, jnp.broadcast_to(g, (n, LANES))], axis=1))
    yield
    gc = cum[:, n:]
    dec = jnp.exp(jnp.where(incl, cum[:, :n], -1e30))
    eg = jnp.exp(gc)
    kq = _nt(jnp.concatenate([k, q], axis=0), k)
    head0 = _iota((n, 1), 0) < c
    head0_2 = (_iota((2 * n, 1), 0) % n) < c
    by_head = lambda x, m: jnp.concatenate([jnp.where(m, x, 0.0), jnp.where(m, 0.0, x)], axis=1)
    ks = _nn(by_head(jnp.concatenate([k * eg, q * eg], axis=0), head0_2), jnp.concatenate(states, axis=0))
    yield
    a_mat = jnp.where(strict, beta * kq[:n] * dec, 0.0)
    v_new = yield from _nilpotent_solve(-a_mat, beta * (v - ks[:n]), int(math.log2(c)))
    o = ks[n:] + _nn(kq[n:] * dec, v_new)
    yield
    gl = jnp.where(head0, gc[c - 1:c, :], gc[n - 1:n, :])
    upd = _tn(k * jnp.exp(gl - gc), by_head(v_new, head0))
    yield
    new_states = [s * jnp.exp(gc[(hi + 1) * c - 1:(hi + 1) * c, :]) + upd[:, hi * LANES:(hi + 1) * LANES]
                  for hi, s in enumerate(states)]
    return o, new_states


def _gdn_kernel(q_ref, k_ref, v_ref, z_ref, ba_ref, cwq_ref, cwk_ref, cwv_ref, cq_ref, ck_ref, cv_ref,
                par_ref, nw_ref, s0_ref, o_ref, so_ref, s_ref, buf_ref, qs_ref, ks_ref, vs_ref,
                *, heads, conv_w):
    hg = pl.program_id(1)
    l = pl.program_id(2)
    t, width = q_ref.shape
    hb = width // LANES
    c = min(CHUNK, t)
    pad = 8
    hist = conv_w - 1
    streams = ((q_ref, cq_ref, cwq_ref, qs_ref), (k_ref, ck_ref, cwk_ref, ks_ref), (v_ref, cv_ref, cwv_ref, vs_ref))

    @pl.when(l == 0)
    def _():
        s_ref[...] = s0_ref[...]
        for i, (_, c_ref, _, _) in enumerate(streams):
            buf_ref[i, pad - hist:pad, :] = c_ref[...]

    for i, (x_ref, _, w_ref, dst_ref) in enumerate(streams):
        x = x_ref[...]
        w = w_ref[...]
        y = x * w[hist:hist + 1]
        for shift in range(1, conv_w):
            y = y + pltpu.roll(x, shift, 0) * w[hist - shift:hist - shift + 1]
        dst_ref[...] = _silu(y)
        buf_ref[i, pad:2 * pad, :] = x[0:pad]
        head = jnp.zeros((pad, width), F32)
        for j in range(conv_w):
            head = head + buf_ref[i, pad - hist + j:2 * pad - hist + j, :] * w[j:j + 1]
        dst_ref[0:pad, :] = _silu(head)
        buf_ref[i, pad - hist:pad, :] = x[t - hist:t]

    par = par_ref[...]
    lane = _iota((1, LANES), 1)
    consts = _gdn_consts(c)
    nw = nw_ref[...]

    def body(ci, carry):
        sl = pl.ds(pl.multiple_of(ci * c, c), c)
        bac = ba_ref[sl, :]
        bac = jnp.where(lane < heads, _sigmoid(bac), -jnp.exp(par[0:1]) * _softplus(bac + par[1:2]))
        column = lambda idx: jnp.sum(jnp.where(lane == idx, bac, 0.0), axis=-1, keepdims=True)
        operands = []
        for pi in range(hb // 2):
            local = (2 * pi, 2 * pi + 1)
            cols = [slice(h * LANES, (h + 1) * LANES) for h in local]
            stack = lambda ref: jnp.concatenate([ref[sl, cs] for cs in cols], axis=0)
            beta = jnp.concatenate([column(hg * hb + h) for h in local], axis=0)
            g = jnp.concatenate([column(heads + hg * hb + h) for h in local], axis=0)
            operands.append((stack(qs_ref), stack(ks_ref), stack(vs_ref), g, beta, [s_ref[h] for h in local],
                             [z_ref[sl, cs] for cs in cols]))
        chains = []
        for q, k, v, g, beta, states, _ in operands:
            q = q * lax.rsqrt(jnp.sum(q * q, axis=-1, keepdims=True) + 1e-12) * LANES ** -0.5
            k = k * lax.rsqrt(jnp.sum(k * k, axis=-1, keepdims=True) + 1e-12)
            chains.append(_gdn_pair_chunk(q, k, v, g, beta, states, consts))
        results = _lockstep(chains)
        for pi, (o, new_states) in enumerate(results):
            z = operands[pi][6]
            for i in range(2):
                h = 2 * pi + i
                s_ref[h] = new_states[i]
                oh = o[i * c:(i + 1) * c]
                ms = jnp.mean(oh * oh, axis=-1, keepdims=True)
                o_ref[sl, h * LANES:(h + 1) * LANES] = (
                    oh * lax.rsqrt(ms + NORM_EPS) * nw * _silu(z[i])).astype(o_ref.dtype)
        return carry

    lax.fori_loop(0, t // c, body, 0)

    @pl.when(l == pl.num_programs(2) - 1)
    def _():
        so_ref[...] = s_ref[...]


def _gdn(p, conv_w, conv_state, par, norm_w, s0, batch, seq, heads, tile):
    tile = min(tile, seq)
    nl = seq // tile
    cw = conv_w.shape[0]
    hb = GDN_HEADS_PER_STEP
    assert heads % hb == 0 and hb % 2 == 0
    groups = heads // hb
    width = hb * LANES
    pblk = lambda off: pl.BlockSpec((tile, width), lambda b, h, l, off=off: (b * nl + l, off * groups + h))
    wblk = lambda off: pl.BlockSpec((cw, width), lambda b, h, l, off=off: (0, off * groups + h))
    cblk = lambda off: pl.BlockSpec((None, cw - 1, width), lambda b, h, l, off=off: (b, 0, off * groups + h))
    st_spec = pl.BlockSpec((None, hb, LANES, LANES), lambda b, h, l: (b, h, 0, 0))
    tbuf = pltpu.VMEM((tile, width), F32)
    return pl.pallas_call(
        functools.partial(_gdn_kernel, heads=heads, conv_w=cw),
        grid=(batch, groups, nl),
        in_specs=[pblk(0), pblk(1), pblk(2), pblk(3),
                  pl.BlockSpec((tile, LANES), lambda b, h, l: (b * nl + l, 4 * heads)),
                  wblk(0), wblk(1), wblk(2), cblk(0), cblk(1), cblk(2),
                  pl.BlockSpec(par.shape, lambda b, h, l: (0, 0)),
                  pl.BlockSpec((1, LANES), lambda b, h, l: (0, 0)),
                  st_spec],
        out_specs=[pl.BlockSpec((tile, width), lambda b, h, l: (b * nl + l, h)), st_spec],
        out_shape=[jax.ShapeDtypeStruct((batch * seq, heads * LANES), BF16), jax.ShapeDtypeStruct(s0.shape, F32)],
        scratch_shapes=[pltpu.VMEM((hb, LANES, LANES), F32),
                        pltpu.VMEM((3, 16, width), F32),
                        tbuf, tbuf, tbuf],
        compiler_params=_params(("parallel", "parallel", "arbitrary")),
        name="gdn",
    )(p, p, p, p, p, conv_w, conv_w, conv_w, conv_state, conv_state, conv_state, par, norm_w, s0)


def _pad_cols(w, total):
    return jnp.pad(w, ((0, 0),) * (w.ndim - 1) + ((0, total - w.shape[-1]),))


def _ab_layout(a_cols, wb, lora_w, lora_a, lora_g):
    assert lora_w == LANES and lora_a == LANES and lora_g <= 512
    xg_off = -(-(3 * wb + 2 * LANES) // 512) * 512
    b_width = xg_off + 512
    assert a_cols % 512 == 0
    return xg_off, b_width


def _rwkv_cols(t, wb, xg_off, b_width, lora_g):
    head = t[..., :3 * wb + 2 * LANES]
    tail = t[..., 3 * wb + 2 * LANES:]
    z = lambda n: jnp.zeros(t.shape[:-1] + (n,), t.dtype)
    return jnp.concatenate([head, z(xg_off - head.shape[-1]), tail, z(b_width - xg_off - lora_g)], axis=-1)


def _trunk(x, mod, s_hgrn, s_rwkv, s_shift, s_gdn, s_conv, wts, tiles):
    batch, seq, d = x.shape
    m = batch * seq
    x2 = x.reshape(m, d)
    tm, t_hgrn, t_rwkv, t_gdn = tiles
    outs = {}
    depth = mod.shape[0]
    for layer in range(depth):
        j = layer // 2
        vecs6 = [mod[layer, :, i * d:(i + 1) * d].reshape(batch, 1, d) for i in range(6)]
        sh1, sc1, g1, sh2, sc2, g2 = vecs6
        if layer % 2 == 0:
            w = wts["ab"][j]
            p = _norm_matmul(x2, sc1, sh1, w["w_in"], seq, tm, 512)
            ha, wb = w["ha"], w["wb"]
            oa, st_a = _hgrn(p, w["lb"], w["hgrn_norm_w"], jnp.swapaxes(s_hgrn[j], -1, -2), batch, seq, ha, t_hgrn)
            pairs = wb // LANES
            hb = s_rwkv.shape[2]
            nb = s_rwkv.shape[-1]
            sp = s_rwkv[j].reshape(batch, pairs, 2, nb, nb)
            zero = jnp.zeros_like(sp[:, :, 0])
            s0 = jnp.concatenate([jnp.concatenate([sp[:, :, 0], zero], -1),
                                  jnp.concatenate([zero, sp[:, :, 1]], -1)], -2)
            shift = _rwkv_cols(s_shift[j], wb, w["xg_off"], w["b_width"], w["lora_g"])
            ob, st_b = _rwkv(p, shift, w["vecs"], w["mus"], w["w2"], w["a2"], w["g2"], s0,
                             batch, seq, pairs, w["a_cols"], t_rwkv)
            mix = jnp.concatenate([oa, ob], axis=-1)
            outs.setdefault("hgrn", []).append(jnp.swapaxes(st_a, -1, -2))
            outs.setdefault("rwkv", []).append(
                jnp.stack([st_b[:, :, :nb, :nb], st_b[:, :, nb:, nb:]], axis=2).reshape(batch, hb, nb, nb))
            last = p.reshape(batch, seq, -1)[:, seq - 1:, w["a_cols"]:]
            n_head = 3 * wb + 2 * LANES
            outs.setdefault("shift", []).append(
                jnp.concatenate([last[..., :n_head], last[..., w["xg_off"]:w["xg_off"] + w["lora_g"]]], axis=-1))
            w_out = w["w_out"]
        else:
            w = wts["gdn"][j]
            p = _norm_matmul(x2, sc1, sh1, w["w_in"], seq, tm, 512)
            hc = w["hc"]
            mix, st_c = _gdn(p, w["conv_w"], s_conv[j], w["par"], w["norm_w"], s_gdn[j], batch, seq, hc, t_gdn)
            outs.setdefault("gdn", []).append(st_c)
            cw = w["conv_w"].shape[0]
            raw = p.reshape(batch, seq, -1)[:, :, :3 * hc * LANES]
            prev = jnp.concatenate([s_conv[j], raw[:, max(seq - (cw - 1), 0):]], axis=1)
            outs.setdefault("conv", []).append(prev[:, -(cw - 1):])
            w_out = w["w_out"]
        x2 = _out_residual(mix, w_out, x2, g1, seq, tm, 512)
        x2 = _mlp(x2, sc2, sh2, g2, wts["mlp_w1"], wts["mlp_w2"], layer, seq, min(tm, 512), 512, 256)
    x2 = _final_norm(x2, wts["final_w"], 256)
    st = lambda name: jnp.stack(outs[name])
    return x2.reshape(batch, seq, d), st("hgrn"), st("rwkv"), st("shift"), st("gdn"), st("conv")


def kernel(x_prompt, x_sample, c_prompt, c_sample, state_hgrn, state_rwkv, state_rwkv_shift, state_gdn,
           state_gdn_conv, ada_w, ada_b, mlp_w1, mlp_w2, final_norm_w, ab_w_in, ab_w_out, hgrn_lb_logits,
           hgrn_norm_w, rwkv_mu, rwkv_w0, rwkv_w2, rwkv_a0, rwkv_a2, rwkv_g2, rwkv_k_k, rwkv_k_a, rwkv_r_k,
           rwkv_ln_w, rwkv_ln_b, gdn_w_in, gdn_w_out, gdn_conv_w, gdn_a_log, gdn_dt_bias, gdn_norm_w):
    d = x_prompt.shape[-1]
    nb_p, nb_s = x_prompt.shape[0], x_sample.shape[0]
    n_ab, n_c = ab_w_in.shape[0], gdn_w_in.shape[0]
    ha = state_hgrn.shape[2]
    a_cols = 4 * ha * LANES
    wb = rwkv_w0.shape[1]
    lora_w, lora_a, lora_g = rwkv_w2.shape[1], rwkv_a2.shape[1], rwkv_g2.shape[1]
    xg_off, b_width = _ab_layout(a_cols, wb, lora_w, lora_a, lora_g)
    hc = state_gdn.shape[2]

    rows = nb_p + nb_s
    rows_pad = -(-rows // 8) * 8
    c_all = jnp.pad(jnp.concatenate([c_prompt, c_sample], axis=0), ((0, rows_pad - rows), (0, 0)))
    mod = _adaln(c_all, ada_w, ada_b)

    lbs = jnp.cumsum(jax.nn.softmax(hgrn_lb_logits.astype(F32), axis=0), axis=0)
    wts = {"ab": [], "gdn": [], "final_w": final_norm_w.reshape(1, d),
           "mlp_w1": mlp_w1.astype(BF16), "mlp_w2": mlp_w2.astype(BF16)}
    for j in range(n_ab):
        hole = ((a_cols + 3 * wb + 2 * LANES) // LANES, (a_cols + xg_off) // LANES)
        w_in = _cast_cols(ab_w_in, j, a_cols + b_width, hole, hole[1] - hole[0])
        mu = _rwkv_cols(rwkv_mu[j][None], wb, xg_off, b_width, lora_g)[0]
        vec_rows = [mu[:wb], mu[wb:2 * wb], mu[2 * wb:3 * wb], rwkv_w0[j], rwkv_a0[j], rwkv_k_k[j], rwkv_k_a[j],
                    rwkv_r_k[j].reshape(-1), rwkv_ln_w[j], rwkv_ln_b[j]]
        vecs = jnp.pad(jnp.stack(vec_rows), ((0, 16 - len(vec_rows)), (0, 0)))
        mus = jnp.stack([_pad_cols(mu[3 * wb:3 * wb + LANES], 512), _pad_cols(mu[3 * wb + LANES:3 * wb + 2 * LANES], 512),
                         mu[xg_off:xg_off + 512]])
        wts["ab"].append({
            "w_in": w_in, "w_out": ab_w_out[j].astype(BF16), "lb": lbs[j][None],
            "hgrn_norm_w": hgrn_norm_w[j][None], "vecs": vecs, "mus": jnp.pad(mus, ((0, 5), (0, 0))),
            "w2": rwkv_w2[j].astype(BF16), "a2": rwkv_a2[j].astype(BF16),
            "g2": jnp.pad(rwkv_g2[j], ((0, 512 - lora_g), (0, 0))).astype(BF16),
            "ha": ha, "wb": wb, "a_cols": a_cols, "xg_off": xg_off, "b_width": b_width, "lora_g": lora_g})
    for j in range(n_c):
        cols = gdn_w_in.shape[-1]
        cols_pad = -(-cols // 512) * 512
        par = jnp.zeros((8, LANES), F32)
        par = par.at[0, hc:2 * hc].set(gdn_a_log[j]).at[1, hc:2 * hc].set(gdn_dt_bias[j])
        wts["gdn"].append({
            "w_in": _cast_cols(gdn_w_in, j, cols_pad), "w_out": gdn_w_out[j].astype(BF16),
            "conv_w": gdn_conv_w[j], "par": par, "norm_w": gdn_norm_w[j][None], "hc": hc})

    zeros = lambda s: jnp.zeros((s.shape[0], nb_p) + s.shape[2:], x_prompt.dtype)
    y_p, hg_p, rw_p, sh_p, gd_p, cv_p = _trunk(
        x_prompt, mod[:, :nb_p], zeros(state_hgrn), zeros(state_rwkv), zeros(state_rwkv_shift),
        zeros(state_gdn), zeros(state_gdn_conv), wts, PROMPT_TILES)
    y_s, hg_s, rw_s, sh_s, gd_s, cv_s = _trunk(
        x_sample, mod[:, nb_p:rows], state_hgrn, state_rwkv, state_rwkv_shift, state_gdn, state_gdn_conv,
        wts, SAMPLE_TILES)
    return (y_p, y_s, hg_p, rw_p, sh_p, gd_p, cv_p, hg_s, rw_s, sh_s, gd_s, cv_s)
```

```python
import functools
import math

import jax
import jax.numpy as jnp
from jax import lax
from jax.experimental import pallas as pl
from jax.experimental.pallas import tpu as pltpu

F32 = jnp.float32
BF16 = jnp.bfloat16

NORM_EPS = 1e-6
RWKV_LN_EPS = 64e-5
CHUNK = 64
DIAG = 8
LANES = 128
VMEM_LIMIT = 56 * 1024 * 1024
MLP_VMEM_LIMIT = 60 * 1024 * 1024
PROMPT_TILES = (1024, 512, 256, 256)
SAMPLE_TILES = (512, 64, 64, 64)
HGRN_HEADS_PER_STEP = 8
RWKV_PAIRS_PER_STEP = 16
GDN_HEADS_PER_STEP = 16


def _mm(a, b, ca, cb):
    return lax.dot_general(a.astype(BF16), b.astype(BF16), (((ca,), (cb,)), ((), ())), preferred_element_type=F32)


def _nn(a, b):
    return _mm(a, b, 1, 0)


def _nt(a, b):
    return _mm(a, b, 1, 1)


def _tn(a, b):
    return _mm(a, b, 0, 0)


def _sigmoid(x):
    return 1.0 / (1.0 + jnp.exp(-x))


def _silu(x):
    return x * _sigmoid(x)


def _softplus(x):
    return jnp.maximum(x, 0.0) + jnp.log(1.0 + jnp.exp(-jnp.abs(x)))


def _iota(shape, dim):
    return lax.broadcasted_iota(jnp.int32, shape, dim)


def _split2(x):
    hi = x.astype(BF16)
    return hi, (x - hi.astype(F32)).astype(BF16)


def _split3(x):
    hi = x.astype(BF16)
    r = x - hi.astype(F32)
    lo = r.astype(BF16)
    return hi, lo, (r - lo.astype(F32)).astype(BF16)


def _mm_hi(a, b, a_parts=None, b_parts=None):
    a_hi, a_lo = a_parts if a_parts is not None else _split2(a)
    b_hi, b_lo = b_parts if b_parts is not None else _split2(b)
    return jnp.dot(jnp.concatenate([a_hi, a_lo, a_hi], axis=1), jnp.concatenate([b_hi, b_hi, b_lo], axis=0),
                   preferred_element_type=F32)


def _mask_mm_left(mask3, x):
    return jnp.dot(mask3, jnp.concatenate(_split3(x), axis=0), preferred_element_type=F32)


def _nilpotent_solve(m, x, stages, hi_lo=True):
    n = m.shape[0]
    mm = _mm_hi if hi_lo else _nn
    for k in range(stages):
        last = k == stages - 1
        y = mm(m, x if last else jnp.concatenate([m, x], axis=1))
        yield
        if last:
            x = x + y
        else:
            m = y[:, :n]
            x = x + y[:, n:]
    return x


def _lockstep(generators):
    generators = list(generators)
    results = [None] * len(generators)
    live = list(range(len(generators)))
    while live:
        for i in list(live):
            try:
                next(generators[i])
            except StopIteration as stop:
                results[i] = stop.value
                live.remove(i)
    return results


def _params(sem, vmem=VMEM_LIMIT):
    return pltpu.CompilerParams(dimension_semantics=sem, vmem_limit_bytes=vmem)


def _rows_per_tile(rows_per_batch, tile):
    if tile <= rows_per_batch:
        assert rows_per_batch % tile == 0
        return 1
    assert tile % rows_per_batch == 0
    return tile // rows_per_batch


def _adaln_kernel(c_ref, w_ref, b_ref, o_ref):
    c = _silu(c_ref[...]).astype(BF16)
    o_ref[...] = jnp.dot(c, w_ref[...].astype(BF16), preferred_element_type=F32) + b_ref[...]


def _adaln(c, ada_w, ada_b):
    depth, d, n = ada_w.shape
    rows = c.shape[0]
    tn = 512 if n % 512 == 0 else n
    return pl.pallas_call(
        _adaln_kernel,
        grid=(depth, n // tn),
        in_specs=[pl.BlockSpec((rows, d), lambda l, j: (0, 0)),
                  pl.BlockSpec((None, d, tn), lambda l, j: (l, 0, j)),
                  pl.BlockSpec((None, 1, tn), lambda l, j: (l, 0, j))],
        out_specs=pl.BlockSpec((None, rows, tn), lambda l, j: (l, 0, j)),
        out_shape=jax.ShapeDtypeStruct((depth, rows, n), F32),
        compiler_params=_params(("parallel", "parallel")),
        name="adaln",
    )(c, ada_w, ada_b.reshape(depth, 1, n))


def _cast_cols_kernel(w_ref, o_ref, *, hole, shift, n_src):
    j = pl.program_id(0)
    src_col = jnp.where(j < hole[0], j, j - shift) * LANES + _iota((LANES, 1), 0)
    keep = (src_col < n_src) & jnp.logical_not((j >= hole[0]) & (j < hole[1]))
    o_ref[...] = jnp.where(keep, w_ref[...], 0.0).T.astype(o_ref.dtype)


def _cast_cols(w, layer, n_out, hole=(0, 0), shift=0):
    _, k, n_src = w.shape
    if hole[0] == hole[1]:
        hole = (n_out // LANES, n_out // LANES)
    last = (n_src - 1) // LANES
    return pl.pallas_call(
        functools.partial(_cast_cols_kernel, hole=hole, shift=shift, n_src=n_src),
        grid=(n_out // LANES,),
        in_specs=[pl.BlockSpec((None, LANES, k),
                               lambda j: (layer, jnp.clip(jnp.where(j < hole[0], j, j - shift), 0, last), 0))],
        out_specs=pl.BlockSpec((k, LANES), lambda j: (0, j)),
        out_shape=jax.ShapeDtypeStruct((k, n_out), BF16),
        compiler_params=_params(("parallel",)),
        name="cast_cols",
    )(jnp.swapaxes(w, 1, 2))


def _modulated_norm(x_ref, sc_ref, sh_ref, h_ref, groups):
    rows = x_ref.shape[0] // groups
    step = min(rows, 128)

    for gi in range(groups):
        sc = 1.0 + sc_ref[gi]
        sh = sh_ref[gi]

        def body(i, carry, gi=gi, sc=sc, sh=sh):
            sl = pl.ds(pl.multiple_of(gi * rows + i * step, step), step)
            x = x_ref[sl, :]
            ms = jnp.mean(x * x, axis=-1, keepdims=True)
            h_ref[sl, :] = (x * lax.rsqrt(ms + NORM_EPS) * sc + sh).astype(BF16)
            return carry

        lax.fori_loop(0, rows // step, body, 0)


def _norm_mm_kernel(x_ref, sc_ref, sh_ref, w_ref, o_ref, h_ref, *, groups):
    @pl.when(pl.program_id(1) == 0)
    def _():
        _modulated_norm(x_ref, sc_ref, sh_ref, h_ref, groups)

    o_ref[...] = jnp.dot(h_ref[...], w_ref[...], preferred_element_type=F32).astype(o_ref.dtype)


def _norm_matmul(x, scale, shift, w, rows_per_batch, tm, tn):
    m, d = x.shape
    n = w.shape[1]
    tm = min(tm, m)
    groups = _rows_per_tile(rows_per_batch, tm)
    bidx = (lambda i, j: (i * tm // rows_per_batch, 0, 0)) if groups == 1 else (lambda i, j: (i, 0, 0))
    return pl.pallas_call(
        functools.partial(_norm_mm_kernel, groups=groups),
        grid=(m // tm, n // tn),
        in_specs=[pl.BlockSpec((tm, d), lambda i, j: (i, 0), pipeline_mode=pl.Buffered(1)),
                  pl.BlockSpec((groups, 1, d), bidx),
                  pl.BlockSpec((groups, 1, d), bidx),
                  pl.BlockSpec((d, tn), lambda i, j: (0, j))],
        out_specs=pl.BlockSpec((tm, tn), lambda i, j: (i, j)),
        out_shape=jax.ShapeDtypeStruct((m, n), F32),
        scratch_shapes=[pltpu.VMEM((tm, d), BF16)],
        compiler_params=_params(("parallel", "arbitrary")),
        name="norm_matmul",
    )(x, scale, shift, w)


def _out_res_kernel(a_ref, w_ref, x_ref, g_ref, o_ref, *, groups):
    acc = jnp.dot(a_ref[...], w_ref[...], preferred_element_type=F32)
    rows = acc.shape[0] // groups
    for gi in range(groups):
        sl = slice(gi * rows, (gi + 1) * rows)
        o_ref[sl, :] = x_ref[sl, :] + g_ref[gi] * acc[sl, :]


def _out_residual(a, w, x, gate, rows_per_batch, tm, tn):
    m, k = a.shape
    n = w.shape[1]
    tm = min(tm, m)
    groups = _rows_per_tile(rows_per_batch, tm)
    bidx = (lambda i, j: (i * tm // rows_per_batch, 0, j)) if groups == 1 else (lambda i, j: (i, 0, j))
    return pl.pallas_call(
        functools.partial(_out_res_kernel, groups=groups),
        grid=(m // tm, n // tn),
        in_specs=[pl.BlockSpec((tm, k), lambda i, j: (i, 0)),
                  pl.BlockSpec((k, tn), lambda i, j: (0, j)),
                  pl.BlockSpec((tm, tn), lambda i, j: (i, j)),
                  pl.BlockSpec((groups, 1, tn), bidx)],
        out_specs=pl.BlockSpec((tm, tn), lambda i, j: (i, j)),
        out_shape=jax.ShapeDtypeStruct((m, n), F32),
        compiler_params=_params(("parallel", "parallel")),
        name="out_residual",
    )(a, w, x, gate)


def _mlp_kernel(x_ref, sc_ref, sh_ref, xr_ref, g_ref, w1_ref, w2_ref, o_ref, h_ref, u_ref, *, groups, nf):
    j = pl.program_id(1)

    @pl.when(j == 0)
    def _():
        _modulated_norm(x_ref, sc_ref, sh_ref, h_ref, groups)

    @pl.when(j < nf)
    def _():
        u = jnp.dot(h_ref[...], w1_ref[...], preferred_element_type=F32)
        u_ref[j] = jnp.square(jnp.maximum(u, 0.0)).astype(BF16)

    @pl.when(j >= nf)
    def _():
        tf = u_ref.shape[2]
        acc = jnp.dot(u_ref[0], w2_ref[0:tf, :], preferred_element_type=F32)
        for f in range(1, nf):
            acc += jnp.dot(u_ref[f], w2_ref[f * tf:(f + 1) * tf, :], preferred_element_type=F32)
        rows = acc.shape[0] // groups
        for gi in range(groups):
            sl = slice(gi * rows, (gi + 1) * rows)
            o_ref[sl, :] = xr_ref[sl, :] + g_ref[gi] * acc[sl, :]


def _mlp(x, scale, shift, gate, w1, w2, layer, rows_per_batch, tm, tf, tn):
    m, d = x.shape
    ff = w1.shape[2]
    tm = min(tm, m)
    nf, nn = ff // tf, d // tn
    groups = _rows_per_tile(rows_per_batch, tm)
    batch_of = (lambda i: i * tm // rows_per_batch) if groups == 1 else (lambda i: i)
    vec = pl.BlockSpec((groups, 1, d), lambda i, j: (batch_of(i), 0, 0))
    col = lambda j: jnp.maximum(j - nf, 0)
    return pl.pallas_call(
        functools.partial(_mlp_kernel, groups=groups, nf=nf),
        grid=(m // tm, nf + nn),
        in_specs=[pl.BlockSpec((tm, d), lambda i, j: (i, 0), pipeline_mode=pl.Buffered(1)),
                  vec, vec,
                  pl.BlockSpec((tm, tn), lambda i, j: (i, col(j))),
                  pl.BlockSpec((groups, 1, tn), lambda i, j: (batch_of(i), 0, col(j))),
                  pl.BlockSpec((None, d, tf), lambda i, j: (layer, 0, jnp.minimum(j, nf - 1))),
                  pl.BlockSpec((None, ff, tn), lambda i, j: (layer, 0, col(j)))],
        out_specs=pl.BlockSpec((tm, tn), lambda i, j: (i, col(j))),
        out_shape=jax.ShapeDtypeStruct((m, d), F32),
        scratch_shapes=[pltpu.VMEM((tm, d), BF16), pltpu.VMEM((nf, tm, tf), BF16)],
        compiler_params=_params(("parallel", "arbitrary"), MLP_VMEM_LIMIT),
        name="mlp",
    )(x, scale, shift, x, gate, w1, w2)


def _final_norm_kernel(x_ref, w_ref, o_ref):
    x = x_ref[...]
    ms = jnp.mean(x * x, axis=-1, keepdims=True)
    o_ref[...] = x * lax.rsqrt(ms + NORM_EPS) * w_ref[...]


def _final_norm(x, w, tm):
    m, d = x.shape
    tm = min(tm, m)
    return pl.pallas_call(
        _final_norm_kernel,
        grid=(m // tm,),
        in_specs=[pl.BlockSpec((tm, d), lambda i: (i, 0)), pl.BlockSpec((1, d), lambda i: (0, 0))],
        out_specs=pl.BlockSpec((tm, d), lambda i: (i, 0)),
        out_shape=jax.ShapeDtypeStruct((m, d), F32),
        compiler_params=_params(("parallel",)),
        name="final_norm",
    )(x, w)


def _hgrn_consts(c):
    row = _iota((c, c), 0)
    col = _iota((c, c), 1)
    ltri = (_iota((c, 3 * c), 0) >= _iota((c, 3 * c), 1) % c).astype(BF16)
    levels = []
    gs = 2 * DIAG
    while gs <= c:
        half = gs // 2
        pair = (row // gs == col // gs) & (row % gs >= half) & (col % gs < half)
        levels.append((gs, pair))
        gs *= 2
    rix = _iota((c, 1), 0)
    return ltri, levels, rix


def _hgrn_chunk(q, k, v, g, st, consts):
    ltri, levels, rix = consts
    c = q.shape[0]
    gc = _mask_mm_left(ltri, g)
    yield
    lhs, rhs = [q * jnp.exp(gc)], [st]
    for gs, _ in levels:
        half = gs // 2
        ref = jnp.concatenate(
            [jnp.broadcast_to(gc[m0 * gs + half - 1:m0 * gs + half, :], (gs, gc.shape[1])) for m0 in range(c // gs)],
            axis=0)
        is_q = (rix % gs) >= half
        lhs.append(jnp.where(is_q, q * jnp.exp(gc - ref), 0.0))
        rhs.append(jnp.where(is_q, 0.0, k * jnp.exp(ref - gc)))
        rhs.append(jnp.zeros((LANES - c, k.shape[1]), F32))
    prod = _nt(jnp.concatenate(lhs, axis=0), jnp.concatenate(rhs, axis=0))
    gl = gc[c - 1:c, :]
    upd = _tn(v, k * jnp.exp(gl - gc))
    yield
    att = jnp.zeros((c, c), F32)
    for li, (_, pair) in enumerate(levels):
        att = att + jnp.where(pair, prod[(li + 1) * c:(li + 2) * c, (li + 1) * LANES:(li + 1) * LANES + c], 0.0)
    o = prod[:c, :LANES] + _nn(att, v)
    in_block = lambda x, dist: pltpu.roll(x.reshape(c // DIAG, DIAG, x.shape[1]), dist, 1).reshape(x.shape)
    for dist in range(DIAG):
        if dist == 0:
            kd, gd, vd = k, gc, v
        else:
            kd, gd, vd = in_block(k, dist), in_block(gc, dist), in_block(v, dist)
        w = jnp.sum(q * kd * jnp.exp(gc - gd), axis=-1, keepdims=True)
        o = o + jnp.where((rix % DIAG) >= dist, w, 0.0) * vd
    yield
    return o, st * jnp.exp(gl) + upd


def _hgrn_kernel(q_ref, f_ref, i_ref, g_ref, lb_ref, nw_ref, s0_ref, o_ref, so_ref, s_ref, *, dk):
    l = pl.program_id(2)

    @pl.when(l == 0)
    def _():
        s_ref[...] = s0_ref[...]

    t, width = q_ref.shape
    c = min(CHUNK, t)
    consts = _hgrn_consts(c)
    lb = lb_ref[...]
    nw = nw_ref[...]

    def body(ci, carry):
        sl = pl.ds(pl.multiple_of(ci * c, c), c)
        f = lb + (1.0 - lb) * _sigmoid(f_ref[sl, :])
        q = _silu(q_ref[sl, :]) * dk ** -0.5
        k = 1.0 - f
        v = i_ref[sl, :]
        g = jnp.log(f)
        gate = _silu(g_ref[sl, :])
        heads = [slice(h * dk, (h + 1) * dk) for h in range(width // dk)]
        results = _lockstep(_hgrn_chunk(q[:, hs], k[:, hs], v[:, hs], g[:, hs], s_ref[h], consts)
                            for h, hs in enumerate(heads))
        for h, (o, st) in enumerate(results):
            s_ref[h] = st
            ms = jnp.mean(o * o, axis=-1, keepdims=True)
            o_ref[sl, heads[h]] = (o * lax.rsqrt(ms + NORM_EPS) * nw * gate[:, heads[h]]).astype(o_ref.dtype)
        return carry

    lax.fori_loop(0, t // c, body, 0)

    @pl.when(l == pl.num_programs(2) - 1)
    def _():
        so_ref[...] = s_ref[...]


def _hgrn(p, lb, norm_w, s0t, batch, seq, heads, tile):
    dk = LANES
    tile = min(tile, seq)
    nl = seq // tile
    hb = HGRN_HEADS_PER_STEP
    assert heads % hb == 0
    groups = heads // hb
    width = hb * dk
    blk = lambda off: pl.BlockSpec((tile, width), lambda b, h, l, off=off: (b * nl + l, off * groups + h))
    st_spec = pl.BlockSpec((None, hb, dk, dk), lambda b, h, l: (b, h, 0, 0))
    return pl.pallas_call(
        functools.partial(_hgrn_kernel, dk=dk),
        grid=(batch, groups, nl),
        in_specs=[blk(0), blk(1), blk(2), blk(3),
                  pl.BlockSpec((1, width), lambda b, h, l: (0, h)),
                  pl.BlockSpec((1, dk), lambda b, h, l: (0, 0)),
                  st_spec],
        out_specs=[pl.BlockSpec((tile, width), lambda b, h, l: (b * nl + l, h)), st_spec],
        out_shape=[jax.ShapeDtypeStruct((batch * seq, heads * dk), BF16),
                   jax.ShapeDtypeStruct(s0t.shape, F32)],
        scratch_shapes=[pltpu.VMEM((hb, dk, dk), F32)],
        compiler_params=_params(("parallel", "parallel", "arbitrary")),
        name="hgrn2",
    )(p, p, p, p, lb, norm_w, s0t)


def _rwkv_consts(c):
    n = 4 * c
    row = _iota((n, n), 0)
    col = _iota((n, n), 1)
    same_head = ((row // c) % 2) == ((col // c) % 2)
    strict = (row % c) > (col % c)
    incl = (row % c) >= (col % c)
    keep = same_head & (strict | ((row >= 2 * c) & incl))
    ltri = (_iota((c, 3 * c), 0) >= _iota((c, 3 * c), 1) % c).astype(BF16)
    head0 = _iota((1, LANES), 1) < (LANES // 2)
    rs = _iota((LANES, LANES), 0)
    cs = _iota((LANES, LANES), 1)
    bdiag = (rs // (LANES // 2)) == (cs // (LANES // 2))
    return keep, ltri, head0, bdiag


def _rwkv_chunk(r, lg, k, v, a, b, s, consts):
    keep, ltri, head0, bdiag = consts
    c = r.shape[0]
    gc = _mask_mm_left(ltri, lg)
    yield
    e_incl = jnp.exp(gc)
    e_inv = jnp.exp(-gc)
    at = a * jnp.exp(gc - lg)
    rt = r * e_incl
    bt = b * e_inv
    kt = k * e_inv
    split = lambda x: [jnp.where(head0, x, 0.0), jnp.where(head0, 0.0, x)]
    lhs = jnp.concatenate(split(at) + split(rt), axis=0)
    rhs = jnp.concatenate([bt, bt, kt, kt, s], axis=0)
    pls = _nt(lhs, rhs)
    yield
    p = jnp.where(keep, pls[:, :4 * c], 0.0)
    ls = pls[:, 4 * c:]
    vst = jnp.concatenate(split(v), axis=0)
    pv = _nn(p[:, 2 * c:], vst)
    yield
    ust = yield from _nilpotent_solve(p[:2 * c, :2 * c], ls[:2 * c] + pv[:2 * c], int(math.log2(c)), hi_lo=False)
    ost = ls[2 * c:] + pv[2 * c:] + _nn(p[2 * c:, :2 * c], ust)
    yield
    o = ost[:c] + ost[c:]
    u = ust[:c] + ust[c:]
    gl = gc[c - 1:c, :]
    e_tail = jnp.exp(gl - gc)
    upd = _tn(jnp.concatenate([u, v], axis=0), jnp.concatenate([b * e_tail, k * e_tail], axis=0))
    yield
    return o, s * jnp.exp(gl) + jnp.where(bdiag, upd, 0.0)


def _rwkv_kernel(xr_ref, xk_ref, xv_ref, xw_ref, xa_ref, xg_ref,
                 hr_ref, hk_ref, hv_ref, hw_ref, ha_ref, hg_ref,
                 vec_ref, mus_ref, w2_ref, a2_ref, g2_ref, s0_ref,
                 o_ref, so_ref,
                 s_ref, buf_ref, bufs_ref, bufg_ref, r_ref, lg_ref, k_ref, v_ref, a_ref, b_ref, y_ref):
    l = pl.program_id(2)
    t, width = xr_ref.shape
    c = min(CHUNK, t)

    @pl.when(l == 0)
    def _():
        s_ref[...] = s0_ref[...]
        for i, h_ref in enumerate((hr_ref, hk_ref, hv_ref)):
            buf_ref[i, 0:1, :] = h_ref[...]
        for i, h_ref in enumerate((hw_ref, ha_ref)):
            bufs_ref[i, 0:1, :] = h_ref[...]
        bufg_ref[0:1, :] = hg_ref[...]

    first_row = _iota((c, 1), 0) == 0
    vec = vec_ref[...]
    mus = mus_ref[...]
    w0, a0, k_k, k_a, r_k, ln_w, ln_b = (vec[i:i + 1] for i in range(3, 10))
    half = LANES // 2
    ones2 = ((_iota((2 * LANES, LANES), 0) % LANES) // half == _iota((2 * LANES, LANES), 1) // half).astype(BF16)

    def head_sums(x):
        outs = []
        for blk in range(width // LANES):
            outs.append(jnp.dot(jnp.concatenate(_split2(x[:, blk * LANES:(blk + 1) * LANES]), axis=1), ones2,
                                preferred_element_type=F32))
        return jnp.concatenate(outs, axis=1)

    consts = _rwkv_consts(c)
    inv_n = 1.0 / half

    def body(ci, carry):
        sl = pl.ds(pl.multiple_of(ci * c, c), c)

        def shifted(cref, x_ref, mu):
            x = x_ref[sl, :]
            prev = jnp.where(first_row, cref[0:1, :], pltpu.roll(x, 1, 0))
            cref[0:1, :] = x[c - 1:c, :]
            return x + (prev - x) * mu

        r = shifted(buf_ref.at[0], xr_ref, vec[0:1])
        k = shifted(buf_ref.at[1], xk_ref, vec[1:2])
        v = shifted(buf_ref.at[2], xv_ref, vec[2:3])
        xw = shifted(bufs_ref.at[0], xw_ref, mus[0:1, :LANES])
        xa = shifted(bufs_ref.at[1], xa_ref, mus[1:2, :LANES])
        xg = shifted(bufg_ref, xg_ref, mus[2:3])
        w_log = -_softplus(-(w0 + _nn(jnp.tanh(xw), w2_ref[...]))) - 0.5
        a_lr = _sigmoid(a0 + _nn(xa, a2_ref[...]))
        gate = _nn(_sigmoid(xg), g2_ref[...])
        kk = k * k_k
        kk = kk * lax.rsqrt(head_sums(kk * kk) + 1e-12)
        k = k * (1.0 + (a_lr - 1.0) * k_a)
        lg = -jnp.exp(w_log)
        a = -kk
        b = kk * a_lr
        cols = [slice(pi * LANES, (pi + 1) * LANES) for pi in range(width // LANES)]
        results = _lockstep(_rwkv_chunk(r[:, cs], lg[:, cs], k[:, cs], v[:, cs], a[:, cs], b[:, cs], s_ref[pi], consts)
                            for pi, cs in enumerate(cols))
        for pi, (_, s_new) in enumerate(results):
            s_ref[pi] = s_new
        o = jnp.concatenate([res[0] for res in results], axis=1)
        mean = head_sums(o) * inv_n
        dev = o - mean
        var = head_sums(dev * dev) * inv_n
        o = dev * lax.rsqrt(var + RWKV_LN_EPS) * ln_w + ln_b
        o = o + head_sums(r * k * r_k) * v
        o_ref[sl, :] = (o * gate).astype(o_ref.dtype)
        return carry

    lax.fori_loop(0, t // c, body, 0)

    @pl.when(l == pl.num_programs(2) - 1)
    def _():
        so_ref[...] = s_ref[...]


def _rwkv(p, shift, vecs, mus, w2, a2, g2, s0, batch, seq, pairs, col0, tile):
    tile = min(tile, seq)
    nl = seq // tile
    wb = pairs * LANES
    pb = RWKV_PAIRS_PER_STEP
    width = pb * LANES
    groups = pairs // pb
    assert pairs % pb == 0 and col0 % width == 0
    cb0 = col0 // LANES
    gb0 = col0 // width
    xg_off = -(-(3 * wb + 2 * LANES) // 512) * 512
    assert (col0 + xg_off) % 512 == 0
    pblk = lambda off: pl.BlockSpec((tile, width), lambda b, h, l, off=off: (b * nl + l, gb0 + off * groups + h))
    pfix = lambda cb: pl.BlockSpec((tile, LANES), lambda b, h, l, cb=cb: (b * nl + l, cb))
    hblk = lambda off: pl.BlockSpec((None, 1, width), lambda b, h, l, off=off: (b, 0, off * groups + h))
    hfix = lambda cb: pl.BlockSpec((None, 1, LANES), lambda b, h, l, cb=cb: (b, 0, cb))
    st_spec = pl.BlockSpec((None, pb, LANES, LANES), lambda b, h, l: (b, h, 0, 0))
    col = lambda rows: pl.BlockSpec((rows, width), lambda b, h, l: (0, h))
    full = lambda a: pl.BlockSpec(a.shape, lambda b, h, l: (0,) * a.ndim)
    tbuf = pltpu.VMEM((tile, width), F32)
    return pl.pallas_call(
        _rwkv_kernel,
        grid=(batch, groups, nl),
        in_specs=[pblk(0), pblk(1), pblk(2), pfix(cb0 + 3 * pairs), pfix(cb0 + 3 * pairs + 1),
                  pl.BlockSpec((tile, 512), lambda b, h, l: (b * nl + l, (col0 + xg_off) // 512)),
                  hblk(0), hblk(1), hblk(2), hfix(3 * pairs), hfix(3 * pairs + 1),
                  pl.BlockSpec((None, 1, 512), lambda b, h, l: (b, 0, xg_off // 512)),
                  col(vecs.shape[0]), full(mus), col(w2.shape[0]), col(a2.shape[0]), col(g2.shape[0]), st_spec],
        out_specs=[pl.BlockSpec((tile, width), lambda b, h, l: (b * nl + l, h)), st_spec],
        out_shape=[jax.ShapeDtypeStruct((batch * seq, wb), BF16), jax.ShapeDtypeStruct(s0.shape, F32)],
        scratch_shapes=[pltpu.VMEM((pb, LANES, LANES), F32),
                        pltpu.VMEM((3, 8, width), F32),
                        pltpu.VMEM((2, 8, LANES), F32),
                        pltpu.VMEM((8, 512), F32),
                        tbuf, tbuf, tbuf, tbuf, tbuf, tbuf, tbuf],
        compiler_params=_params(("parallel", "parallel", "arbitrary")),
        name="rwkv7",
    )(p, p, p, p, p, p, shift, shift, shift, shift, shift, shift, vecs, mus, w2, a2, g2, s0)


def _gdn_consts(c):
    n = 2 * c
    row = _iota((n, n), 0)
    col = _iota((n, n), 1)
    same = (row // c) == (col // c)
    incl = same & ((row % c) >= (col % c))
    strict = same & ((row % c) > (col % c))
    r3 = _iota((n, 3 * n), 0)
    c3 = _iota((n, 3 * n), 1) % n
    ltri3 = (((r3 // c) == (c3 // c)) & ((r3 % c) >= (c3 % c))).astype(BF16)
    return ltri3, incl, strict, strict.astype(F32)


def _gdn_pair_chunk(q, k, v, g, beta, states, consts):
    ltri3, incl, strict, strict_f = consts
    n = q.shape[0]
    c = n // 2
    cum = _mask_mm_left(ltri3, jnp.concatenate([g * strict_f, jnp.broadcast_to(g, (n, LANES))], axis=1))
    yield
    gc = cum[:, n:]
    dec = jnp.exp(jnp.where(incl, cum[:, :n], -1e30))
    eg = jnp.exp(gc)
    kq = _nt(jnp.concatenate([k, q], axis=0), k)
    head0 = _iota((n, 1), 0) < c
    head0_2 = (_iota((2 * n, 1), 0) % n) < c
    by_head = lambda x, m: jnp.concatenate([jnp.where(m, x, 0.0), jnp.where(m, 0.0, x)], axis=1)
    ks = _nn(by_head(jnp.concatenate([k * eg, q * eg], axis=0), head0_2), jnp.concatenate(states, axis=0))
    yield
    a_mat = jnp.where(strict, beta * kq[:n] * dec, 0.0)
    v_new = yield from _nilpotent_solve(-a_mat, beta * (v - ks[:n]), int(math.log2(c)))
    o = ks[n:] + _nn(kq[n:] * dec, v_new)
    yield
    gl = jnp.where(head0, gc[c - 1:c, :], gc[n - 1:n, :])
    upd = _tn(k * jnp.exp(gl - gc), by_head(v_new, head0))
    yield
    new_states = [s * jnp.exp(gc[(hi + 1) * c - 1:(hi + 1) * c, :]) + upd[:, hi * LANES:(hi + 1) * LANES]
                  for hi, s in enumerate(states)]
    return o, new_states


def _gdn_kernel(q_ref, k_ref, v_ref, z_ref, ba_ref, cwq_ref, cwk_ref, cwv_ref, cq_ref, ck_ref, cv_ref,
                par_ref, nw_ref, s0_ref, o_ref, so_ref, s_ref, buf_ref, qs_ref, ks_ref, vs_ref,
                *, heads, conv_w):
    hg = pl.program_id(1)
    l = pl.program_id(2)
    t, width = q_ref.shape
    hb = width // LANES
    c = min(CHUNK, t)
    pad = 8
    hist = conv_w - 1
    streams = ((q_ref, cq_ref, cwq_ref, qs_ref), (k_ref, ck_ref, cwk_ref, ks_ref), (v_ref, cv_ref, cwv_ref, vs_ref))

    @pl.when(l == 0)
    def _():
        s_ref[...] = s0_ref[...]
        for i, (_, c_ref, _, _) in enumerate(streams):
            buf_ref[i, pad - hist:pad, :] = c_ref[...]

    for i, (x_ref, _, w_ref, dst_ref) in enumerate(streams):
        x = x_ref[...]
        w = w_ref[...]
        y = x * w[hist:hist + 1]
        for shift in range(1, conv_w):
            y = y + pltpu.roll(x, shift, 0) * w[hist - shift:hist - shift + 1]
        dst_ref[...] = _silu(y)
        buf_ref[i, pad:2 * pad, :] = x[0:pad]
        head = jnp.zeros((pad, width), F32)
        for j in range(conv_w):
            head = head + buf_ref[i, pad - hist + j:2 * pad - hist + j, :] * w[j:j + 1]
        dst_ref[0:pad, :] = _silu(head)
        buf_ref[i, pad - hist:pad, :] = x[t - hist:t]

    par = par_ref[...]
    lane = _iota((1, LANES), 1)
    consts = _gdn_consts(c)
    nw = nw_ref[...]

    def body(ci, carry):
        sl = pl.ds(pl.multiple_of(ci * c, c), c)
        bac = ba_ref[sl, :]
        bac = jnp.where(lane < heads, _sigmoid(bac), -jnp.exp(par[0:1]) * _softplus(bac + par[1:2]))
        column = lambda idx: jnp.sum(jnp.where(lane == idx, bac, 0.0), axis=-1, keepdims=True)
        operands = []
        for pi in range(hb // 2):
            local = (2 * pi, 2 * pi + 1)
            cols = [slice(h * LANES, (h + 1) * LANES) for h in local]
            stack = lambda ref: jnp.concatenate([ref[sl, cs] for cs in cols], axis=0)
            beta = jnp.concatenate([column(hg * hb + h) for h in local], axis=0)
            g = jnp.concatenate([column(heads + hg * hb + h) for h in local], axis=0)
            operands.append((stack(qs_ref), stack(ks_ref), stack(vs_ref), g, beta, [s_ref[h] for h in local],
                             [z_ref[sl, cs] for cs in cols]))
        chains = []
        for q, k, v, g, beta, states, _ in operands:
            q = q * lax.rsqrt(jnp.sum(q * q, axis=-1, keepdims=True) + 1e-12) * LANES ** -0.5
            k = k * lax.rsqrt(jnp.sum(k * k, axis=-1, keepdims=True) + 1e-12)
            chains.append(_gdn_pair_chunk(q, k, v, g, beta, states, consts))
        results = _lockstep(chains)
        for pi, (o, new_states) in enumerate(results):
            z = operands[pi][6]
            for i in range(2):
                h = 2 * pi + i
                s_ref[h] = new_states[i]
                oh = o[i * c:(i + 1) * c]
                ms = jnp.mean(oh * oh, axis=-1, keepdims=True)
                o_ref[sl, h * LANES:(h + 1) * LANES] = (
                    oh * lax.rsqrt(ms + NORM_EPS) * nw * _silu(z[i])).astype(o_ref.dtype)
        return carry

    lax.fori_loop(0, t // c, body, 0)

    @pl.when(l == pl.num_programs(2) - 1)
    def _():
        so_ref[...] = s_ref[...]


def _gdn(p, conv_w, conv_state, par, norm_w, s0, batch, seq, heads, tile):
    tile = min(tile, seq)
    nl = seq // tile
    cw = conv_w.shape[0]
    hb = GDN_HEADS_PER_STEP
    assert heads % hb == 0 and hb % 2 == 0
    groups = heads // hb
    width = hb * LANES
    pblk = lambda off: pl.BlockSpec((tile, width), lambda b, h, l, off=off: (b * nl + l, off * groups + h))
    wblk = lambda off: pl.BlockSpec((cw, width), lambda b, h, l, off=off: (0, off * groups + h))
    cblk = lambda off: pl.BlockSpec((None, cw - 1, width), lambda b, h, l, off=off: (b, 0, off * groups + h))
    st_spec = pl.BlockSpec((None, hb, LANES, LANES), lambda b, h, l: (b, h, 0, 0))
    tbuf = pltpu.VMEM((tile, width), F32)
    return pl.pallas_call(
        functools.partial(_gdn_kernel, heads=heads, conv_w=cw),
        grid=(batch, groups, nl),
        in_specs=[pblk(0), pblk(1), pblk(2), pblk(3),
                  pl.BlockSpec((tile, LANES), lambda b, h, l: (b * nl + l, 4 * heads)),
                  wblk(0), wblk(1), wblk(2), cblk(0), cblk(1), cblk(2),
                  pl.BlockSpec(par.shape, lambda b, h, l: (0, 0)),
                  pl.BlockSpec((1, LANES), lambda b, h, l: (0, 0)),
                  st_spec],
        out_specs=[pl.BlockSpec((tile, width), lambda b, h, l: (b * nl + l, h)), st_spec],
        out_shape=[jax.ShapeDtypeStruct((batch * seq, heads * LANES), BF16), jax.ShapeDtypeStruct(s0.shape, F32)],
        scratch_shapes=[pltpu.VMEM((hb, LANES, LANES), F32),
                        pltpu.VMEM((3, 16, width), F32),
                        tbuf, tbuf, tbuf],
        compiler_params=_params(("parallel", "parallel", "arbitrary")),
        name="gdn",
    )(p, p, p, p, p, conv_w, conv_w, conv_w, conv_state, conv_state, conv_state, par, norm_w, s0)


def _pad_cols(w, total):
    return jnp.pad(w, ((0, 0),) * (w.ndim - 1) + ((0, total - w.shape[-1]),))


def _ab_layout(a_cols, wb, lora_w, lora_a, lora_g):
    assert lora_w == LANES and lora_a == LANES and lora_g <= 512
    xg_off = -(-(3 * wb + 2 * LANES) // 512) * 512
    b_width = xg_off + 512
    assert a_cols % 512 == 0
    return xg_off, b_width


def _rwkv_cols(t, wb, xg_off, b_width, lora_g):
    head = t[..., :3 * wb + 2 * LANES]
    tail = t[..., 3 * wb + 2 * LANES:]
    z = lambda n: jnp.zeros(t.shape[:-1] + (n,), t.dtype)
    return jnp.concatenate([head, z(xg_off - head.shape[-1]), tail, z(b_width - xg_off - lora_g)], axis=-1)


def _trunk(x, mod, s_hgrn, s_rwkv, s_shift, s_gdn, s_conv, wts, tiles):
    batch, seq, d = x.shape
    m = batch * seq
    x2 = x.reshape(m, d)
    tm, t_hgrn, t_rwkv, t_gdn = tiles
    outs = {}
    depth = mod.shape[0]
    for layer in range(depth):
        j = layer // 2
        vecs6 = [mod[layer, :, i * d:(i + 1) * d].reshape(batch, 1, d) for i in range(6)]
        sh1, sc1, g1, sh2, sc2, g2 = vecs6
        if layer % 2 == 0:
            w = wts["ab"][j]
            p = _norm_matmul(x2, sc1, sh1, w["w_in"], seq, tm, 512)
            ha, wb = w["ha"], w["wb"]
            oa, st_a = _hgrn(p, w["lb"], w["hgrn_norm_w"], jnp.swapaxes(s_hgrn[j], -1, -2), batch, seq, ha, t_hgrn)
            pairs = wb // LANES
            hb = s_rwkv.shape[2]
            nb = s_rwkv.shape[-1]
            sp = s_rwkv[j].reshape(batch, pairs, 2, nb, nb)
            zero = jnp.zeros_like(sp[:, :, 0])
            s0 = jnp.concatenate([jnp.concatenate([sp[:, :, 0], zero], -1),
                                  jnp.concatenate([zero, sp[:, :, 1]], -1)], -2)
            shift = _rwkv_cols(s_shift[j], wb, w["xg_off"], w["b_width"], w["lora_g"])
            ob, st_b = _rwkv(p, shift, w["vecs"], w["mus"], w["w2"], w["a2"], w["g2"], s0,
                             batch, seq, pairs, w["a_cols"], t_rwkv)
            mix = jnp.concatenate([oa, ob], axis=-1)
            outs.setdefault("hgrn", []).append(jnp.swapaxes(st_a, -1, -2))
            outs.setdefault("rwkv", []).append(
                jnp.stack([st_b[:, :, :nb, :nb], st_b[:, :, nb:, nb:]], axis=2).reshape(batch, hb, nb, nb))
            last = p.reshape(batch, seq, -1)[:, seq - 1:, w["a_cols"]:]
            n_head = 3 * wb + 2 * LANES
            outs.setdefault("shift", []).append(
                jnp.concatenate([last[..., :n_head], last[..., w["xg_off"]:w["xg_off"] + w["lora_g"]]], axis=-1))
            w_out = w["w_out"]
        else:
            w = wts["gdn"][j]
            p = _norm_matmul(x2, sc1, sh1, w["w_in"], seq, tm, 512)
            hc = w["hc"]
            mix, st_c = _gdn(p, w["conv_w"], s_conv[j], w["par"], w["norm_w"], s_gdn[j], batch, seq, hc, t_gdn)
            outs.setdefault("gdn", []).append(st_c)
            cw = w["conv_w"].shape[0]
            raw = p.reshape(batch, seq, -1)[:, :, :3 * hc * LANES]
            prev = jnp.concatenate([s_conv[j], raw[:, max(seq - (cw - 1), 0):]], axis=1)
            outs.setdefault("conv", []).append(prev[:, -(cw - 1):])
            w_out = w["w_out"]
        x2 = _out_residual(mix, w_out, x2, g1, seq, tm, 512)
        x2 = _mlp(x2, sc2, sh2, g2, wts["mlp_w1"], wts["mlp_w2"], layer, seq, min(tm, 512), 512, 256)
    x2 = _final_norm(x2, wts["final_w"], 256)
    st = lambda name: jnp.stack(outs[name])
    return x2.reshape(batch, seq, d), st("hgrn"), st("rwkv"), st("shift"), st("gdn"), st("conv")


def kernel(x_prompt, x_sample, c_prompt, c_sample, state_hgrn, state_rwkv, state_rwkv_shift, state_gdn,
           state_gdn_conv, ada_w, ada_b, mlp_w1, mlp_w2, final_norm_w, ab_w_in, ab_w_out, hgrn_lb_logits,
           hgrn_norm_w, rwkv_mu, rwkv_w0, rwkv_w2, rwkv_a0, rwkv_a2, rwkv_g2, rwkv_k_k, rwkv_k_a, rwkv_r_k,
           rwkv_ln_w, rwkv_ln_b, gdn_w_in, gdn_w_out, gdn_conv_w, gdn_a_log, gdn_dt_bias, gdn_norm_w):
    d = x_prompt.shape[-1]
    nb_p, nb_s = x_prompt.shape[0], x_sample.shape[0]
    n_ab, n_c = ab_w_in.shape[0], gdn_w_in.shape[0]
    ha = state_hgrn.shape[2]
    a_cols = 4 * ha * LANES
    wb = rwkv_w0.shape[1]
    lora_w, lora_a, lora_g = rwkv_w2.shape[1], rwkv_a2.shape[1], rwkv_g2.shape[1]
    xg_off, b_width = _ab_layout(a_cols, wb, lora_w, lora_a, lora_g)
    hc = state_gdn.shape[2]

    rows = nb_p + nb_s
    rows_pad = -(-rows // 8) * 8
    c_all = jnp.pad(jnp.concatenate([c_prompt, c_sample], axis=0), ((0, rows_pad - rows), (0, 0)))
    mod = _adaln(c_all, ada_w, ada_b)

    lbs = jnp.cumsum(jax.nn.softmax(hgrn_lb_logits.astype(F32), axis=0), axis=0)
    wts = {"ab": [], "gdn": [], "final_w": final_norm_w.reshape(1, d),
           "mlp_w1": mlp_w1.astype(BF16), "mlp_w2": mlp_w2.astype(BF16)}
    for j in range(n_ab):
        hole = ((a_cols + 3 * wb + 2 * LANES) // LANES, (a_cols + xg_off) // LANES)
        w_in = _cast_cols(ab_w_in, j, a_cols + b_width, hole, hole[1] - hole[0])
        mu = _rwkv_cols(rwkv_mu[j][None], wb, xg_off, b_width, lora_g)[0]
        vec_rows = [mu[:wb], mu[wb:2 * wb], mu[2 * wb:3 * wb], rwkv_w0[j], rwkv_a0[j], rwkv_k_k[j], rwkv_k_a[j],
                    rwkv_r_k[j].reshape(-1), rwkv_ln_w[j], rwkv_ln_b[j]]
        vecs = jnp.pad(jnp.stack(vec_rows), ((0, 16 - len(vec_rows)), (0, 0)))
        mus = jnp.stack([_pad_cols(mu[3 * wb:3 * wb + LANES], 512), _pad_cols(mu[3 * wb + LANES:3 * wb + 2 * LANES], 512),
                         mu[xg_off:xg_off + 512]])
        wts["ab"].append({
            "w_in": w_in, "w_out": ab_w_out[j].astype(BF16), "lb": lbs[j][None],
            "hgrn_norm_w": hgrn_norm_w[j][None], "vecs": vecs, "mus": jnp.pad(mus, ((0, 5), (0, 0))),
            "w2": rwkv_w2[j].astype(BF16), "a2": rwkv_a2[j].astype(BF16),
            "g2": jnp.pad(rwkv_g2[j], ((0, 512 - lora_g), (0, 0))).astype(BF16),
            "ha": ha, "wb": wb, "a_cols": a_cols, "xg_off": xg_off, "b_width": b_width, "lora_g": lora_g})
    for j in range(n_c):
        cols = gdn_w_in.shape[-1]
        cols_pad = -(-cols // 512) * 512
        par = jnp.zeros((8, LANES), F32)
        par = par.at[0, hc:2 * hc].set(gdn_a_log[j]).at[1, hc:2 * hc].set(gdn_dt_bias[j])
        wts["gdn"].append({
            "w_in": _cast_cols(gdn_w_in, j, cols_pad), "w_out": gdn_w_out[j].astype(BF16),
            "conv_w": gdn_conv_w[j], "par": par, "norm_w": gdn_norm_w[j][None], "hc": hc})

    zeros = lambda s: jnp.zeros((s.shape[0], nb_p) + s.shape[2:], x_prompt.dtype)
    y_p, hg_p, rw_p, sh_p, gd_p, cv_p = _trunk(
        x_prompt, mod[:, :nb_p], zeros(state_hgrn), zeros(state_rwkv), zeros(state_rwkv_shift),
        zeros(state_gdn), zeros(state_gdn_conv), wts, PROMPT_TILES)
    y_s, hg_s, rw_s, sh_s, gd_s, cv_s = _trunk(
        x_sample, mod[:, nb_p:rows], state_hgrn, state_rwkv, state_rwkv_shift, state_gdn, state_gdn_conv,
        wts, SAMPLE_TILES)
    return (y_p, y_s, hg_p, rw_p, sh_p, gd_p, cv_p, hg_s, rw_s, sh_s, gd_s, cv_s)
```
